```python
import math, functools
import jax, jax.numpy as jnp
from jax import lax
import numpy as np

D_MODEL = 1024
BATCH = 4
SEQ = 4096
DEPTH = 2
DEC_BATCH = 32
DEC_SEQ = 1
PAST_LEN = 16384
PAGE_SIZE = 128

DN_HEADS = 4
DN_DK = 128
DN_DV = 128
DN_WIDTH = DN_HEADS * DN_DK
DN_CONV = 4
DN_CHUNK = 64
S5_WIDTH = 512
S5_GROUP = 16
S5_GROUPS = S5_WIDTH // S5_GROUP
S5_STATE = 64
MB_HEADS = 4
MB_HD = 128
MB_WIDTH = MB_HEADS * MB_HD
MB_BLOCK = 256
MB_TOPK = 3
MB_QCHUNK = 32
N_EXPERTS = 64
TOP_K = 8
EXPERT_FF = 256
SHARED_FF = 256
ROUTED_SCALE = 2.5
MOE_TBLOCK = 512
DEEPNORM_ALPHA = (2 * DEPTH) ** 0.25
DEEPNORM_BETA = (8 * DEPTH) ** -0.25
LN_EPS = 1e-5
NORM_EPS = 1e-6
IN_SIZES = (DN_WIDTH,) * 4 + (DN_HEADS,) * 2 + (S5_WIDTH,) + (MB_WIDTH,) * 3 + (D_MODEL,) * 3
IN_DIM = sum(IN_SIZES)
IN_OFFSETS = tuple(int(o) for o in np.cumsum(IN_SIZES)[:-1])

kernel_name = 'hybrid_deltanet_s5_moba_moe_decoder_step'

F32 = jnp.float32


def layer_norm(x, g, b):
    xf = x.astype(F32)
    mu = jnp.mean(xf, -1, keepdims=True)
    var = jnp.mean(jnp.square(xf - mu), -1, keepdims=True)
    return ((xf - mu) * lax.rsqrt(var + LN_EPS) * g.astype(F32) + b.astype(F32)).astype(x.dtype)


def l2norm(x):
    return x * lax.rsqrt(jnp.sum(x * x, -1, keepdims=True) + NORM_EPS)


def causal_conv(x, buf, w):
    L = x.shape[1]
    xp = jnp.concatenate([buf.astype(x.dtype), x], axis=1)
    y = xp[:, :L] * w[0]
    for j in range(1, DN_CONV):
        y = y + xp[:, j:j + L] * w[j]
    return y, xp[:, L:]


def gated_delta_rule(q, k, v, beta, g, S0):
    N, L, H, _ = q.shape
    C = min(DN_CHUNK, L)
    Lp = -(-L // C) * C
    pad = Lp - L

    def prep(t):
        t = jnp.pad(t, [(0, 0), (0, pad)] + [(0, 0)] * (t.ndim - 2))
        t = jnp.moveaxis(t, 2, 1)
        return t.reshape(N, H, Lp // C, C, *t.shape[3:])

    q, k, v, beta, g = map(prep, (q, k, v, beta, g))
    Gc = jnp.cumsum(g, axis=-1)
    idx = jnp.arange(C)
    causal = idx[:, None] >= idx[None, :]
    strict = idx[:, None] > idx[None, :]
    decay_c = jnp.exp(jnp.where(causal, Gc[..., :, None] - Gc[..., None, :], -jnp.inf))
    kb = k * beta[..., None]
    A = jnp.where(strict, jnp.einsum('nhcid,nhcjd->nhcij', kb, k) * decay_c, 0.0)
    rhs = jnp.concatenate([v * beta[..., None], kb * jnp.exp(Gc)[..., None]], axis=-1)
    sol = lax.linalg.triangular_solve(jnp.eye(C, dtype=F32) + A, rhs, left_side=True, lower=True)
    u, w = sol[..., :DN_DV], sol[..., DN_DV:]
    qk = jnp.einsum('nhcid,nhcjd->nhcij', q, k) * decay_c
    q_dec = q * jnp.exp(Gc)[..., None]
    k_dec = k * jnp.exp(Gc[..., -1:] - Gc)[..., None]
    g_last = jnp.exp(Gc[..., -1])

    def step(S, xs):
        u_c, w_c, qk_c, qd_c, kd_c, gl_c = xs
        v_new = u_c - jnp.einsum('nhik,nhkv->nhiv', w_c, S)
        o = jnp.einsum('nhik,nhkv->nhiv', qd_c, S) + jnp.einsum('nhij,nhjv->nhiv', qk_c, v_new)
        S = S * gl_c[..., None, None] + jnp.einsum('nhik,nhiv->nhkv', kd_c, v_new)
        return S, o

    xs = tuple(jnp.moveaxis(t, 2, 0) for t in (u, w, qk, q_dec, k_dec, g_last))
    S, o = lax.scan(step, S0, xs)
    o = jnp.moveaxis(o, 0, 2).reshape(N, H, Lp, DN_DV)[:, :, :L]
    return jnp.moveaxis(o, 1, 2), S


def deltanet_branch(q, k, v, z, b, a, conv_buf, S0, conv_w, a_log, dt_bias, norm_w):
    N, L, _ = q.shape
    dtype = q.dtype
    qkv, conv_new = causal_conv(jnp.concatenate([q, k, v], axis=-1), conv_buf, conv_w)
    qkv = jax.nn.silu(qkv.astype(F32))
    q, k, v = jnp.split(qkv, 3, axis=-1)
    q = l2norm(q.reshape(N, L, DN_HEADS, DN_DK)) * DN_DK ** -0.5
    k = l2norm(k.reshape(N, L, DN_HEADS, DN_DK))
    v = v.reshape(N, L, DN_HEADS, DN_DV)
    beta = jax.nn.sigmoid(b.astype(F32))
    g = -jnp.exp(a_log.astype(F32)) * jax.nn.softplus(a.astype(F32) + dt_bias.astype(F32))
    o, S = gated_delta_rule(q, k, v, beta, g, S0.astype(F32))
    o = o * lax.rsqrt(jnp.mean(o * o, -1, keepdims=True) + NORM_EPS) * norm_w.astype(F32)
    o = o * jax.nn.silu(z.reshape(N, L, DN_HEADS, DN_DV).astype(F32))
    return o.reshape(N, L, DN_WIDTH).astype(dtype), conv_new, S.astype(S0.dtype)


def s5_branch(u, s0_re, s0_im, a_re, a_im, log_dt, b_re, b_im, c_re, c_im, d, glu_w, glu_b):
    N, L, _ = u.shape
    uf = u.astype(F32)
    x = uf.reshape(N, L, S5_GROUPS, S5_GROUP)
    dt = jnp.exp(log_dt.astype(F32))[:, None]
    ar, ai = a_re.astype(F32), a_im.astype(F32)
    mag = jnp.exp(ar * dt)
    abr, abi = mag * jnp.cos(ai * dt), mag * jnp.sin(ai * dt)
    den = ar * ar + ai * ai
    fr = ((abr - 1.0) * ar + abi * ai) / den
    fi = (abi * ar - (abr - 1.0) * ai) / den
    br, bi = b_re.astype(F32), b_im.astype(F32)
    bbr = fr[..., None] * br - fi[..., None] * bi
    bbi = fr[..., None] * bi + fi[..., None] * br
    bu_r = jnp.einsum('nlgh,gph->nlgp', x, bbr)
    bu_i = jnp.einsum('nlgh,gph->nlgp', x, bbi)
    sr0, si0 = s0_re.astype(F32), s0_im.astype(F32)
    bu_r = bu_r.at[:, 0].add(abr * sr0 - abi * si0)
    bu_i = bu_i.at[:, 0].add(abr * si0 + abi * sr0)
    a_r = jnp.broadcast_to(abr, bu_r.shape)
    a_i = jnp.broadcast_to(abi, bu_i.shape)

    def combine(e1, e2):
        ar1, ai1, br1, bi1 = e1
        ar2, ai2, br2, bi2 = e2
        return (ar1 * ar2 - ai1 * ai2, ar1 * ai2 + ai1 * ar2,
                ar2 * br1 - ai2 * bi1 + br2, ar2 * bi1 + ai2 * br1 + bi2)

    _, _, sr, si = lax.associative_scan(combine, (a_r, a_i, bu_r, bu_i), axis=1)
    y = (jnp.einsum('nlgp,ghp->nlgh', sr, c_re.astype(F32))
         - jnp.einsum('nlgp,ghp->nlgh', si, c_im.astype(F32)))
    y = y.reshape(N, L, S5_WIDTH) + d.astype(F32) * uf
    y = jax.nn.gelu(y)
    y = y * jax.nn.sigmoid(y @ glu_w.astype(F32) + glu_b.astype(F32))
    return y.astype(u.dtype), sr[:, -1].astype(s0_re.dtype), si[:, -1].astype(s0_im.dtype)


def alibi_slopes():
    return jnp.exp2(-8.0 * jnp.arange(1, MB_HEADS + 1, dtype=F32) / MB_HEADS)


def moba_core(q, qpos, means, n_blocks, fetch):
    N, H, T, _ = q.shape
    own = qpos // MB_BLOCK
    nsel = min(MB_TOPK, n_blocks)
    gate = jnp.einsum('nhtd,nhbd->nhtb', q, means)
    fully_past = jnp.arange(n_blocks)[None, :] < own[:, None]
    gate = jnp.where(fully_past, gate, -jnp.inf)
    gval, gidx = lax.top_k(gate, nsel)
    sel = jnp.concatenate([gidx, jnp.broadcast_to(own[:, None], (N, H, T, 1))], axis=-1).astype(jnp.int32)
    ok = jnp.concatenate([gval > -jnp.inf, jnp.ones((N, H, T, 1), bool)], axis=-1)
    k, v = fetch(sel)
    kpos = sel[..., None] * MB_BLOCK + jnp.arange(MB_BLOCK, dtype=jnp.int32)
    qp = qpos[None, None, :, None, None]
    s = jnp.einsum('nhtd,nhtsjd->nhtsj', q, k.astype(F32))
    s = s - alibi_slopes()[None, :, None, None, None] * (qp - kpos).astype(F32)
    s = jnp.where(ok[..., None] & (kpos <= qp), s, -jnp.inf)
    p = jax.nn.softmax(s.reshape(N, H, T, -1), axis=-1).reshape(s.shape)
    return jnp.einsum('nhtsj,nhtsjd->nhtd', p, v.astype(F32))


def moba_prompt(q, k, v):
    N, L, H, hd = q.shape
    nb = -(-L // MB_BLOCK)
    pad = nb * MB_BLOCK - L

    def to_blocks(t):
        t = jnp.pad(t, ((0, 0), (0, pad), (0, 0), (0, 0)))
        return t.reshape(N, nb, MB_BLOCK, H, hd).transpose(0, 3, 1, 2, 4)

    kb, vb = to_blocks(k), to_blocks(v)
    means = jnp.mean(kb.astype(F32), axis=3)
    n_i = jnp.arange(N)[:, None, None, None]
    h_i = jnp.arange(H)[None, :, None, None]

    def fetch(sel):
        return kb[n_i, h_i, sel], vb[n_i, h_i, sel]

    qc = min(MB_QCHUNK, L)
    nq = L // qc
    qs = (q.astype(F32) * hd ** -0.5).transpose(0, 2, 1, 3).reshape(N, H, nq, qc, hd).transpose(2, 0, 1, 3, 4)
    pos = jnp.arange(L, dtype=jnp.int32).reshape(nq, qc)
    out = lax.map(lambda a: moba_core(a[0], a[1], means, nb, fetch), (qs, pos))
    return out.transpose(1, 0, 3, 2, 4).reshape(N, L, H * hd)


def moba_sample(q, k_new, v_new, k_pool, v_pool, layer, page_table):
    N, T, H, hd = q.shape
    n_pages = page_table.shape[1]
    past = n_pages * PAGE_SIZE
    nb = -(-(past + T) // MB_BLOCK)
    page_sums = jnp.sum(k_pool[layer, page_table].astype(F32), axis=2)
    page_blk = jnp.arange(n_pages) * PAGE_SIZE // MB_BLOCK
    new_blk = (past + jnp.arange(T)) // MB_BLOCK
    blk_sum = (jax.ops.segment_sum(page_sums.transpose(1, 0, 2, 3), page_blk, nb)
               + jax.ops.segment_sum(k_new.astype(F32).transpose(1, 0, 2, 3), new_blk, nb))
    means = blk_sum.transpose(1, 2, 0, 3) / MB_BLOCK
    n_i = jnp.arange(N)[:, None, None, None, None]
    h_i = jnp.arange(H)[None, :, None, None, None]

    def fetch(sel):
        kpos = sel[..., None] * MB_BLOCK + jnp.arange(MB_BLOCK, dtype=jnp.int32)
        in_past = (kpos < past)[..., None]
        pp = jnp.clip(kpos, 0, past - 1)
        phys = page_table[n_i, pp // PAGE_SIZE]
        off = pp % PAGE_SIZE
        ni = jnp.clip(kpos - past, 0, T - 1)
        kk = jnp.where(in_past, k_pool[layer, phys, off, h_i], k_new[n_i, ni, h_i])
        vv = jnp.where(in_past, v_pool[layer, phys, off, h_i], v_new[n_i, ni, h_i])
        return kk, vv

    qf = (q.astype(F32) * hd ** -0.5).transpose(0, 2, 1, 3)
    out = moba_core(qf, past + jnp.arange(T, dtype=jnp.int32), means, nb, fetch)
    return out.transpose(0, 2, 1, 3).reshape(N, T, H * hd)


def moe(x, router_w, router_bias, w_gate, w_up, w_down, sw_gate, sw_up, sw_down):
    N, L, D = x.shape
    n = N * L
    t = x.reshape(n, D)
    scores = jax.nn.sigmoid(t.astype(F32) @ router_w.astype(F32))
    _, idx = lax.top_k(scores + router_bias.astype(F32), TOP_K)
    picked = jnp.take_along_axis(scores, idx, axis=-1)
    wts = picked / jnp.sum(picked, -1, keepdims=True) * ROUTED_SCALE
    gate = jnp.zeros_like(scores).at[jnp.arange(n)[:, None], idx].set(wts)
    tb = min(MOE_TBLOCK, n)
    nb = -(-n // tb)
    pad = nb * tb - n
    tp = jnp.pad(t, ((0, pad), (0, 0))).reshape(nb, tb, D)
    gp = jnp.pad(gate, ((0, pad), (0, 0))).reshape(nb, tb, N_EXPERTS).astype(x.dtype)

    def block(a):
        xb, gb = a
        h = jax.nn.silu(jnp.einsum('td,edf->tef', xb, w_gate)) * jnp.einsum('td,edf->tef', xb, w_up)
        return jnp.einsum('tef,efd->td', h * gb[..., None], w_down)

    routed = lax.map(block, (tp, gp)).reshape(nb * tb, D)[:n]
    shared = (jax.nn.silu(t @ sw_gate) * (t @ sw_up)) @ sw_down
    return (routed + shared).reshape(N, L, D)


def trunk_layer(x, dn_buf, dn_S, s5_re, s5_im, attend,
                w_in, dn_conv_w, dn_a_log, dn_dt_bias, dn_norm_w,
                s5_a_re, s5_a_im, s5_log_dt, s5_b_re, s5_b_im, s5_c_re, s5_c_im, s5_d, s5_glu_w, s5_glu_b,
                w_branch_a, w_branch_b, w_branch_c, w_out, ln1_g, ln1_b,
                router_w, router_bias, exp_w_gate, exp_w_up, exp_w_down, sh_w_gate, sh_w_up, sh_w_down,
                ln2_g, ln2_b):
    N, L, _ = x.shape
    proj = x @ w_in
    dq, dk, dv, dz, db, da, su, mq, mk, mv, ga, gb, gc = jnp.split(proj, IN_OFFSETS, axis=-1)
    a_out, dn_buf_new, dn_S_new = deltanet_branch(dq, dk, dv, dz, db, da, dn_buf, dn_S,
                                                  dn_conv_w, dn_a_log, dn_dt_bias, dn_norm_w)
    b_out, s5_re_new, s5_im_new = s5_branch(su, s5_re, s5_im, s5_a_re, s5_a_im, s5_log_dt,
                                            s5_b_re, s5_b_im, s5_c_re, s5_c_im, s5_d, s5_glu_w, s5_glu_b)
    mq = mq.reshape(N, L, MB_HEADS, MB_HD)
    mk = mk.reshape(N, L, MB_HEADS, MB_HD)
    mv = mv.reshape(N, L, MB_HEADS, MB_HD)
    c_out = attend(mq, mk, mv).astype(x.dtype)
    merged = (jax.nn.sigmoid(ga) * (a_out @ w_branch_a)
              + jax.nn.sigmoid(gb) * (b_out @ w_branch_b)
              + jax.nn.sigmoid(gc) * (c_out @ w_branch_c))
    h = layer_norm(DEEPNORM_ALPHA * x + merged @ w_out, ln1_g, ln1_b)
    ffn = moe(h, router_w, router_bias, exp_w_gate, exp_w_up, exp_w_down, sh_w_gate, sh_w_up, sh_w_down)
    y = layer_norm(DEEPNORM_ALPHA * h + ffn, ln2_g, ln2_b)
    return y, (mk, mv, dn_buf_new, dn_S_new, s5_re_new, s5_im_new)


def setup_inputs(seed: int = 0) -> dict:
    key = jax.random.key(seed)
    ks = list(jax.random.split(key, 48))

    def nrm(shape, scale):
        return jax.random.normal(ks.pop(), shape, F32) * scale

    n_pages = PAST_LEN // PAGE_SIZE
    n_used = DEC_BATCH * n_pages
    n_pool = n_used + n_used // 4
    page_table = jax.random.permutation(ks.pop(), n_pool)[:n_used].astype(jnp.int32).reshape(DEC_BATCH, n_pages)
    dt_dn = jnp.exp(jax.random.uniform(ks.pop(), (DEPTH, DN_HEADS), F32, math.log(1e-3), math.log(1e-1)))
    inp = {
        'x_prompt': nrm((BATCH, SEQ, D_MODEL), 1.0),
        'x_sample': nrm((DEC_BATCH, DEC_SEQ, D_MODEL), 1.0),
        'cache_k': nrm((DEPTH, n_pool, PAGE_SIZE, MB_HEADS, MB_HD), 1.0),
        'cache_v': nrm((DEPTH, n_pool, PAGE_SIZE, MB_HEADS, MB_HD), 1.0),
        'page_table': page_table,
        'state_dn_conv': nrm((DEPTH, DEC_BATCH, DN_CONV - 1, 3 * DN_WIDTH), 1.0),
        'state_dn': nrm((DEPTH, DEC_BATCH, DN_HEADS, DN_DK, DN_DV), 0.1),
        'state_s5_re': nrm((DEPTH, DEC_BATCH, S5_GROUPS, S5_STATE), 0.1),
        'state_s5_im': nrm((DEPTH, DEC_BATCH, S5_GROUPS, S5_STATE), 0.1),
        'w_in': nrm((DEPTH, D_MODEL, IN_DIM), D_MODEL ** -0.5),
        'dn_conv_w': nrm((DEPTH, DN_CONV, 3 * DN_WIDTH), DN_CONV ** -0.5),
        'dn_a_log': jnp.log(jax.random.uniform(ks.pop(), (DEPTH, DN_HEADS), F32, 1.0, 16.0)),
        'dn_dt_bias': jnp.log(jnp.expm1(dt_dn)),
        'dn_norm_w': 1.0 + nrm((DEPTH, DN_DV), 0.01),
        's5_a_re': -0.5 + nrm((DEPTH, S5_GROUPS, S5_STATE), 0.01),
        's5_a_im': jnp.pi * jnp.arange(S5_STATE, dtype=F32) + nrm((DEPTH, S5_GROUPS, S5_STATE), 0.01),
        's5_log_dt': jax.random.uniform(ks.pop(), (DEPTH, S5_GROUPS), F32, math.log(1e-3), math.log(1e-1)),
        's5_b_re': nrm((DEPTH, S5_GROUPS, S5_STATE, S5_GROUP), (2 * S5_GROUP) ** -0.5),
        's5_b_im': nrm((DEPTH, S5_GROUPS, S5_STATE, S5_GROUP), (2 * S5_GROUP) ** -0.5),
        's5_c_re': nrm((DEPTH, S5_GROUPS, S5_GROUP, S5_STATE), (2 * S5_STATE) ** -0.5),
        's5_c_im': nrm((DEPTH, S5_GROUPS, S5_GROUP, S5_STATE), (2 * S5_STATE) ** -0.5),
        's5_d': nrm((DEPTH, S5_WIDTH), 1.0),
        's5_glu_w': nrm((DEPTH, S5_WIDTH, S5_WIDTH), S5_WIDTH ** -0.5),
        's5_glu_b': nrm((DEPTH, S5_WIDTH), 0.01),
        'w_branch_a': nrm((DEPTH, DN_WIDTH, D_MODEL), DN_WIDTH ** -0.5),
        'w_branch_b': nrm((DEPTH, S5_WIDTH, D_MODEL), S5_WIDTH ** -0.5),
        'w_branch_c': nrm((DEPTH, MB_WIDTH, D_MODEL), MB_WIDTH ** -0.5),
        'w_out': nrm((DEPTH, D_MODEL, D_MODEL), D_MODEL ** -0.5 * DEEPNORM_BETA),
        'ln1_g': 1.0 + nrm((DEPTH, D_MODEL), 0.01),
        'ln1_b': nrm((DEPTH, D_MODEL), 0.01),
        'router_w': nrm((DEPTH, D_MODEL, N_EXPERTS), D_MODEL ** -0.5),
        'router_bias': nrm((DEPTH, N_EXPERTS), 0.01),
        'exp_w_gate': nrm((DEPTH, N_EXPERTS, D_MODEL, EXPERT_FF), D_MODEL ** -0.5),
        'exp_w_up': nrm((DEPTH, N_EXPERTS, D_MODEL, EXPERT_FF), D_MODEL ** -0.5),
        'exp_w_down': nrm((DEPTH, N_EXPERTS, EXPERT_FF, D_MODEL), EXPERT_FF ** -0.5 * DEEPNORM_BETA),
        'sh_w_gate': nrm((DEPTH, D_MODEL, SHARED_FF), D_MODEL ** -0.5),
        'sh_w_up': nrm((DEPTH, D_MODEL, SHARED_FF), D_MODEL ** -0.5),
        'sh_w_down': nrm((DEPTH, SHARED_FF, D_MODEL), SHARED_FF ** -0.5 * DEEPNORM_BETA),
        'ln2_g': 1.0 + nrm((DEPTH, D_MODEL), 0.01),
        'ln2_b': nrm((DEPTH, D_MODEL), 0.01),
    }
    return inp


def reference(x_prompt, x_sample, cache_k, cache_v, page_table, state_dn_conv, state_dn, state_s5_re, state_s5_im,
              w_in, dn_conv_w, dn_a_log, dn_dt_bias, dn_norm_w,
              s5_a_re, s5_a_im, s5_log_dt, s5_b_re, s5_b_im, s5_c_re, s5_c_im, s5_d, s5_glu_w, s5_glu_b,
              w_branch_a, w_branch_b, w_branch_c, w_out, ln1_g, ln1_b,
              router_w, router_bias, exp_w_gate, exp_w_up, exp_w_down, sh_w_gate, sh_w_up, sh_w_down,
              ln2_g, ln2_b):
    n_p = x_prompt.shape[0]
    zdt = x_prompt.dtype
    hp, hs = x_prompt, x_sample
    p_st, s_st = [], []
    for l in range(DEPTH):
        lw = (w_in[l], dn_conv_w[l], dn_a_log[l], dn_dt_bias[l], dn_norm_w[l],
              s5_a_re[l], s5_a_im[l], s5_log_dt[l], s5_b_re[l], s5_b_im[l], s5_c_re[l], s5_c_im[l],
              s5_d[l], s5_glu_w[l], s5_glu_b[l], w_branch_a[l], w_branch_b[l], w_branch_c[l], w_out[l],
              ln1_g[l], ln1_b[l], router_w[l], router_bias[l], exp_w_gate[l], exp_w_up[l], exp_w_down[l],
              sh_w_gate[l], sh_w_up[l], sh_w_down[l], ln2_g[l], ln2_b[l])
        hp, st = trunk_layer(hp,
                             jnp.zeros((n_p, DN_CONV - 1, 3 * DN_WIDTH), zdt),
                             jnp.zeros((n_p, DN_HEADS, DN_DK, DN_DV), zdt),
                             jnp.zeros((n_p, S5_GROUPS, S5_STATE), zdt),
                             jnp.zeros((n_p, S5_GROUPS, S5_STATE), zdt),
                             moba_prompt, *lw)
        p_st.append(st)
        sample_attend = functools.partial(moba_sample, k_pool=cache_k, v_pool=cache_v, layer=l,
                                          page_table=page_table)
        hs, st = trunk_layer(hs, state_dn_conv[l], state_dn[l], state_s5_re[l], state_s5_im[l],
                             sample_attend, *lw)
        s_st.append(st)
    p_k, p_v, p_dn_conv, p_dn, p_s5_re, p_s5_im = [jnp.stack(t) for t in zip(*p_st)]
    s_k, s_v, s_dn_conv, s_dn, s_s5_re, s_s5_im = [jnp.stack(t) for t in zip(*s_st)]
    return (hp, hs, p_k, p_v, p_dn_conv, p_dn, p_s5_re, p_s5_im,
            s_k, s_v, s_dn_conv, s_dn, s_s5_re, s_s5_im)
```

```python
import functools
import math

import jax
import jax.numpy as jnp
import numpy as np
from jax import lax
from jax.experimental import pallas as pl
from jax.experimental.pallas import tpu as pltpu

F32 = jnp.float32
BF16 = jnp.bfloat16

DN_HEADS = 4
DN_DK = 128
DN_DV = 128
DN_WIDTH = DN_HEADS * DN_DK
DN_CONV = 4
DN_CHUNK = 64
S5_WIDTH = 512
S5_GROUP = 16
S5_GROUPS = S5_WIDTH // S5_GROUP
S5_STATE = 64
MB_HEADS = 4
MB_HD = 128
MB_WIDTH = MB_HEADS * MB_HD
MB_BLOCK = 256
MB_TOPK = 3
PAGE_SIZE = 128
N_EXPERTS = 64
TOP_K = 8
EXPERT_FF = 256
ROUTED_SCALE = 2.5
LN_EPS = 1e-5
NORM_EPS = 1e-6

LANES = 128
S5_T = 16
DN_TB = 256
VMEM_LIMIT = 56 * 1024 * 1024
NEG = -1e30


def _params(*sem):
    return pltpu.CompilerParams(dimension_semantics=sem, vmem_limit_bytes=VMEM_LIMIT)


def _const_spec(shape):
    nd = len(shape)
    return pl.BlockSpec(shape, lambda *_: (0,) * nd)


def _dot(a, b):
    return jnp.dot(a, b, preferred_element_type=F32)


def _dot_nt(a, b, precision=None):
    return lax.dot_general(a, b, (((1,), (1,)), ((), ())), preferred_element_type=F32, precision=precision)


def _dot_tn(a, b):
    return lax.dot_general(a, b, (((0,), (0,)), ((), ())), preferred_element_type=F32)


def _split(a):
    hi = a.astype(BF16)
    return hi, (a - hi.astype(F32)).astype(BF16)


def _mm3(a, b):
    a_hi, a_lo = _split(a)
    b_hi, b_lo = _split(b)
    return _dot(a_hi, b_hi) + (_dot(a_hi, b_lo) + _dot(a_lo, b_hi))


def _silu(x):
    return x * jax.nn.sigmoid(x)


def _layer_norm(r, g, b):
    mu = jnp.mean(r, axis=-1, keepdims=True)
    c = r - mu
    var = jnp.mean(c * c, axis=-1, keepdims=True)
    return c * lax.rsqrt(var + LN_EPS) * g + b


def _proj_body(x_ref, wdn, wba, ws5, wmq, wkv, wg, odn, oba, os5, omq, okv, og, *, q_scale):
    xb = x_ref[...].astype(BF16)

    def mm(w_ref, o_ref, scale=None):
        n = w_ref.shape[1]
        step = min(n, 512)
        for c in range(0, n, step):
            r = _dot(xb, w_ref[:, c:c + step])
            if scale is not None:
                r = r * scale
            o_ref[:, c:c + step] = r.astype(o_ref.dtype)

    mm(wdn, odn)
    mm(wba, oba)
    mm(ws5, os5)
    mm(wmq, omq, q_scale)
    mm(wkv, okv)
    mm(wg, og)


def _proj(x, w, tm):
    m, d = x.shape
    outs = (("dn", BF16), ("ba", F32), ("s5", F32), ("mq", BF16), ("kv", F32), ("g", BF16))
    ws = [w["w_" + k] for k, _ in outs]
    return pl.pallas_call(
        functools.partial(_proj_body, q_scale=MB_HD ** -0.5),
        grid=(m // tm,),
        in_specs=[pl.BlockSpec((tm, d), lambda i: (i, 0))] + [_const_spec(wi.shape) for wi in ws],
        out_specs=[pl.BlockSpec((tm, wi.shape[1]), lambda i: (i, 0)) for wi in ws],
        out_shape=[jax.ShapeDtypeStruct((m, wi.shape[1]), dt) for wi, (_, dt) in zip(ws, outs)],
        compiler_params=_params("parallel"),
        name="proj",
    )(x, *ws)


def _dn_body(pdn_ref, pba_ref, cw_ref, ab_ref, nw_ref, o_ref, s_out_ref, buf, s_scr):
    tb = DN_TB
    t = pl.program_id(1)

    @pl.when(t == 0)
    def _():
        buf[0:8, :] = jnp.zeros((8, buf.shape[1]), F32)
        s_scr[...] = jnp.zeros(s_scr.shape, F32)

    buf[8:8 + tb, :] = pdn_ref[0, :, 0:3 * DN_WIDTH].astype(F32)

    def conv(c0):
        cs = slice(c0, c0 + DN_DK)
        y = cw_ref[3:4, cs] * buf[8:8 + tb, cs]
        for j in range(1, DN_CONV):
            y = y + cw_ref[3 - j:4 - j, cs] * buf[8 - j:8 - j + tb, cs]
        return _silu(y)

    def l2norm(x):
        return x * lax.rsqrt(jnp.sum(x * x, axis=-1, keepdims=True) + NORM_EPS)

    pba = pba_ref[0]
    beta_all = jax.nn.sigmoid(pba)
    g_all = -jnp.exp(ab_ref[0:1, :]) * jax.nn.softplus(pba + ab_ref[1:2, :])

    row = lax.broadcasted_iota(jnp.int32, (tb, tb), 0)
    col = lax.broadcasted_iota(jnp.int32, (tb, tb), 1)
    same64 = (row >> 6) == (col >> 6)
    same32 = (row >> 5) == (col >> 5)
    same16 = (row >> 4) == (col >> 4)
    incl = jnp.logical_and(same64, row >= col)
    strict = jnp.logical_and(same64, row > col)
    hp = lax.Precision.HIGHEST
    gc = jnp.dot(jnp.where(incl, 1.0, 0.0), g_all, precision=hp, preferred_element_type=F32)
    gl = jnp.dot(jnp.where(same64, 1.0, 0.0), g_all, precision=hp, preferred_element_type=F32)
    gct = gc.T
    eye = jnp.where(row == col, 1.0, 0.0)

    for h in range(DN_HEADS):
        q = l2norm(conv(h * DN_DK)) * DN_DK ** -0.5
        k = l2norm(conv(DN_WIDTH + h * DN_DK))
        v = conv(2 * DN_WIDTH + h * DN_DK)
        beta = beta_all[:, h:h + 1]
        gcol = gc[:, DN_HEADS + h:DN_HEADS + h + 1]
        grow = gct[DN_HEADS + h:DN_HEADS + h + 1, :]
        glast = gl[:, DN_HEADS + h:DN_HEADS + h + 1]
        dec = jnp.exp(jnp.where(incl, gcol - grow, NEG))
        kb = k * beta
        k16 = k.astype(BF16)
        a = jnp.where(strict, _dot_nt(kb.astype(BF16), k16) * dec, 0.0)
        qk = _dot_nt(q.astype(BF16), k16) * dec

        n1 = jnp.where(same16, -a, 0.0)
        n2 = _mm3(n1, n1)
        n4 = _mm3(n2, n2)
        n8 = _mm3(n4, n4)
        p = eye + n1
        p = p + _mm3(p, n2)
        p = p + _mm3(p, n4)
        p = p + _mm3(p, n8)
        a1 = jnp.where(jnp.logical_and(same32, jnp.logical_not(same16)), a, 0.0)
        p = p - _mm3(p, _mm3(a1, p))
        a2 = jnp.where(same32, 0.0, a)
        p = p - _mm3(p, _mm3(a2, p))

        eg = jnp.exp(gcol)
        sol = _mm3(p, jnp.concatenate([v * beta, kb * eg], axis=1))
        u = sol[:, :DN_DV]
        w = sol[:, DN_DV:]
        qd = q * eg
        kd = k * jnp.exp(glast - gcol)

        sh = s_scr[h]
        for c in range(tb // DN_CHUNK):
            r = slice(c * DN_CHUNK, (c + 1) * DN_CHUNK)
            wq = jnp.concatenate([w[r], qd[r]], axis=0).astype(BF16)
            ws_ = _dot(wq, sh.astype(BF16))
            vn = u[r] - ws_[:DN_CHUNK]
            vn16 = vn.astype(BF16)
            o = ws_[DN_CHUNK:] + _dot(qk[r, r].astype(BF16), vn16)
            sh = sh * jnp.exp(glast[c * DN_CHUNK:c * DN_CHUNK + 1, :]) + _dot_tn(kd[r].astype(BF16), vn16)
            z = pdn_ref[0, r, 3 * DN_WIDTH + h * DN_DV:3 * DN_WIDTH + (h + 1) * DN_DV].astype(F32)
            o = o * lax.rsqrt(jnp.mean(o * o, axis=-1, keepdims=True) + NORM_EPS) * nw_ref[...] * _silu(z)
            o_ref[0, r, h * DN_DV:(h + 1) * DN_DV] = o.astype(o_ref.dtype)
        s_scr[h] = sh

    buf[0:8, :] = buf[tb:tb + 8, :]

    @pl.when(t == pl.num_programs(1) - 1)
    def _():
        s_out_ref[0] = s_scr[...]


def _dn_gate_rows(a_log, dt_bias):
    ab = jnp.zeros((2, LANES), F32)
    ab = ab.at[0, DN_HEADS:2 * DN_HEADS].set(a_log)
    return ab.at[1, DN_HEADS:2 * DN_HEADS].set(dt_bias)


def _deltanet_prompt(pdn, pba, conv_w, ab, norm_w):
    n, l, _ = pdn.shape
    assert l % DN_TB == 0
    return pl.pallas_call(
        _dn_body,
        grid=(n, l // DN_TB),
        in_specs=[
            pl.BlockSpec((1, DN_TB, 4 * DN_WIDTH), lambda i, t: (i, t, 0)),
            pl.BlockSpec((1, DN_TB, LANES), lambda i, t: (i, t, 0)),
            _const_spec(conv_w.shape), _const_spec(ab.shape), _const_spec(norm_w.shape),
        ],
        out_specs=[
            pl.BlockSpec((1, DN_TB, DN_WIDTH), lambda i, t: (i, t, 0)),
            pl.BlockSpec((1, DN_HEADS, DN_DK, DN_DV), lambda i, t: (i, 0, 0, 0)),
        ],
        out_shape=[
            jax.ShapeDtypeStruct((n, l, DN_WIDTH), BF16),
            jax.ShapeDtypeStruct((n, DN_HEADS, DN_DK, DN_DV), F32),
        ],
        scratch_shapes=[pltpu.VMEM((DN_TB + 8, 3 * DN_WIDTH), F32), pltpu.VMEM((DN_HEADS, DN_DK, DN_DV), F32)],
        compiler_params=_params("parallel", "arbitrary"),
        name="deltanet_prompt",
    )(pdn, pba, conv_w, ab, norm_w)


def _dn_step_body(pdn_ref, pba_ref, cbuf_ref, s0_ref, cw_ref, ab_ref, nw_ref, o_ref, s_out_ref):
    x = pdn_ref[0, :, 0:3 * DN_WIDTH].astype(F32)
    y = cw_ref[3:4, :] * x
    for j in range(1, DN_CONV):
        y = y + cw_ref[3 - j:4 - j, :] * cbuf_ref[0, 3 - j:4 - j, :]
    qkv = _silu(y)
    pba = pba_ref[0]
    beta_all = jax.nn.sigmoid(pba)
    g_all = -jnp.exp(ab_ref[0:1, :]) * jax.nn.softplus(pba + ab_ref[1:2, :])

    def l2norm(t):
        return t * lax.rsqrt(jnp.sum(t * t, axis=-1, keepdims=True) + NORM_EPS)

    sub = lax.broadcasted_iota(jnp.int32, (8, DN_DK), 0)
    for h in range(DN_HEADS):
        q = l2norm(qkv[:, h * DN_DK:(h + 1) * DN_DK]) * DN_DK ** -0.5
        k = l2norm(qkv[:, DN_WIDTH + h * DN_DK:DN_WIDTH + (h + 1) * DN_DK])
        v = qkv[:, 2 * DN_WIDTH + h * DN_DV:2 * DN_WIDTH + (h + 1) * DN_DV]
        beta = beta_all[:, h:h + 1]
        eg = jnp.exp(g_all[:, DN_HEADS + h:DN_HEADS + h + 1])
        s0 = s0_ref[0, h]
        w = k * beta * eg
        lhs = jnp.where(sub == 0, w, jnp.where(sub == 1, q * eg, 0.0))
        ws_ = _dot(lhs.astype(BF16), s0.astype(BF16))
        vn = v * beta - ws_[0:1, :]
        o = ws_[1:2, :] + jnp.sum(q * k, axis=-1, keepdims=True) * vn
        kt = jnp.broadcast_to(k, (DN_DK, DN_DK)).T
        s_out_ref[0, h] = s0 * eg + kt * vn
        z = pdn_ref[0, :, 3 * DN_WIDTH + h * DN_DV:3 * DN_WIDTH + (h + 1) * DN_DV].astype(F32)
        o = o * lax.rsqrt(jnp.mean(o * o, axis=-1, keepdims=True) + NORM_EPS) * nw_ref[...] * _silu(z)
        o_ref[0, :, h * DN_DV:(h + 1) * DN_DV] = o.astype(o_ref.dtype)


def _deltanet_step(pdn, pba, conv_buf, s0, conv_w, ab, norm_w):
    n = pdn.shape[0]
    return pl.pallas_call(
        _dn_step_body,
        grid=(n,),
        in_specs=[
            pl.BlockSpec((1, 1, 4 * DN_WIDTH), lambda i: (i, 0, 0)),
            pl.BlockSpec((1, 1, LANES), lambda i: (i, 0, 0)),
            pl.BlockSpec((1, DN_CONV - 1, 3 * DN_WIDTH), lambda i: (i, 0, 0)),
            pl.BlockSpec((1, DN_HEADS, DN_DK, DN_DV), lambda i: (i, 0, 0, 0)),
            _const_spec(conv_w.shape), _const_spec(ab.shape), _const_spec(norm_w.shape),
        ],
        out_specs=[
            pl.BlockSpec((1, 1, DN_WIDTH), lambda i: (i, 0, 0)),
            pl.BlockSpec((1, DN_HEADS, DN_DK, DN_DV), lambda i: (i, 0, 0, 0)),
        ],
        out_shape=[
            jax.ShapeDtypeStruct((n, 1, DN_WIDTH), BF16),
            jax.ShapeDtypeStruct((n, DN_HEADS, DN_DK, DN_DV), F32),
        ],
        compiler_params=_params("parallel"),
        name="deltanet_step",
    )(pdn, pba, conv_buf, s0, conv_w, ab, norm_w)


def _s5_discretize(a_re, a_im, log_dt, b_re, b_im):
    dt = jnp.exp(log_dt)[:, None]
    mag = jnp.exp(a_re * dt)
    abr, abi = mag * jnp.cos(a_im * dt), mag * jnp.sin(a_im * dt)
    den = a_re * a_re + a_im * a_im
    fr = ((abr - 1.0) * a_re + abi * a_im) / den
    fi = (abi * a_re - (abr - 1.0) * a_im) / den
    bbr = fr[..., None] * b_re - fi[..., None] * b_im
    bbi = fr[..., None] * b_im + fi[..., None] * b_re
    return abr, abi, bbr, bbi


def _s5_powers(a_re, a_im, log_dt, taus):
    dt = jnp.exp(log_dt)[None, :, None]
    tau = jnp.asarray(taus, F32)[:, None, None]
    mag = jnp.exp(a_re[None] * dt * tau)
    ang = a_im[None] * dt * tau
    return mag * jnp.cos(ang), mag * jnp.sin(ang)


def _s5_prompt_weights(a_re, a_im, log_dt, b_re, b_im, c_re, c_im, n_chunks):
    t = S5_T
    hp = lax.Precision.HIGHEST
    _, _, bbr, bbi = _s5_discretize(a_re, a_im, log_dt, b_re, b_im)
    lr, li = _s5_powers(a_re, a_im, log_dt, list(range(t + 1)))
    lbr = lr[..., None] * bbr[None] - li[..., None] * bbi[None]
    lbi = lr[..., None] * bbi[None] + li[..., None] * bbr[None]
    kt = (jnp.einsum("ghp,tgpk->tghk", c_re, lbr, precision=hp)
          - jnp.einsum("ghp,tgpk->tghk", c_im, lbi, precision=hp))
    lag = np.arange(t)[None, :] - np.arange(t)[:, None]
    toep = jnp.where((lag >= 0)[:, :, None, None, None], kt[np.clip(lag, 0, t)], 0.0)
    toep = toep.transpose(2, 0, 4, 1, 3).reshape(S5_GROUPS, t * S5_GROUP, t * S5_GROUP)
    wz_r = lbr[t - 1 - np.arange(t)].transpose(1, 0, 3, 2).reshape(S5_GROUPS, t * S5_GROUP, S5_STATE)
    wz_i = lbi[t - 1 - np.arange(t)].transpose(1, 0, 3, 2).reshape(S5_GROUPS, t * S5_GROUP, S5_STATE)
    wz = jnp.concatenate([wz_r, wz_i], axis=-1)
    lr1, li1 = lr[1:], li[1:]
    wy_r = c_re[None] * lr1[:, :, None, :] - c_im[None] * li1[:, :, None, :]
    wy_i = -(c_re[None] * li1[:, :, None, :] + c_im[None] * lr1[:, :, None, :])
    wy = jnp.concatenate([wy_r, wy_i], axis=-1).transpose(1, 3, 0, 2).reshape(
        S5_GROUPS, 2 * S5_STATE, t * S5_GROUP)
    n_steps = max(1, int(math.ceil(math.log2(n_chunks))))
    sr, si = _s5_powers(a_re, a_im, log_dt, [t * (1 << i) for i in range(n_steps)])
    sc_r = jnp.concatenate([sr, sr], axis=-1).transpose(1, 0, 2)
    sc_i = jnp.concatenate([-si, si], axis=-1).transpose(1, 0, 2)
    return toep.astype(BF16), wz.astype(BF16), wy.astype(BF16), sc_r, sc_i


def _s5_body(u_ref, toep_ref, wz_ref, wy_ref, lr_ref, li_ref, y_ref, sfin_ref):
    u = u_ref[0, 0]
    z = _dot(u, wz_ref[0])
    nc = z.shape[0]
    rowi = lax.broadcasted_iota(jnp.int32, z.shape, 0)
    s = z
    for i in range(lr_ref.shape[1]):
        d = 1 << i
        if d >= nc:
            break
        sh = jnp.where(rowi >= d, pltpu.roll(s, d, 0), 0.0)
        s = s + lr_ref[0, i:i + 1, :] * sh + li_ref[0, i:i + 1, :] * pltpu.roll(sh, S5_STATE, 1)
    sfin_ref[0, 0] = s[nc - 1:nc, :]
    sprev = jnp.where(rowi >= 1, pltpu.roll(s, 1, 0), 0.0)
    y_ref[0, 0] = _dot(u, toep_ref[0]) + _dot(sprev.astype(BF16), wy_ref[0])


def _s5_prompt(u, wts):
    toep, wz, wy, sc_r, sc_i = wts
    n, l, _ = u.shape
    nc = l // S5_T
    tw = S5_T * S5_GROUP
    ug = u.astype(BF16).reshape(n, nc, S5_T, S5_GROUPS, S5_GROUP).transpose(3, 0, 1, 2, 4).reshape(
        S5_GROUPS, n, nc, tw)
    wspec = lambda shp: pl.BlockSpec((1,) + shp[1:], lambda g, i: (g, 0, 0))
    y, sfin = pl.pallas_call(
        _s5_body,
        grid=(S5_GROUPS, n),
        in_specs=[pl.BlockSpec((1, 1, nc, tw), lambda g, i: (g, i, 0, 0)),
                  wspec(toep.shape), wspec(wz.shape), wspec(wy.shape), wspec(sc_r.shape), wspec(sc_i.shape)],
        out_specs=[pl.BlockSpec((1, 1, nc, tw), lambda g, i: (g, i, 0, 0)),
                   pl.BlockSpec((1, 1, 1, 2 * S5_STATE), lambda g, i: (g, i, 0, 0))],
        out_shape=[jax.ShapeDtypeStruct((S5_GROUPS, n, nc, tw), F32),
                   jax.ShapeDtypeStruct((S5_GROUPS, n, 1, 2 * S5_STATE), F32)],
        compiler_params=_params("parallel", "parallel"),
        name="s5_prompt",
    )(ug, toep, wz, wy, sc_r, sc_i)
    y = y.reshape(S5_GROUPS, n, nc, S5_T, S5_GROUP).transpose(1, 2, 3, 0, 4).reshape(n, l, S5_WIDTH)
    sfin = sfin[:, :, 0, :].transpose(1, 0, 2)
    return y, sfin[..., :S5_STATE], sfin[..., S5_STATE:]


def _s5_step_weights(a_re, a_im, log_dt, b_re, b_im, c_re, c_im):
    abr, abi, bbr, bbi = _s5_discretize(a_re, a_im, log_dt, b_re, b_im)
    eye = jnp.eye(S5_GROUPS, dtype=F32)
    gp = S5_GROUPS * S5_STATE
    wb = lambda b: jnp.einsum("gph,gk->ghkp", b, eye).reshape(S5_WIDTH, gp).astype(BF16)
    wc = lambda c: jnp.einsum("ghp,gk->gpkh", c, eye).reshape(gp, S5_WIDTH).astype(BF16)
    return (abr.reshape(1, gp), abi.reshape(1, gp), wb(bbr), wb(bbi), wc(c_re), wc(c_im))


def _s5_step_body(u_ref, s0r_ref, s0i_ref, abr_ref, abi_ref, wbr, wbi, wcr, wci, y_ref, sr_ref, si_ref):
    u = u_ref[...].astype(BF16)
    abr, abi = abr_ref[...], abi_ref[...]
    s0r, s0i = s0r_ref[...], s0i_ref[...]
    sr = abr * s0r - abi * s0i + _dot(u, wbr[...])
    si = abr * s0i + abi * s0r + _dot(u, wbi[...])
    sr_ref[...] = sr
    si_ref[...] = si
    y_ref[...] = _dot(sr.astype(BF16), wcr[...]) - _dot(si.astype(BF16), wci[...])


def _s5_step(u, s0_re, s0_im, wts):
    n = u.shape[0]
    gp = S5_GROUPS * S5_STATE
    args = (u, s0_re.reshape(n, gp), s0_im.reshape(n, gp)) + tuple(wts)
    y, sr, si = pl.pallas_call(
        _s5_step_body,
        grid=(1,),
        in_specs=[_const_spec(a.shape) for a in args],
        out_specs=[_const_spec((n, S5_WIDTH)), _const_spec((n, gp)), _const_spec((n, gp))],
        out_shape=[jax.ShapeDtypeStruct((n, S5_WIDTH), F32), jax.ShapeDtypeStruct((n, gp), F32),
                   jax.ShapeDtypeStruct((n, gp), F32)],
        compiler_params=_params("arbitrary"),
        name="s5_step",
    )(*args)
    return y, sr.reshape(n, S5_GROUPS, S5_STATE), si.reshape(n, S5_GROUPS, S5_STATE)


def _top_mask(gate, lane, count):
    sel = jnp.zeros(gate.shape, F32)
    for _ in range(count):
        m = jnp.max(gate, axis=-1, keepdims=True)
        idx = jnp.min(jnp.where(gate == m, lane, 2 ** 30), axis=-1, keepdims=True)
        hit = jnp.logical_and(lane == idx, m > -jnp.inf)
        sel = jnp.where(hit, 1.0, sel)
        gate = jnp.where(lane == idx, -jnp.inf, gate)
    return sel


def _alibi_slopes():
    s = np.exp2(-8.0 * np.arange(1, MB_HEADS + 1, dtype=np.float32) / MB_HEADS).astype(np.float32)
    return jnp.asarray(np.broadcast_to(s[:, None, None], (MB_HEADS, 1, LANES)))


def _moba_body(q_ref, k_ref, v_ref, slope_ref, o_ref, means_s, m_s, l_s, acc_s, *, nb):
    blk = MB_BLOCK
    qi = pl.program_id(2)

    @pl.when(qi == 0)
    def _():
        means_s[...] = jnp.zeros(means_s.shape, F32)
        for b in range(nb):
            means_s[b:b + 1, :] = jnp.mean(k_ref[0, b * blk:(b + 1) * blk, :], axis=0, keepdims=True)

    qb = q_ref[0]
    lane = lax.broadcasted_iota(jnp.int32, (blk, LANES), 1)
    gate = _dot_nt(qb.astype(F32), means_s[...], precision=lax.Precision.HIGHEST)
    sel = _top_mask(jnp.where(lane < qi, gate, -jnp.inf), lane, MB_TOPK)

    slope = slope_ref[0, :, 0:1]
    r = lax.broadcasted_iota(jnp.int32, (blk, blk), 0)
    c = lax.broadcasted_iota(jnp.int32, (blk, blk), 1)
    rel = -slope * (r - c).astype(F32)

    own = pl.multiple_of(qi * blk, blk)
    s = _dot_nt(qb, k_ref[0, pl.ds(own, blk), :].astype(BF16))
    s = jnp.where(c <= r, s + rel, -jnp.inf)
    m = jnp.max(s, axis=-1, keepdims=True)
    p = jnp.exp(s - m)
    m_s[...] = m
    l_s[...] = jnp.sum(p, axis=-1, keepdims=True)
    acc_s[...] = _dot(p.astype(BF16), v_ref[0, pl.ds(own, blk), :].astype(BF16))

    def body(j, carry):
        start = pl.multiple_of(j * blk, blk)
        selj = jnp.sum(jnp.where(lane == j, sel, 0.0), axis=-1, keepdims=True) > 0.0
        far = (qi - j).astype(F32) * float(blk)
        s = _dot_nt(qb, k_ref[0, pl.ds(start, blk), :].astype(BF16)) + (rel - slope * far)
        s = jnp.where(selj, s, -jnp.inf)
        m_old = m_s[...]
        m_new = jnp.maximum(m_old, jnp.max(s, axis=-1, keepdims=True))
        alpha = jnp.exp(m_old - m_new)
        p = jnp.exp(s - m_new)
        l_s[...] = alpha * l_s[...] + jnp.sum(p, axis=-1, keepdims=True)
        acc_s[...] = alpha * acc_s[...] + _dot(p.astype(BF16), v_ref[0, pl.ds(start, blk), :].astype(BF16))
        m_s[...] = m_new
        return carry

    lax.fori_loop(0, qi, body, 0)
    o_ref[0] = (acc_s[...] / l_s[...]).astype(o_ref.dtype)


def _moba_prompt(q, kv):
    n, l, _ = q.shape
    assert l % MB_BLOCK == 0 and l // MB_BLOCK <= LANES
    nb = l // MB_BLOCK
    return pl.pallas_call(
        functools.partial(_moba_body, nb=nb),
        grid=(n, MB_HEADS, nb),
        in_specs=[
            pl.BlockSpec((1, MB_BLOCK, MB_HD), lambda i, h, t: (i, t, h)),
            pl.BlockSpec((1, l, MB_HD), lambda i, h, t: (i, 0, h)),
            pl.BlockSpec((1, l, MB_HD), lambda i, h, t: (i, 0, MB_HEADS + h)),
            pl.BlockSpec((1, 1, LANES), lambda i, h, t: (h, 0, 0)),
        ],
        out_specs=pl.BlockSpec((1, MB_BLOCK, MB_HD), lambda i, h, t: (i, t, h)),
        out_shape=jax.ShapeDtypeStruct((n, l, MB_WIDTH), BF16),
        scratch_shapes=[pltpu.VMEM((LANES, MB_HD), F32), pltpu.VMEM((MB_BLOCK, 1), F32),
                        pltpu.VMEM((MB_BLOCK, 1), F32), pltpu.VMEM((MB_BLOCK, MB_HD), F32)],
        compiler_params=_params("parallel", "parallel", "arbitrary"),
        name="moba_prompt",
    )(q, kv, kv, _alibi_slopes())


PAGES_PER_STEP = 16
PAGES_PER_BLOCK = MB_BLOCK // PAGE_SIZE


def _page_sum_body(pt_ref, *refs):
    del pt_ref
    o_ref = refs[-1]
    for b in range(PAGES_PER_STEP // PAGES_PER_BLOCK):
        acc = jnp.sum(refs[PAGES_PER_BLOCK * b][0], axis=0, keepdims=True)
        for j in range(1, PAGES_PER_BLOCK):
            acc = acc + jnp.sum(refs[PAGES_PER_BLOCK * b + j][0], axis=0, keepdims=True)
        o_ref[0, b:b + 1, :] = acc


def _block_key_sums(pool, page_table, base):
    n, n_pages = page_table.shape
    assert n_pages % PAGES_PER_STEP == 0
    width = pool.shape[-1]

    def page_spec(j):
        return pl.BlockSpec((1, PAGE_SIZE, width), lambda i, s, pt: (base + pt[i, s * PAGES_PER_STEP + j], 0, 0))

    return pl.pallas_call(
        _page_sum_body,
        grid_spec=pltpu.PrefetchScalarGridSpec(
            num_scalar_prefetch=1,
            grid=(n, n_pages // PAGES_PER_STEP),
            in_specs=[page_spec(j) for j in range(PAGES_PER_STEP)],
            out_specs=pl.BlockSpec((1, PAGES_PER_STEP // PAGES_PER_BLOCK, width), lambda i, s, pt: (i, s, 0)),
        ),
        out_shape=jax.ShapeDtypeStruct((n, n_pages // PAGES_PER_BLOCK, width), F32),
        compiler_params=_params("parallel", "arbitrary"),
        name="moba_page_sums",
    )(page_table, *([pool] * PAGES_PER_STEP))


def _moba_pick_body(q_ref, sums_ref, sel_ref):
    nbk = sums_ref.shape[1]
    out = jnp.zeros(sel_ref.shape[1:], jnp.int32)
    osub = lax.broadcasted_iota(jnp.int32, out.shape, 0)
    olane = lax.broadcasted_iota(jnp.int32, out.shape, 1)
    bidx = lax.broadcasted_iota(jnp.int32, (nbk, 1), 0)
    for h in range(MB_HEADS):
        means = sums_ref[0, :, h * MB_HD:(h + 1) * MB_HD] / float(MB_BLOCK)
        q = q_ref[0, :, h * MB_HD:(h + 1) * MB_HD].astype(F32)
        gate = jnp.sum(means * q, axis=-1, keepdims=True)
        for rnk in range(MB_TOPK):
            m = jnp.max(gate, axis=0, keepdims=True)
            idx = jnp.min(jnp.where(gate == m, bidx, 2 ** 30), axis=0, keepdims=True)
            out = jnp.where(jnp.logical_and(osub == h, olane == rnk), idx, out)
            gate = jnp.where(bidx == idx, -jnp.inf, gate)
    sel_ref[0] = out


def _moba_pick(q, sums):
    n = q.shape[0]
    return pl.pallas_call(
        _moba_pick_body,
        grid=(n,),
        in_specs=[pl.BlockSpec((1, 1, MB_WIDTH), lambda i: (i, 0, 0)),
                  pl.BlockSpec((1,) + sums.shape[1:], lambda i: (i, 0, 0))],
        out_specs=pl.BlockSpec((1, 8, LANES), lambda i: (i, 0, 0)),
        out_shape=jax.ShapeDtypeStruct((n, 8, LANES), jnp.int32),
        compiler_params=_params("parallel"),
        name="moba_pick",
    )(q, sums)


def _moba_step_body(sel_ref, pt_ref, q_ref, kn_ref, vn_ref, slope_ref, k0, k1, v0, v1, o_ref, m_s, l_s, acc_s,
                    *, past):
    del pt_ref
    i, h, s = pl.program_id(0), pl.program_id(1), pl.program_id(2)
    q = q_ref[0, 0].astype(F32)
    slope = slope_ref[0, :, 0:1]

    @pl.when(s == 0)
    def _():
        m_s[...] = jnp.sum(q * kn_ref[0, 0], axis=-1, keepdims=True)
        l_s[...] = jnp.ones(l_s.shape, F32)
        acc_s[...] = vn_ref[0, 0]

    blk = sel_ref[(i * MB_HEADS + h) * MB_TOPK + s]
    sub = lax.broadcasted_iota(jnp.int32, (PAGE_SIZE, 1), 0)
    m_old = m_s[...]
    sc = []
    for half, kr in enumerate((k0, k1)):
        kpos = blk * MB_BLOCK + half * PAGE_SIZE + sub
        sc.append(jnp.sum(kr[0] * q, axis=-1, keepdims=True) - slope * (past - kpos).astype(F32))
    m_new = jnp.maximum(m_old, jnp.maximum(jnp.max(sc[0], axis=0, keepdims=True),
                                           jnp.max(sc[1], axis=0, keepdims=True)))
    alpha = jnp.exp(m_old - m_new)
    p0 = jnp.exp(sc[0] - m_new)
    p1 = jnp.exp(sc[1] - m_new)
    l_s[...] = alpha * l_s[...] + jnp.sum(p0, axis=0, keepdims=True) + jnp.sum(p1, axis=0, keepdims=True)
    acc_s[...] = (alpha * acc_s[...] + jnp.sum(p0 * v0[0], axis=0, keepdims=True)
                  + jnp.sum(p1 * v1[0], axis=0, keepdims=True))
    m_s[...] = m_new

    @pl.when(s == MB_TOPK - 1)
    def _():
        o_ref[0, 0] = (acc_s[...] / l_s[...]).astype(o_ref.dtype)


def _moba_step(q, k_new, v_new, sel, page_table, k_pool, v_pool, base):
    n = q.shape[0]
    past = page_table.shape[1] * PAGE_SIZE
    assert PAGES_PER_BLOCK == 2 and past // MB_BLOCK >= MB_TOPK

    def page_spec(half):
        def imap(i, h, s, sel_r, pt_r):
            blk = sel_r[(i * MB_HEADS + h) * MB_TOPK + s]
            return (base + pt_r[i, blk * PAGES_PER_BLOCK + half], 0, h)
        return pl.BlockSpec((1, PAGE_SIZE, MB_HD), imap)

    tok = pl.BlockSpec((1, 1, 1, MB_HD), lambda i, h, s, *_: (i, h, 0, 0))
    return pl.pallas_call(
        functools.partial(_moba_step_body, past=past),
        grid_spec=pltpu.PrefetchScalarGridSpec(
            num_scalar_prefetch=2,
            grid=(n, MB_HEADS, MB_TOPK),
            in_specs=[tok, tok, tok, pl.BlockSpec((1, 1, LANES), lambda i, h, s, *_: (h, 0, 0)),
                      page_spec(0), page_spec(1), page_spec(0), page_spec(1)],
            out_specs=tok,
            scratch_shapes=[pltpu.VMEM((1, 1), F32), pltpu.VMEM((1, 1), F32), pltpu.VMEM((1, MB_HD), F32)],
        ),
        out_shape=jax.ShapeDtypeStruct((n, MB_HEADS, 1, MB_HD), BF16),
        compiler_params=_params("parallel", "parallel", "arbitrary"),
        name="moba_step",
    )(sel, page_table, q, k_new, v_new, _alibi_slopes(), k_pool, k_pool, v_pool, v_pool)


def _merge_body(x_ref, a_ref, y5_ref, u_ref, c_ref, g_ref, d_ref, gluw, glub, wa, wb, wc, wo, lng, lnb, h_ref,
                *, alpha):
    y = jax.nn.gelu(y5_ref[...] + d_ref[...] * u_ref[...])
    b = y * jax.nn.sigmoid(_dot(y.astype(BF16), gluw[...]) + glub[...])
    dm = x_ref.shape[1]
    gate = lambda j: jax.nn.sigmoid(g_ref[:, j * dm:(j + 1) * dm].astype(F32))
    merged = (gate(0) * _dot(a_ref[...], wa[...]) + gate(1) * _dot(b.astype(BF16), wb[...])
              + gate(2) * _dot(c_ref[...], wc[...]))
    r = alpha * x_ref[...] + _dot(merged.astype(BF16), wo[...])
    h_ref[...] = _layer_norm(r, lng[...], lnb[...])


def _merge(x, a, y5, u, c, g, w, alpha, tm):
    m, dm = x.shape
    acts = (x, a, y5, u, c, g)
    consts = (w["s5_d"], w["glu_w"], w["glu_b"], w["wa"], w["wb"], w["wc"], w["wo"], w["ln1_g"], w["ln1_b"])
    return pl.pallas_call(
        functools.partial(_merge_body, alpha=alpha),
        grid=(m // tm,),
        in_specs=[pl.BlockSpec((tm, t.shape[1]), lambda i: (i, 0)) for t in acts]
        + [_const_spec(t.shape) for t in consts],
        out_specs=pl.BlockSpec((tm, dm), lambda i: (i, 0)),
        out_shape=jax.ShapeDtypeStruct((m, dm), F32),
        compiler_params=_params("parallel"),
        name="merge",
    )(*acts, *consts)


def _moe_body(h_ref, rw, rb, wgu, wd, sgu, sd, lng, lnb, y_ref, xb_s, gate_s, acc_s, *, alpha):
    e = pl.program_id(1)
    lane = lax.broadcasted_iota(jnp.int32, gate_s.shape, 1)

    @pl.when(e == 0)
    def _():
        xb = h_ref[...].astype(BF16)
        xb_s[...] = xb
        scores = jax.nn.sigmoid(_dot(xb, rw[...]))
        ranked = jnp.where(lane < N_EXPERTS, scores + rb[...], -jnp.inf)
        picked = _top_mask(ranked, lane, TOP_K) * scores
        gate_s[...] = picked / jnp.sum(picked, axis=-1, keepdims=True) * ROUTED_SCALE
        hs = _dot(xb, sgu[...])
        ff = sd.shape[0]
        acc_s[...] = _dot((_silu(hs[:, :ff]) * hs[:, ff:]).astype(BF16), sd[...])

    hg = _dot(xb_s[...], wgu[0])
    gcol = jnp.sum(jnp.where(lane == e, gate_s[...], 0.0), axis=-1, keepdims=True)
    hm = _silu(hg[:, :EXPERT_FF]) * hg[:, EXPERT_FF:] * gcol
    acc_s[...] += _dot(hm.astype(BF16), wd[0])

    @pl.when(e == pl.num_programs(1) - 1)
    def _():
        y_ref[...] = _layer_norm(alpha * h_ref[...] + acc_s[...], lng[...], lnb[...])


def _moe(h, w, alpha, tm):
    m, dm = h.shape
    consts_a = (w["router_w"], w["router_b"])
    consts_b = (w["sh_gu"], w["sh_d"], w["ln2_g"], w["ln2_b"])
    return pl.pallas_call(
        functools.partial(_moe_body, alpha=alpha),
        grid=(m // tm, N_EXPERTS),
        in_specs=[pl.BlockSpec((tm, dm), lambda i, e: (i, 0))]
        + [_const_spec(t.shape) for t in consts_a]
        + [pl.BlockSpec((1, dm, 2 * EXPERT_FF), lambda i, e: (e, 0, 0)),
           pl.BlockSpec((1, EXPERT_FF, dm), lambda i, e: (e, 0, 0))]
        + [_const_spec(t.shape) for t in consts_b],
        out_specs=pl.BlockSpec((tm, dm), lambda i, e: (i, 0)),
        out_shape=jax.ShapeDtypeStruct((m, dm), F32),
        scratch_shapes=[pltpu.VMEM((tm, dm), BF16), pltpu.VMEM((tm, LANES), F32), pltpu.VMEM((tm, dm), F32)],
        compiler_params=_params("parallel", "arbitrary"),
        name="moe",
    )(h, *consts_a, w["exp_gu"], w["exp_d"], *consts_b)


def _layer_weights(l, p, n_chunks):
    w_in = p["w_in"][l]
    o = 0
    cols = {}
    for name, size in (("dn", 4 * DN_WIDTH), ("ba", 2 * DN_HEADS), ("s5", S5_WIDTH), ("mq", MB_WIDTH),
                       ("kv", 2 * MB_WIDTH), ("g", 3 * w_in.shape[0])):
        cols[name] = w_in[:, o:o + size]
        o += size
    w = {"w_" + k: v.astype(BF16) for k, v in cols.items()}
    w["w_ba"] = jnp.pad(w["w_ba"], ((0, 0), (0, LANES - 2 * DN_HEADS)))
    row = lambda v: v.reshape(1, -1)
    w["conv_w"] = p["dn_conv_w"][l]
    w["dn_ab"] = _dn_gate_rows(p["dn_a_log"][l], p["dn_dt_bias"][l])
    w["dn_norm_w"] = row(p["dn_norm_w"][l])
    s5 = tuple(p[k][l] for k in ("s5_a_re", "s5_a_im", "s5_log_dt", "s5_b_re", "s5_b_im", "s5_c_re", "s5_c_im"))
    w["s5_prompt"] = _s5_prompt_weights(*s5, n_chunks)
    w["s5_step"] = _s5_step_weights(*s5)
    w["s5_d"] = row(p["s5_d"][l])
    w["glu_w"] = p["s5_glu_w"][l].astype(BF16)
    w["glu_b"] = row(p["s5_glu_b"][l])
    w["wa"] = p["w_branch_a"][l].astype(BF16)
    w["wb"] = p["w_branch_b"][l].astype(BF16)
    w["wc"] = p["w_branch_c"][l].astype(BF16)
    w["wo"] = p["w_out"][l].astype(BF16)
    w["ln1_g"], w["ln1_b"] = row(p["ln1_g"][l]), row(p["ln1_b"][l])
    w["ln2_g"], w["ln2_b"] = row(p["ln2_g"][l]), row(p["ln2_b"][l])
    w["router_w"] = jnp.pad(p["router_w"][l].astype(BF16), ((0, 0), (0, LANES - N_EXPERTS)))
    w["router_b"] = jnp.pad(row(p["router_bias"][l]), ((0, 0), (0, LANES - N_EXPERTS)))
    w["exp_gu"] = jnp.concatenate([p["exp_w_gate"][l], p["exp_w_up"][l]], axis=-1).astype(BF16)
    w["exp_d"] = p["exp_w_down"][l].astype(BF16)
    w["sh_gu"] = jnp.concatenate([p["sh_w_gate"][l], p["sh_w_up"][l]], axis=-1).astype(BF16)
    w["sh_d"] = p["sh_w_down"][l].astype(BF16)
    return w


def _prompt_layer(x, w, alpha, tm, tm_moe):
    n, l, dm = x.shape
    x2 = x.reshape(n * l, dm)
    pdn, pba, ps5, pmq, pkv, pg = _proj(x2, w, tm)
    a_out, s_fin = _deltanet_prompt(pdn.reshape(n, l, -1), pba.reshape(n, l, -1), w["conv_w"], w["dn_ab"],
                                    w["dn_norm_w"])
    y5, s5_re, s5_im = _s5_prompt(ps5.reshape(n, l, -1), w["s5_prompt"])
    c_out = _moba_prompt(pmq.reshape(n, l, -1), pkv.reshape(n, l, -1))
    h = _merge(x2, a_out.reshape(n * l, -1), y5.reshape(n * l, -1), ps5, c_out.reshape(n * l, -1), pg, w, alpha, tm)
    y = _moe(h, w, alpha, tm_moe)
    kv = pkv.reshape(n, l, 2, MB_HEADS, MB_HD)
    conv_new = pdn.reshape(n, l, -1)[:, l - (DN_CONV - 1):, :3 * DN_WIDTH].astype(F32)
    return y.reshape(n, l, dm), (kv[:, :, 0], kv[:, :, 1], conv_new, s_fin, s5_re, s5_im)


def _sample_layer(x, w, alpha, conv_buf, s_dn, s5_re, s5_im, k_pool, v_pool, page_table, base):
    n, l, dm = x.shape
    assert l == 1
    x2 = x.reshape(n, dm)
    pdn, pba, ps5, pmq, pkv, pg = _proj(x2, w, n)
    a_out, s_new = _deltanet_step(pdn.reshape(n, 1, -1), pba.reshape(n, 1, -1), conv_buf, s_dn, w["conv_w"],
                                  w["dn_ab"], w["dn_norm_w"])
    y5, s5_re_new, s5_im_new = _s5_step(ps5, s5_re, s5_im, w["s5_step"])
    sums = _block_key_sums(k_pool, page_table, base)
    sel = _moba_pick(pmq.reshape(n, 1, -1), sums)
    sel = sel[:, :MB_HEADS, :MB_TOPK].reshape(-1)
    heads = lambda t: t.reshape(n, MB_HEADS, 1, MB_HD)
    c_out = _moba_step(heads(pmq), heads(pkv[:, :MB_WIDTH]), heads(pkv[:, MB_WIDTH:]), sel, page_table,
                       k_pool, v_pool, base)
    h = _merge(x2, a_out.reshape(n, -1), y5, ps5, c_out.reshape(n, -1), pg, w, alpha, n)
    y = _moe(h, w, alpha, n)
    kv = pkv.reshape(n, 1, 2, MB_HEADS, MB_HD)
    conv_new = jnp.concatenate([conv_buf[:, 1:], pdn[:, None, :3 * DN_WIDTH].astype(F32)], axis=1)
    return y.reshape(n, 1, dm), (kv[:, :, 0], kv[:, :, 1], conv_new, s_new, s5_re_new, s5_im_new)


def kernel(x_prompt, x_sample, cache_k, cache_v, page_table, state_dn_conv, state_dn, state_s5_re, state_s5_im, w_in, dn_conv_w, dn_a_log, dn_dt_bias, dn_norm_w, s5_a_re, s5_a_im, s5_log_dt, s5_b_re, s5_b_im, s5_c_re, s5_c_im, s5_d, s5_glu_w, s5_glu_b, w_branch_a, w_branch_b, w_branch_c, w_out, ln1_g, ln1_b, router_w, router_bias, exp_w_gate, exp_w_up, exp_w_down, sh_w_gate, sh_w_up, sh_w_down, ln2_g, ln2_b):
    p = dict(w_in=w_in, dn_conv_w=dn_conv_w, dn_a_log=dn_a_log, dn_dt_bias=dn_dt_bias, dn_norm_w=dn_norm_w,
             s5_a_re=s5_a_re, s5_a_im=s5_a_im, s5_log_dt=s5_log_dt, s5_b_re=s5_b_re, s5_b_im=s5_b_im,
             s5_c_re=s5_c_re, s5_c_im=s5_c_im, s5_d=s5_d, s5_glu_w=s5_glu_w, s5_glu_b=s5_glu_b,
             w_branch_a=w_branch_a, w_branch_b=w_branch_b, w_branch_c=w_branch_c, w_out=w_out,
             ln1_g=ln1_g, ln1_b=ln1_b, router_w=router_w, router_bias=router_bias, exp_w_gate=exp_w_gate,
             exp_w_up=exp_w_up, exp_w_down=exp_w_down, sh_w_gate=sh_w_gate, sh_w_up=sh_w_up,
             sh_w_down=sh_w_down, ln2_g=ln2_g, ln2_b=ln2_b)
    depth = w_in.shape[0]
    alpha = (2 * depth) ** 0.25
    n_pool = cache_k.shape[1]
    k_pool = cache_k.reshape(depth * n_pool, PAGE_SIZE, MB_WIDTH)
    v_pool = cache_v.reshape(depth * n_pool, PAGE_SIZE, MB_WIDTH)
    seq = x_prompt.shape[1]
    tm = min(512, x_prompt.shape[0] * seq)
    tm_moe = min(1024, x_prompt.shape[0] * seq)
    hp, hs = x_prompt, x_sample
    p_st, s_st = [], []
    for l in range(depth):
        w = _layer_weights(l, p, seq // S5_T)
        hp, st = _prompt_layer(hp, w, alpha, tm, tm_moe)
        p_st.append(st)
        hs, st = _sample_layer(hs, w, alpha, state_dn_conv[l], state_dn[l], state_s5_re[l], state_s5_im[l],
                               k_pool, v_pool, page_table, l * n_pool)
        s_st.append(st)
    p_out = [jnp.stack(t) for t in zip(*p_st)]
    s_out = [jnp.stack(t) for t in zip(*s_st)]
    return (hp, hs, *p_out, *s_out)
```

```python
import functools
import math

import jax
import jax.numpy as jnp
import numpy as np
from jax import lax
from jax.experimental import pallas as pl
from jax.experimental.pallas import tpu as pltpu

F32 = jnp.float32
BF16 = jnp.bfloat16

DN_HEADS = 4
DN_DK = 128
DN_DV = 128
DN_WIDTH = DN_HEADS * DN_DK
DN_CONV = 4
DN_CHUNK = 64
S5_WIDTH = 512
S5_GROUP = 16
S5_GROUPS = S5_WIDTH // S5_GROUP
S5_STATE = 64
MB_HEADS = 4
MB_HD = 128
MB_WIDTH = MB_HEADS * MB_HD
MB_BLOCK = 256
MB_TOPK = 3
PAGE_SIZE = 128
N_EXPERTS = 64
TOP_K = 8
EXPERT_FF = 256
ROUTED_SCALE = 2.5
LN_EPS = 1e-5
NORM_EPS = 1e-6

LANES = 128
S5_T = 16
DN_TB = 256
VMEM_LIMIT = 56 * 1024 * 1024
NEG = -1e30


def _params(*sem):
    return pltpu.CompilerParams(dimension_semantics=sem, vmem_limit_bytes=VMEM_LIMIT)


def _const_spec(shape):
    nd = len(shape)
    return pl.BlockSpec(shape, lambda *_: (0,) * nd)


def _dot(a, b):
    return jnp.dot(a, b, preferred_element_type=F32)


def _dot_nt(a, b, precision=None):
    return lax.dot_general(a, b, (((1,), (1,)), ((), ())), preferred_element_type=F32, precision=precision)


def _dot_tn(a, b):
    return lax.dot_general(a, b, (((0,), (0,)), ((), ())), preferred_element_type=F32)


def _split(a):
    hi = a.astype(BF16)
    return hi, (a - hi.astype(F32)).astype(BF16)


def _split3(a):
    hi = a.astype(BF16)
    r = a - hi.astype(F32)
    mid = r.astype(BF16)
    return hi, mid, (r - mid.astype(F32)).astype(BF16)


def _mm3(a, b):
    a_hi, a_lo = _split(a)
    b_hi, b_lo = _split(b)
    return _dot(jnp.concatenate([a_hi, a_hi, a_lo], axis=1), jnp.concatenate([b_hi, b_lo, b_hi], axis=0))


def _silu(x):
    return x * jax.nn.sigmoid(x)


def _layer_norm(r, g, b):
    mu = jnp.mean(r, axis=-1, keepdims=True)
    c = r - mu
    var = jnp.mean(c * c, axis=-1, keepdims=True)
    return c * lax.rsqrt(var + LN_EPS) * g + b


def _proj_body(x_ref, wdn, wba, ws5, wmq, wkv, wg, odn, oba, os5, omq, okv, og, *, q_scale):
    xb = x_ref[...].astype(BF16)

    def mm(w_ref, o_ref, scale=None):
        n = w_ref.shape[1]
        step = min(n, 512)
        for c in range(0, n, step):
            r = _dot(xb, w_ref[:, c:c + step])
            if scale is not None:
                r = r * scale
            o_ref[:, c:c + step] = r.astype(o_ref.dtype)

    mm(wdn, odn)
    mm(wba, oba)
    mm(ws5, os5)
    mm(wmq, omq, q_scale)
    mm(wkv, okv)
    mm(wg, og)


def _proj(x, w, tm):
    m, d = x.shape
    outs = (("dn", BF16), ("ba", F32), ("s5", F32), ("mq", BF16), ("kv", F32), ("g", BF16))
    ws = [w["w_" + k] for k, _ in outs]
    return pl.pallas_call(
        functools.partial(_proj_body, q_scale=MB_HD ** -0.5),
        grid=(m // tm,),
        in_specs=[pl.BlockSpec((tm, d), lambda i: (i, 0))] + [_const_spec(wi.shape) for wi in ws],
        out_specs=[pl.BlockSpec((tm, wi.shape[1]), lambda i: (i, 0)) for wi in ws],
        out_shape=[jax.ShapeDtypeStruct((m, wi.shape[1]), dt) for wi, (_, dt) in zip(ws, outs)],
        compiler_params=_params("parallel"),
        name="proj",
    )(x, *ws)


def _dn_body(pdn_ref, pba_ref, cw_ref, ab_ref, nw_ref, o_ref, s_out_ref, buf, s_scr):
    tb = DN_TB
    t = pl.program_id(1)

    @pl.when(t == 0)
    def _():
        buf[0:8, :] = jnp.zeros((8, buf.shape[1]), F32)
        s_scr[...] = jnp.zeros(s_scr.shape, F32)

    buf[8:8 + tb, :] = pdn_ref[0, :, 0:3 * DN_WIDTH].astype(F32)

    def conv(c0):
        cs = slice(c0, c0 + DN_DK)
        y = cw_ref[3:4, cs] * buf[8:8 + tb, cs]
        for j in range(1, DN_CONV):
            y = y + cw_ref[3 - j:4 - j, cs] * buf[8 - j:8 - j + tb, cs]
        return _silu(y)

    def l2norm(x):
        return x * lax.rsqrt(jnp.sum(x * x, axis=-1, keepdims=True) + NORM_EPS)

    pba = pba_ref[0]
    beta_all = jax.nn.sigmoid(pba)
    g_all = -jnp.exp(ab_ref[0:1, :]) * jax.nn.softplus(pba + ab_ref[1:2, :])

    row = lax.broadcasted_iota(jnp.int32, (tb, tb), 0)
    col = lax.broadcasted_iota(jnp.int32, (tb, tb), 1)
    same64 = (row >> 6) == (col >> 6)
    same32 = (row >> 5) == (col >> 5)
    same16 = (row >> 4) == (col >> 4)
    incl = jnp.logical_and(same64, row >= col)
    strict = jnp.logical_and(same64, row > col)
    g_hi, g_mid, g_lo = _split3(g_all)
    ones_incl = jnp.where(incl, 1.0, 0.0).astype(BF16)
    gc = _dot(jnp.concatenate([ones_incl] * 3, axis=1), jnp.concatenate([g_hi, g_mid, g_lo], axis=0))
    gl = jnp.concatenate(
        [jnp.broadcast_to(gc[(c + 1) * DN_CHUNK - 1:(c + 1) * DN_CHUNK, :], (DN_CHUNK, LANES))
         for c in range(tb // DN_CHUNK)], axis=0)
    gct = gc.T
    eye = jnp.where(row == col, 1.0, 0.0)

    heads = range(DN_HEADS)
    q = [l2norm(conv(h * DN_DK)) * DN_DK ** -0.5 for h in heads]
    k = [l2norm(conv(DN_WIDTH + h * DN_DK)) for h in heads]
    v = [conv(2 * DN_WIDTH + h * DN_DK) for h in heads]
    beta = [beta_all[:, h:h + 1] for h in heads]
    gcol = [gc[:, DN_HEADS + h:DN_HEADS + h + 1] for h in heads]
    glast = [gl[:, DN_HEADS + h:DN_HEADS + h + 1] for h in heads]
    dec = [jnp.exp(jnp.where(incl, gcol[h] - gct[DN_HEADS + h:DN_HEADS + h + 1, :], NEG)) for h in heads]
    kb = [k[h] * beta[h] for h in heads]
    k16 = [k[h].astype(BF16) for h in heads]
    a = [jnp.where(strict, _dot_nt(kb[h].astype(BF16), k16[h]) * dec[h], 0.0) for h in heads]
    qk = [_dot_nt(q[h].astype(BF16), k16[h]) * dec[h] for h in heads]

    n1 = [jnp.where(same16, -a[h], 0.0) for h in heads]
    n2 = [_mm3(n1[h], n1[h]) for h in heads]
    n4 = [_mm3(n2[h], n2[h]) for h in heads]
    n8 = [_mm3(n4[h], n4[h]) for h in heads]
    p = [eye + n1[h] for h in heads]
    p = [p[h] + _mm3(p[h], n2[h]) for h in heads]
    p = [p[h] + _mm3(p[h], n4[h]) for h in heads]
    p = [p[h] + _mm3(p[h], n8[h]) for h in heads]
    off16 = jnp.logical_and(same32, jnp.logical_not(same16))
    t1 = [_mm3(jnp.where(off16, a[h], 0.0), p[h]) for h in heads]
    p = [p[h] - _mm3(p[h], t1[h]) for h in heads]
    t2 = [_mm3(jnp.where(same32, 0.0, a[h]), p[h]) for h in heads]
    p = [p[h] - _mm3(p[h], t2[h]) for h in heads]

    eg = [jnp.exp(gcol[h]) for h in heads]
    sol = [_mm3(p[h], jnp.concatenate([v[h] * beta[h], kb[h] * eg[h]], axis=1)) for h in heads]
    qd = [q[h] * eg[h] for h in heads]
    kd = [k[h] * jnp.exp(glast[h] - gcol[h]) for h in heads]

    sh = [s_scr[h] for h in heads]
    for c in range(tb // DN_CHUNK):
        r = slice(c * DN_CHUNK, (c + 1) * DN_CHUNK)
        for h in heads:
            wq = jnp.concatenate([sol[h][r, DN_DV:], qd[h][r]], axis=0).astype(BF16)
            ws_ = _dot(wq, sh[h].astype(BF16))
            vn16 = (sol[h][r, :DN_DV] - ws_[:DN_CHUNK]).astype(BF16)
            o = ws_[DN_CHUNK:] + _dot(qk[h][r, r].astype(BF16), vn16)
            sh[h] = (sh[h] * jnp.exp(glast[h][c * DN_CHUNK:c * DN_CHUNK + 1, :])
                     + _dot_tn(kd[h][r].astype(BF16), vn16))
            z = pdn_ref[0, r, 3 * DN_WIDTH + h * DN_DV:3 * DN_WIDTH + (h + 1) * DN_DV].astype(F32)
            o = o * lax.rsqrt(jnp.mean(o * o, axis=-1, keepdims=True) + NORM_EPS) * nw_ref[...] * _silu(z)
            o_ref[0, r, h * DN_DV:(h + 1) * DN_DV] = o.astype(o_ref.dtype)
    for h in heads:
        s_scr[h] = sh[h]

    buf[0:8, :] = buf[tb:tb + 8, :]

    @pl.when(t == pl.num_programs(1) - 1)
    def _():
        s_out_ref[0] = s_scr[...]


def _dn_gate_rows(a_log, dt_bias):
    ab = jnp.zeros((2, LANES), F32)
    ab = ab.at[0, DN_HEADS:2 * DN_HEADS].set(a_log)
    return ab.at[1, DN_HEADS:2 * DN_HEADS].set(dt_bias)


def _deltanet_prompt(pdn, pba, conv_w, ab, norm_w):
    n, l, _ = pdn.shape
    assert l % DN_TB == 0
    return pl.pallas_call(
        _dn_body,
        grid=(n, l // DN_TB),
        in_specs=[
            pl.BlockSpec((1, DN_TB, 4 * DN_WIDTH), lambda i, t: (i, t, 0)),
            pl.BlockSpec((1, DN_TB, LANES), lambda i, t: (i, t, 0)),
            _const_spec(conv_w.shape), _const_spec(ab.shape), _const_spec(norm_w.shape),
        ],
        out_specs=[
            pl.BlockSpec((1, DN_TB, DN_WIDTH), lambda i, t: (i, t, 0)),
            pl.BlockSpec((1, DN_HEADS, DN_DK, DN_DV), lambda i, t: (i, 0, 0, 0)),
        ],
        out_shape=[
            jax.ShapeDtypeStruct((n, l, DN_WIDTH), BF16),
            jax.ShapeDtypeStruct((n, DN_HEADS, DN_DK, DN_DV), F32),
        ],
        scratch_shapes=[pltpu.VMEM((DN_TB + 8, 3 * DN_WIDTH), F32), pltpu.VMEM((DN_HEADS, DN_DK, DN_DV), F32)],
        compiler_params=_params("parallel", "arbitrary"),
        name="deltanet_prompt",
    )(pdn, pba, conv_w, ab, norm_w)


def _dn_step_body(pdn_ref, pba_ref, cbuf_ref, s0_ref, cw_ref, ab_ref, nw_ref, o_ref, s_out_ref):
    x = pdn_ref[0, :, 0:3 * DN_WIDTH].astype(F32)
    y = cw_ref[3:4, :] * x
    for j in range(1, DN_CONV):
        y = y + cw_ref[3 - j:4 - j, :] * cbuf_ref[0, 3 - j:4 - j, :]
    qkv = _silu(y)
    pba = pba_ref[0]
    beta_all = jax.nn.sigmoid(pba)
    g_all = -jnp.exp(ab_ref[0:1, :]) * jax.nn.softplus(pba + ab_ref[1:2, :])

    def l2norm(t):
        return t * lax.rsqrt(jnp.sum(t * t, axis=-1, keepdims=True) + NORM_EPS)

    sub = lax.broadcasted_iota(jnp.int32, (8, DN_DK), 0)
    for h in range(DN_HEADS):
        q = l2norm(qkv[:, h * DN_DK:(h + 1) * DN_DK]) * DN_DK ** -0.5
        k = l2norm(qkv[:, DN_WIDTH + h * DN_DK:DN_WIDTH + (h + 1) * DN_DK])
        v = qkv[:, 2 * DN_WIDTH + h * DN_DV:2 * DN_WIDTH + (h + 1) * DN_DV]
        beta = beta_all[:, h:h + 1]
        eg = jnp.exp(g_all[:, DN_HEADS + h:DN_HEADS + h + 1])
        s0 = s0_ref[0, h]
        w = k * beta * eg
        lhs = jnp.where(sub == 0, w, jnp.where(sub == 1, q * eg, 0.0))
        ws_ = _dot(lhs.astype(BF16), s0.astype(BF16))
        vn = v * beta - ws_[0:1, :]
        o = ws_[1:2, :] + jnp.sum(q * k, axis=-1, keepdims=True) * vn
        kt = jnp.broadcast_to(k, (DN_DK, DN_DK)).T
        s_out_ref[0, h] = s0 * eg + kt * vn
        z = pdn_ref[0, :, 3 * DN_WIDTH + h * DN_DV:3 * DN_WIDTH + (h + 1) * DN_DV].astype(F32)
        o = o * lax.rsqrt(jnp.mean(o * o, axis=-1, keepdims=True) + NORM_EPS) * nw_ref[...] * _silu(z)
        o_ref[0, :, h * DN_DV:(h + 1) * DN_DV] = o.astype(o_ref.dtype)


def _deltanet_step(pdn, pba, conv_buf, s0, conv_w, ab, norm_w):
    n = pdn.shape[0]
    return pl.pallas_call(
        _dn_step_body,
        grid=(n,),
        in_specs=[
            pl.BlockSpec((1, 1, 4 * DN_WIDTH), lambda i: (i, 0, 0)),
            pl.BlockSpec((1, 1, LANES), lambda i: (i, 0, 0)),
            pl.BlockSpec((1, DN_CONV - 1, 3 * DN_WIDTH), lambda i: (i, 0, 0)),
            pl.BlockSpec((1, DN_HEADS, DN_DK, DN_DV), lambda i: (i, 0, 0, 0)),
            _const_spec(conv_w.shape), _const_spec(ab.shape), _const_spec(norm_w.shape),
        ],
        out_specs=[
            pl.BlockSpec((1, 1, DN_WIDTH), lambda i: (i, 0, 0)),
            pl.BlockSpec((1, DN_HEADS, DN_DK, DN_DV), lambda i: (i, 0, 0, 0)),
        ],
        out_shape=[
            jax.ShapeDtypeStruct((n, 1, DN_WIDTH), BF16),
            jax.ShapeDtypeStruct((n, DN_HEADS, DN_DK, DN_DV), F32),
        ],
        compiler_params=_params("parallel"),
        name="deltanet_step",
    )(pdn, pba, conv_buf, s0, conv_w, ab, norm_w)


def _s5_discretize(a_re, a_im, log_dt, b_re, b_im):
    dt = jnp.exp(log_dt)[:, None]
    mag = jnp.exp(a_re * dt)
    abr, abi = mag * jnp.cos(a_im * dt), mag * jnp.sin(a_im * dt)
    den = a_re * a_re + a_im * a_im
    fr = ((abr - 1.0) * a_re + abi * a_im) / den
    fi = (abi * a_re - (abr - 1.0) * a_im) / den
    bbr = fr[..., None] * b_re - fi[..., None] * b_im
    bbi = fr[..., None] * b_im + fi[..., None] * b_re
    return abr, abi, bbr, bbi


def _s5_powers(a_re, a_im, log_dt, taus):
    dt = jnp.exp(log_dt)[None, :, None]
    tau = jnp.asarray(taus, F32)[:, None, None]
    mag = jnp.exp(a_re[None] * dt * tau)
    ang = a_im[None] * dt * tau
    return mag * jnp.cos(ang), mag * jnp.sin(ang)


def _s5_prompt_weights(a_re, a_im, log_dt, b_re, b_im, c_re, c_im, n_chunks):
    t = S5_T
    hp = lax.Precision.HIGHEST
    _, _, bbr, bbi = _s5_discretize(a_re, a_im, log_dt, b_re, b_im)
    lr, li = _s5_powers(a_re, a_im, log_dt, list(range(t + 1)))
    lbr = lr[..., None] * bbr[None] - li[..., None] * bbi[None]
    lbi = lr[..., None] * bbi[None] + li[..., None] * bbr[None]
    kt = (jnp.einsum("ghp,tgpk->tghk", c_re, lbr, precision=hp)
          - jnp.einsum("ghp,tgpk->tghk", c_im, lbi, precision=hp))
    lag = np.arange(t)[None, :] - np.arange(t)[:, None]
    toep = jnp.where((lag >= 0)[:, :, None, None, None], kt[np.clip(lag, 0, t)], 0.0)
    toep = toep.transpose(2, 0, 4, 1, 3).reshape(S5_GROUPS, t * S5_GROUP, t * S5_GROUP)
    wz_r = lbr[t - 1 - np.arange(t)].transpose(1, 0, 3, 2).reshape(S5_GROUPS, t * S5_GROUP, S5_STATE)
    wz_i = lbi[t - 1 - np.arange(t)].transpose(1, 0, 3, 2).reshape(S5_GROUPS, t * S5_GROUP, S5_STATE)
    wz = jnp.concatenate([wz_r, wz_i], axis=-1)
    lr1, li1 = lr[1:], li[1:]
    wy_r = c_re[None] * lr1[:, :, None, :] - c_im[None] * li1[:, :, None, :]
    wy_i = -(c_re[None] * li1[:, :, None, :] + c_im[None] * lr1[:, :, None, :])
    wy = jnp.concatenate([wy_r, wy_i], axis=-1).transpose(1, 3, 0, 2).reshape(
        S5_GROUPS, 2 * S5_STATE, t * S5_GROUP)
    n_steps = max(1, int(math.ceil(math.log2(n_chunks))))
    sr, si = _s5_powers(a_re, a_im, log_dt, [t * (1 << i) for i in range(n_steps)])
    sc_r = jnp.concatenate([sr, sr], axis=-1).transpose(1, 0, 2)
    sc_i = jnp.concatenate([-si, si], axis=-1).transpose(1, 0, 2)
    return toep.astype(BF16), wz.astype(BF16), wy.astype(BF16), sc_r, sc_i


def _s5_body(u_ref, toep_ref, wz_ref, wy_ref, lr_ref, li_ref, y_ref, sfin_ref):
    u = u_ref[0, 0]
    z = _dot(u, wz_ref[0])
    nc = z.shape[0]
    rowi = lax.broadcasted_iota(jnp.int32, z.shape, 0)
    s = z
    for i in range(lr_ref.shape[1]):
        d = 1 << i
        if d >= nc:
            break
        sh = jnp.where(rowi >= d, pltpu.roll(s, d, 0), 0.0)
        s = s + lr_ref[0, i:i + 1, :] * sh + li_ref[0, i:i + 1, :] * pltpu.roll(sh, S5_STATE, 1)
    sfin_ref[0, 0] = s[nc - 1:nc, :]
    sprev = jnp.where(rowi >= 1, pltpu.roll(s, 1, 0), 0.0)
    y_ref[0, 0] = _dot(u, toep_ref[0]) + _dot(sprev.astype(BF16), wy_ref[0])


def _s5_prompt(u, wts):
    toep, wz, wy, sc_r, sc_i = wts
    n, l, _ = u.shape
    nc = l // S5_T
    tw = S5_T * S5_GROUP
    ug = u.astype(BF16).reshape(n, nc, S5_T, S5_GROUPS, S5_GROUP).transpose(3, 0, 1, 2, 4).reshape(
        S5_GROUPS, n, nc, tw)
    wspec = lambda shp: pl.BlockSpec((1,) + shp[1:], lambda g, i: (g, 0, 0))
    y, sfin = pl.pallas_call(
        _s5_body,
        grid=(S5_GROUPS, n),
        in_specs=[pl.BlockSpec((1, 1, nc, tw), lambda g, i: (g, i, 0, 0)),
                  wspec(toep.shape), wspec(wz.shape), wspec(wy.shape), wspec(sc_r.shape), wspec(sc_i.shape)],
        out_specs=[pl.BlockSpec((1, 1, nc, tw), lambda g, i: (g, i, 0, 0)),
                   pl.BlockSpec((1, 1, 1, 2 * S5_STATE), lambda g, i: (g, i, 0, 0))],
        out_shape=[jax.ShapeDtypeStruct((S5_GROUPS, n, nc, tw), F32),
                   jax.ShapeDtypeStruct((S5_GROUPS, n, 1, 2 * S5_STATE), F32)],
        compiler_params=_params("parallel", "parallel"),
        name="s5_prompt",
    )(ug, toep, wz, wy, sc_r, sc_i)
    y = y.reshape(S5_GROUPS, n, nc, S5_T, S5_GROUP).transpose(1, 2, 3, 0, 4).reshape(n, l, S5_WIDTH)
    sfin = sfin[:, :, 0, :].transpose(1, 0, 2)
    return y, sfin[..., :S5_STATE], sfin[..., S5_STATE:]


def _s5_step_weights(a_re, a_im, log_dt, b_re, b_im, c_re, c_im):
    abr, abi, bbr, bbi = _s5_discretize(a_re, a_im, log_dt, b_re, b_im)
    eye = jnp.eye(S5_GROUPS, dtype=F32)
    gp = S5_GROUPS * S5_STATE
    wb = lambda b: jnp.einsum("gph,gk->ghkp", b, eye).reshape(S5_WIDTH, gp).astype(BF16)
    wc = lambda c: jnp.einsum("ghp,gk->gpkh", c, eye).reshape(gp, S5_WIDTH).astype(BF16)
    return (abr.reshape(1, gp), abi.reshape(1, gp), wb(bbr), wb(bbi), wc(c_re), wc(c_im))


def _s5_step_body(u_ref, s0r_ref, s0i_ref, abr_ref, abi_ref, wbr, wbi, wcr, wci, y_ref, sr_ref, si_ref):
    u = u_ref[...].astype(BF16)
    abr, abi = abr_ref[...], abi_ref[...]
    s0r, s0i = s0r_ref[...], s0i_ref[...]
    sr = abr * s0r - abi * s0i + _dot(u, wbr[...])
    si = abr * s0i + abi * s0r + _dot(u, wbi[...])
    sr_ref[...] = sr
    si_ref[...] = si
    y_ref[...] = _dot(sr.astype(BF16), wcr[...]) - _dot(si.astype(BF16), wci[...])


def _s5_step(u, s0_re, s0_im, wts):
    n = u.shape[0]
    gp = S5_GROUPS * S5_STATE
    args = (u, s0_re.reshape(n, gp), s0_im.reshape(n, gp)) + tuple(wts)
    y, sr, si = pl.pallas_call(
        _s5_step_body,
        grid=(1,),
        in_specs=[_const_spec(a.shape) for a in args],
        out_specs=[_const_spec((n, S5_WIDTH)), _const_spec((n, gp)), _const_spec((n, gp))],
        out_shape=[jax.ShapeDtypeStruct((n, S5_WIDTH), F32), jax.ShapeDtypeStruct((n, gp), F32),
                   jax.ShapeDtypeStruct((n, gp), F32)],
        compiler_params=_params("arbitrary"),
        name="s5_step",
    )(*args)
    return y, sr.reshape(n, S5_GROUPS, S5_STATE), si.reshape(n, S5_GROUPS, S5_STATE)


def _top_mask(gate, pos, count, axis=-1):
    sel = jnp.zeros(gate.shape, F32)
    for _ in range(count):
        m = jnp.max(gate, axis=axis, keepdims=True)
        idx = jnp.min(jnp.where(gate == m, pos, 2 ** 30), axis=axis, keepdims=True)
        hit = jnp.logical_and(pos == idx, m > -jnp.inf)
        sel = jnp.where(hit, 1.0, sel)
        gate = jnp.where(pos == idx, -jnp.inf, gate)
    return sel


def _alibi_slopes():
    s = np.exp2(-8.0 * np.arange(1, MB_HEADS + 1, dtype=np.float32) / MB_HEADS).astype(np.float32)
    return jnp.asarray(np.broadcast_to(s[:, None, None], (MB_HEADS, 1, LANES)))


MB_SLOTS = 16
MB_V_ROWS = MB_HD + 16
MB_GROUP = 2
MB_HEADS_PER_STEP = 4


def _moba_body(q_ref, k_ref, v_ref, slope_ref, o_ref, kaug_s, vaug_s, means_s, *, nb):
    blk = MB_BLOCK
    qi = pl.program_id(2)
    heads = range(MB_HEADS_PER_STEP)
    cols = [slice(h * MB_HD, (h + 1) * MB_HD) for h in heads]
    slope = [slope_ref[h, :, 0:1] for h in heads]

    @pl.when(qi == 0)
    def _():
        lane = lax.broadcasted_iota(jnp.int32, (blk, MB_HD), 1)
        off = lax.broadcasted_iota(jnp.int32, (blk, MB_HD), 0).astype(F32)
        means_s[...] = jnp.zeros(means_s.shape, F32)
        vaug_s[:, :, MB_HD:, :] = jnp.ones((len(heads), nb, MB_V_ROWS - MB_HD, blk), BF16)
        for b in range(nb):
            for h in heads:
                kb = k_ref[0, b * blk:(b + 1) * blk, cols[h]]
                means_s[h, b:b + 1, :] = jnp.mean(kb, axis=0, keepdims=True)
                feat = jnp.where(lane == b, 1.0, 0.0)
                feat = jnp.where(lane == MB_SLOTS, slope[h] * off, feat)
                feat = jnp.where(lane == MB_SLOTS + 1, slope[h] * float(b * blk), feat)
                feat = jnp.where(lane == MB_SLOTS + 2, 1.0, feat)
                kaug_s[h, b] = jnp.concatenate([kb.astype(BF16), feat.astype(BF16)], axis=1)
                vaug_s[h, b, 0:MB_HD, :] = v_ref[0, b * blk:(b + 1) * blk, cols[h]].T.astype(BF16)

    slot = lax.broadcasted_iota(jnp.int32, (MB_SLOTS, blk), 0)
    pad = jnp.zeros((MB_HD - 2 * MB_SLOTS, blk), BF16)
    qt = [q_ref[0, :, cols[h]].astype(F32).T for h in heads]
    gate = [jnp.dot(means_s[h], qt[h], precision=lax.Precision.HIGHEST, preferred_element_type=F32) for h in heads]
    sel = [_top_mask(jnp.where(slot < qi, gate[h], -jnp.inf), slot, MB_TOPK, axis=0) for h in heads]
    bias_rows = [jnp.where(slot < 2, 1.0, jnp.where(slot == 2, -slope[h] * (qi * blk).astype(F32), 0.0)).astype(BF16)
                 for h in heads]
    qt16 = [qt[h].astype(BF16) for h in heads]
    q_past = [jnp.concatenate([qt16[h], jnp.where(sel[h] > 0.0, 0.0, NEG).astype(BF16), bias_rows[h], pad], axis=0)
              for h in heads]
    q_own = [jnp.concatenate([qt16[h], jnp.zeros((MB_SLOTS, blk), BF16), bias_rows[h], pad], axis=0) for h in heads]

    key = lax.broadcasted_iota(jnp.int32, (blk, blk), 0)
    qry = lax.broadcasted_iota(jnp.int32, (blk, blk), 1)
    s = [jnp.where(key <= qry, _dot(kaug_s[h, qi], q_own[h]), NEG) for h in heads]
    m = [jnp.max(s[h], axis=0, keepdims=True) for h in heads]
    acc = [_dot(vaug_s[h, qi], jnp.exp(s[h] - m[h]).astype(BF16)) for h in heads]

    grp = MB_GROUP

    def body(i, carry):
        m_old, acc = carry
        s = [[_dot(kaug_s[h, grp * i + g], q_past[h]) for g in range(grp)] for h in heads]
        m_new = list(m_old)
        for g in range(grp):
            m_new = [jnp.maximum(m_new[h], jnp.max(s[h][g], axis=0, keepdims=True)) for h in heads]
        p = [[jnp.exp(s[h][g] - m_new[h]).astype(BF16) for g in range(grp)] for h in heads]
        acc = [jnp.exp(m_old[h] - m_new[h]) * acc[h] for h in heads]
        for g in range(grp):
            acc = [acc[h] + _dot(vaug_s[h, grp * i + g], p[h][g]) for h in heads]
        return tuple(m_new), tuple(acc)

    m, acc = lax.fori_loop(0, (qi + grp - 1) // grp, body, (tuple(m), tuple(acc)))
    for h in heads:
        o_ref[0, :, cols[h]] = (acc[h][0:MB_HD] / acc[h][MB_HD:MB_HD + 1]).T.astype(o_ref.dtype)


def _moba_prompt(q, kv):
    n, l, _ = q.shape
    nb = l // MB_BLOCK
    hps = MB_HEADS_PER_STEP
    width = hps * MB_HD
    assert l % MB_BLOCK == 0 and nb <= MB_SLOTS and nb % MB_GROUP == 0 and MB_HEADS % hps == 0
    return pl.pallas_call(
        functools.partial(_moba_body, nb=nb),
        grid=(n, MB_HEADS // hps, nb),
        in_specs=[
            pl.BlockSpec((1, MB_BLOCK, width), lambda i, h, t: (i, t, h)),
            pl.BlockSpec((1, l, width), lambda i, h, t: (i, 0, h)),
            pl.BlockSpec((1, l, width), lambda i, h, t: (i, 0, MB_HEADS // hps + h)),
            pl.BlockSpec((hps, 1, LANES), lambda i, h, t: (h, 0, 0)),
        ],
        out_specs=pl.BlockSpec((1, MB_BLOCK, width), lambda i, h, t: (i, t, h)),
        out_shape=jax.ShapeDtypeStruct((n, l, MB_WIDTH), BF16),
        scratch_shapes=[pltpu.VMEM((hps, nb, MB_BLOCK, 2 * MB_HD), BF16),
                        pltpu.VMEM((hps, nb, MB_V_ROWS, MB_BLOCK), BF16),
                        pltpu.VMEM((hps, MB_SLOTS, MB_HD), F32)],
        compiler_params=_params("parallel", "parallel", "arbitrary"),
        name="moba_prompt",
    )(q, kv, kv, _alibi_slopes())


PAGES_PER_STEP = 16
PAGES_PER_BLOCK = MB_BLOCK // PAGE_SIZE


def _page_sum_body(pt_ref, *refs):
    del pt_ref
    o_ref = refs[-1]
    for b in range(PAGES_PER_STEP // PAGES_PER_BLOCK):
        acc = jnp.sum(refs[PAGES_PER_BLOCK * b][...], axis=0)
        for j in range(1, PAGES_PER_BLOCK):
            acc = acc + jnp.sum(refs[PAGES_PER_BLOCK * b + j][...], axis=0)
        o_ref[b] = acc


def _block_key_sums(pool, page_table, layer):
    n, n_pages = page_table.shape
    assert n_pages % PAGES_PER_STEP == 0
    tail = pool.shape[2:]

    def page_spec(j):
        return pl.BlockSpec((None, None) + tail, lambda i, s, pt: (layer, pt[i, s * PAGES_PER_STEP + j], 0, 0, 0))

    blocks_per_step = PAGES_PER_STEP // PAGES_PER_BLOCK
    return pl.pallas_call(
        _page_sum_body,
        grid_spec=pltpu.PrefetchScalarGridSpec(
            num_scalar_prefetch=1,
            grid=(n, n_pages // PAGES_PER_STEP),
            in_specs=[page_spec(j) for j in range(PAGES_PER_STEP)],
            out_specs=pl.BlockSpec((None, blocks_per_step) + tail[1:], lambda i, s, pt: (i, s, 0, 0)),
        ),
        out_shape=jax.ShapeDtypeStruct((n, n_pages // PAGES_PER_BLOCK) + tail[1:], F32),
        compiler_params=_params("parallel", "arbitrary"),
        name="moba_page_sums",
    )(page_table, *([pool] * PAGES_PER_STEP))


def _moba_pick_body(q_ref, sums_ref, sel_ref):
    nbk = sums_ref.shape[0]
    out = jnp.zeros(sel_ref.shape, jnp.int32)
    osub = lax.broadcasted_iota(jnp.int32, out.shape, 0)
    olane = lax.broadcasted_iota(jnp.int32, out.shape, 1)
    bidx = lax.broadcasted_iota(jnp.int32, (nbk, 1), 0)
    for h in range(MB_HEADS):
        means = sums_ref[:, h, :] / float(MB_BLOCK)
        q = q_ref[:, h * MB_HD:(h + 1) * MB_HD].astype(F32)
        gate = jnp.sum(means * q, axis=-1, keepdims=True)
        for rnk in range(MB_TOPK):
            m = jnp.max(gate, axis=0, keepdims=True)
            idx = jnp.min(jnp.where(gate == m, bidx, 2 ** 30), axis=0, keepdims=True)
            out = jnp.where(jnp.logical_and(osub == h, olane == rnk), idx, out)
            gate = jnp.where(bidx == idx, -jnp.inf, gate)
    sel_ref[...] = out


def _moba_pick(q, sums):
    n = q.shape[0]
    return pl.pallas_call(
        _moba_pick_body,
        grid=(n,),
        in_specs=[pl.BlockSpec((None, 1, MB_WIDTH), lambda i: (i, 0, 0)),
                  pl.BlockSpec((None,) + sums.shape[1:], lambda i: (i, 0, 0, 0))],
        out_specs=pl.BlockSpec((None, 8, LANES), lambda i: (i, 0, 0)),
        out_shape=jax.ShapeDtypeStruct((n, 8, LANES), jnp.int32),
        compiler_params=_params("parallel"),
        name="moba_pick",
    )(q, sums)


def _moba_step_body(sel_ref, pt_ref, q_ref, kn_ref, vn_ref, slope_ref, *refs, past):
    del pt_ref
    pages, o_ref, m_s, l_s, acc_s = refs[:4 * MB_HEADS], refs[4 * MB_HEADS], *refs[4 * MB_HEADS + 1:]
    i, s = pl.program_id(0), pl.program_id(1)
    sub = lax.broadcasted_iota(jnp.int32, (PAGE_SIZE, 1), 0)
    for h in range(MB_HEADS):
        q = q_ref[0, h].astype(F32)
        slope = slope_ref[h, :, 0:1]

        @pl.when(s == 0)
        def _():
            m_s[h] = jnp.sum(q * kn_ref[0, h], axis=-1, keepdims=True)
            l_s[h] = jnp.ones((1, 1), F32)
            acc_s[h] = vn_ref[0, h]

        blk = sel_ref[(i * MB_HEADS + h) * MB_TOPK + s]
        k0, k1, v0, v1 = pages[4 * h:4 * h + 4]
        m_old = m_s[h]
        sc = []
        for half, kr in enumerate((k0, k1)):
            kpos = blk * MB_BLOCK + half * PAGE_SIZE + sub
            sc.append(jnp.sum(kr[:, h, :] * q, axis=-1, keepdims=True) - slope * (past - kpos).astype(F32))
        m_new = jnp.maximum(m_old, jnp.maximum(jnp.max(sc[0], axis=0, keepdims=True),
                                               jnp.max(sc[1], axis=0, keepdims=True)))
        alpha = jnp.exp(m_old - m_new)
        p0 = jnp.exp(sc[0] - m_new)
        p1 = jnp.exp(sc[1] - m_new)
        l_new = alpha * l_s[h] + jnp.sum(p0, axis=0, keepdims=True) + jnp.sum(p1, axis=0, keepdims=True)
        acc_new = (alpha * acc_s[h] + jnp.sum(p0 * v0[:, h, :], axis=0, keepdims=True)
                   + jnp.sum(p1 * v1[:, h, :], axis=0, keepdims=True))
        m_s[h] = m_new
        l_s[h] = l_new
        acc_s[h] = acc_new

        @pl.when(s == MB_TOPK - 1)
        def _():
            o_ref[0, h] = (acc_new / l_new).astype(o_ref.dtype)


def _moba_step(q, k_new, v_new, sel, page_table, k_pool, v_pool, layer):
    n = q.shape[0]
    past = page_table.shape[1] * PAGE_SIZE
    assert PAGES_PER_BLOCK == 2 and past // MB_BLOCK >= MB_TOPK
    tail = k_pool.shape[2:]

    def page_spec(h, half):
        def imap(i, s, sel_r, pt_r):
            blk = sel_r[(i * MB_HEADS + h) * MB_TOPK + s]
            return (layer, pt_r[i, blk * PAGES_PER_BLOCK + half], 0, 0, 0)
        return pl.BlockSpec((None, None) + tail, imap)

    page_specs, page_args = [], []
    for h in range(MB_HEADS):
        for pool in (k_pool, v_pool):
            for half in range(PAGES_PER_BLOCK):
                page_specs.append(page_spec(h, half))
                page_args.append(pool)
    tok = pl.BlockSpec((1, MB_HEADS, 1, MB_HD), lambda i, s, *_: (i, 0, 0, 0))
    return pl.pallas_call(
        functools.partial(_moba_step_body, past=past),
        grid_spec=pltpu.PrefetchScalarGridSpec(
            num_scalar_prefetch=2,
            grid=(n, MB_TOPK),
            in_specs=[tok, tok, tok, _const_spec((MB_HEADS, 1, LANES))] + page_specs,
            out_specs=tok,
            scratch_shapes=[pltpu.VMEM((MB_HEADS, 1, 1), F32), pltpu.VMEM((MB_HEADS, 1, 1), F32),
                            pltpu.VMEM((MB_HEADS, 1, MB_HD), F32)],
        ),
        out_shape=jax.ShapeDtypeStruct((n, MB_HEADS, 1, MB_HD), BF16),
        compiler_params=_params("parallel", "arbitrary"),
        name="moba_step",
    )(sel, page_table, q, k_new, v_new, _alibi_slopes(), *page_args)


def _merge_body(x_ref, a_ref, y5_ref, u_ref, c_ref, g_ref, d_ref, gluw, glub, wa, wb, wc, wo, lng, lnb, h_ref,
                *, alpha):
    y = jax.nn.gelu(y5_ref[...] + d_ref[...] * u_ref[...])
    b = y * jax.nn.sigmoid(_dot(y.astype(BF16), gluw[...]) + glub[...])
    dm = x_ref.shape[1]
    gate = lambda j: jax.nn.sigmoid(g_ref[:, j * dm:(j + 1) * dm].astype(F32))
    merged = (gate(0) * _dot(a_ref[...], wa[...]) + gate(1) * _dot(b.astype(BF16), wb[...])
              + gate(2) * _dot(c_ref[...], wc[...]))
    r = alpha * x_ref[...] + _dot(merged.astype(BF16), wo[...])
    h_ref[...] = _layer_norm(r, lng[...], lnb[...])


def _merge(x, a, y5, u, c, g, w, alpha, tm):
    m, dm = x.shape
    acts = (x, a, y5, u, c, g)
    consts = (w["s5_d"], w["glu_w"], w["glu_b"], w["wa"], w["wb"], w["wc"], w["wo"], w["ln1_g"], w["ln1_b"])
    return pl.pallas_call(
        functools.partial(_merge_body, alpha=alpha),
        grid=(m // tm,),
        in_specs=[pl.BlockSpec((tm, t.shape[1]), lambda i: (i, 0)) for t in acts]
        + [_const_spec(t.shape) for t in consts],
        out_specs=pl.BlockSpec((tm, dm), lambda i: (i, 0)),
        out_shape=jax.ShapeDtypeStruct((m, dm), F32),
        compiler_params=_params("parallel"),
        name="merge",
    )(*acts, *consts)


def _moe_body(h_ref, rw, rb, wgu, wd, sgu, sd, lng, lnb, y_ref, xb_s, gate_s, acc_s, *, alpha):
    e = pl.program_id(1)
    lane = lax.broadcasted_iota(jnp.int32, gate_s.shape, 1)

    @pl.when(e == 0)
    def _():
        xb = h_ref[...].astype(BF16)
        xb_s[...] = xb
        scores = jax.nn.sigmoid(_dot(xb, rw[...]))
        ranked = jnp.where(lane < N_EXPERTS, scores + rb[...], -jnp.inf)
        picked = _top_mask(ranked, lane, TOP_K) * scores
        gate_s[...] = picked / jnp.sum(picked, axis=-1, keepdims=True) * ROUTED_SCALE
        hs = _dot(xb, sgu[...])
        ff = sd.shape[0]
        acc_s[...] = _dot((_silu(hs[:, :ff]) * hs[:, ff:]).astype(BF16), sd[...])

    hg = _dot(xb_s[...], wgu[0])
    gcol = jnp.sum(jnp.where(lane == e, gate_s[...], 0.0), axis=-1, keepdims=True)
    hm = _silu(hg[:, :EXPERT_FF]) * hg[:, EXPERT_FF:] * gcol
    acc_s[...] += _dot(hm.astype(BF16), wd[0])

    @pl.when(e == pl.num_programs(1) - 1)
    def _():
        y_ref[...] = _layer_norm(alpha * h_ref[...] + acc_s[...], lng[...], lnb[...])


def _moe(h, w, alpha, tm):
    m, dm = h.shape
    consts_a = (w["router_w"], w["router_b"])
    consts_b = (w["sh_gu"], w["sh_d"], w["ln2_g"], w["ln2_b"])
    return pl.pallas_call(
        functools.partial(_moe_body, alpha=alpha),
        grid=(m // tm, N_EXPERTS),
        in_specs=[pl.BlockSpec((tm, dm), lambda i, e: (i, 0))]
        + [_const_spec(t.shape) for t in consts_a]
        + [pl.BlockSpec((1, dm, 2 * EXPERT_FF), lambda i, e: (e, 0, 0)),
           pl.BlockSpec((1, EXPERT_FF, dm), lambda i, e: (e, 0, 0))]
        + [_const_spec(t.shape) for t in consts_b],
        out_specs=pl.BlockSpec((tm, dm), lambda i, e: (i, 0)),
        out_shape=jax.ShapeDtypeStruct((m, dm), F32),
        scratch_shapes=[pltpu.VMEM((tm, dm), BF16), pltpu.VMEM((tm, LANES), F32), pltpu.VMEM((tm, dm), F32)],
        compiler_params=_params("parallel", "arbitrary"),
        name="moe",
    )(h, *consts_a, w["exp_gu"], w["exp_d"], *consts_b)


def _layer_weights(l, p, n_chunks):
    w_in = p["w_in"][l]
    o = 0
    cols = {}
    for name, size in (("dn", 4 * DN_WIDTH), ("ba", 2 * DN_HEADS), ("s5", S5_WIDTH), ("mq", MB_WIDTH),
                       ("kv", 2 * MB_WIDTH), ("g", 3 * w_in.shape[0])):
        cols[name] = w_in[:, o:o + size]
        o += size
    w = {"w_" + k: v.astype(BF16) for k, v in cols.items()}
    w["w_ba"] = jnp.pad(w["w_ba"], ((0, 0), (0, LANES - 2 * DN_HEADS)))
    row = lambda v: v.reshape(1, -1)
    w["conv_w"] = p["dn_conv_w"][l]
    w["dn_ab"] = _dn_gate_rows(p["dn_a_log"][l], p["dn_dt_bias"][l])
    w["dn_norm_w"] = row(p["dn_norm_w"][l])
    s5 = tuple(p[k][l] for k in ("s5_a_re", "s5_a_im", "s5_log_dt", "s5_b_re", "s5_b_im", "s5_c_re", "s5_c_im"))
    w["s5_prompt"] = _s5_prompt_weights(*s5, n_chunks)
    w["s5_step"] = _s5_step_weights(*s5)
    w["s5_d"] = row(p["s5_d"][l])
    w["glu_w"] = p["s5_glu_w"][l].astype(BF16)
    w["glu_b"] = row(p["s5_glu_b"][l])
    w["wa"] = p["w_branch_a"][l].astype(BF16)
    w["wb"] = p["w_branch_b"][l].astype(BF16)
    w["wc"] = p["w_branch_c"][l].astype(BF16)
    w["wo"] = p["w_out"][l].astype(BF16)
    w["ln1_g"], w["ln1_b"] = row(p["ln1_g"][l]), row(p["ln1_b"][l])
    w["ln2_g"], w["ln2_b"] = row(p["ln2_g"][l]), row(p["ln2_b"][l])
    w["router_w"] = jnp.pad(p["router_w"][l].astype(BF16), ((0, 0), (0, LANES - N_EXPERTS)))
    w["router_b"] = jnp.pad(row(p["router_bias"][l]), ((0, 0), (0, LANES - N_EXPERTS)))
    w["exp_gu"] = jnp.concatenate([p["exp_w_gate"][l], p["exp_w_up"][l]], axis=-1).astype(BF16)
    w["exp_d"] = p["exp_w_down"][l].astype(BF16)
    w["sh_gu"] = jnp.concatenate([p["sh_w_gate"][l], p["sh_w_up"][l]], axis=-1).astype(BF16)
    w["sh_d"] = p["sh_w_down"][l].astype(BF16)
    return w


def _prompt_layer(x, w, alpha, tm, tm_moe):
    n, l, dm = x.shape
    x2 = x.reshape(n * l, dm)
    pdn, pba, ps5, pmq, pkv, pg = _proj(x2, w, tm)
    a_out, s_fin = _deltanet_prompt(pdn.reshape(n, l, -1), pba.reshape(n, l, -1), w["conv_w"], w["dn_ab"],
                                    w["dn_norm_w"])
    y5, s5_re, s5_im = _s5_prompt(ps5.reshape(n, l, -1), w["s5_prompt"])
    c_out = _moba_prompt(pmq.reshape(n, l, -1), pkv.reshape(n, l, -1))
    h = _merge(x2, a_out.reshape(n * l, -1), y5.reshape(n * l, -1), ps5, c_out.reshape(n * l, -1), pg, w, alpha, tm)
    y = _moe(h, w, alpha, tm_moe)
    kv = pkv.reshape(n, l, 2, MB_HEADS, MB_HD)
    conv_new = pdn.reshape(n, l, -1)[:, l - (DN_CONV - 1):, :3 * DN_WIDTH].astype(F32)
    return y.reshape(n, l, dm), (kv[:, :, 0], kv[:, :, 1], conv_new, s_fin, s5_re, s5_im)


def _sample_layer(x, w, alpha, conv_buf, s_dn, s5_re, s5_im, k_pool, v_pool, page_table, layer):
    n, l, dm = x.shape
    assert l == 1
    x2 = x.reshape(n, dm)
    pdn, pba, ps5, pmq, pkv, pg = _proj(x2, w, n)
    a_out, s_new = _deltanet_step(pdn.reshape(n, 1, -1), pba.reshape(n, 1, -1), conv_buf, s_dn, w["conv_w"],
                                  w["dn_ab"], w["dn_norm_w"])
    y5, s5_re_new, s5_im_new = _s5_step(ps5, s5_re, s5_im, w["s5_step"])
    sums = _block_key_sums(k_pool, page_table, layer)
    sel = _moba_pick(pmq.reshape(n, 1, -1), sums)
    sel = sel[:, :MB_HEADS, :MB_TOPK].reshape(-1)
    heads = lambda t: t.reshape(n, MB_HEADS, 1, MB_HD)
    c_out = _moba_step(heads(pmq), heads(pkv[:, :MB_WIDTH]), heads(pkv[:, MB_WIDTH:]), sel, page_table,
                       k_pool, v_pool, layer)
    h = _merge(x2, a_out.reshape(n, -1), y5, ps5, c_out.reshape(n, -1), pg, w, alpha, n)
    y = _moe(h, w, alpha, n)
    kv = pkv.reshape(n, 1, 2, MB_HEADS, MB_HD)
    conv_new = jnp.concatenate([conv_buf[:, 1:], pdn[:, None, :3 * DN_WIDTH].astype(F32)], axis=1)
    return y.reshape(n, 1, dm), (kv[:, :, 0], kv[:, :, 1], conv_new, s_new, s5_re_new, s5_im_new)


def kernel(x_prompt, x_sample, cache_k, cache_v, page_table, state_dn_conv, state_dn, state_s5_re, state_s5_im, w_in, dn_conv_w, dn_a_log, dn_dt_bias, dn_norm_w, s5_a_re, s5_a_im, s5_log_dt, s5_b_re, s5_b_im, s5_c_re, s5_c_im, s5_d, s5_glu_w, s5_glu_b, w_branch_a, w_branch_b, w_branch_c, w_out, ln1_g, ln1_b, router_w, router_bias, exp_w_gate, exp_w_up, exp_w_down, sh_w_gate, sh_w_up, sh_w_down, ln2_g, ln2_b):
    p = dict(w_in=w_in, dn_conv_w=dn_conv_w, dn_a_log=dn_a_log, dn_dt_bias=dn_dt_bias, dn_norm_w=dn_norm_w,
             s5_a_re=s5_a_re, s5_a_im=s5_a_im, s5_log_dt=s5_log_dt, s5_b_re=s5_b_re, s5_b_im=s5_b_im,
             s5_c_re=s5_c_re, s5_c_im=s5_c_im, s5_d=s5_d, s5_glu_w=s5_glu_w, s5_glu_b=s5_glu_b,
             w_branch_a=w_branch_a, w_branch_b=w_branch_b, w_branch_c=w_branch_c, w_out=w_out,
             ln1_g=ln1_g, ln1_b=ln1_b, router_w=router_w, router_bias=router_bias, exp_w_gate=exp_w_gate,
             exp_w_up=exp_w_up, exp_w_down=exp_w_down, sh_w_gate=sh_w_gate, sh_w_up=sh_w_up,
             sh_w_down=sh_w_down, ln2_g=ln2_g, ln2_b=ln2_b)
    depth = w_in.shape[0]
    alpha = (2 * depth) ** 0.25
    seq = x_prompt.shape[1]
    tm = min(512, x_prompt.shape[0] * seq)
    tm_moe = min(1024, x_prompt.shape[0] * seq)
    hp, hs = x_prompt, x_sample
    p_st, s_st = [], []
    for l in range(depth):
        w = _layer_weights(l, p, seq // S5_T)
        hp, st = _prompt_layer(hp, w, alpha, tm, tm_moe)
        p_st.append(st)
        hs, st = _sample_layer(hs, w, alpha, state_dn_conv[l], state_dn[l], state_s5_re[l], state_s5_im[l],
                               cache_k, cache_v, page_table, l)
        s_st.append(st)
    p_out = [jnp.stack(t) for t in zip(*p_st)]
    s_out = [jnp.stack(t) for t in zip(*s_st)]
    return (hp, hs, *p_out, *s_out)
```

```python
import functools
import math

import jax
import jax.numpy as jnp
import numpy as np
from jax import lax
from jax.experimental import pallas as pl
from jax.experimental.pallas import tpu as pltpu

F32 = jnp.float32
BF16 = jnp.bfloat16

DN_HEADS = 4
DN_DK = 128
DN_DV = 128
DN_WIDTH = DN_HEADS * DN_DK
DN_CONV = 4
DN_CHUNK = 64
S5_WIDTH = 512
S5_GROUP = 16
S5_GROUPS = S5_WIDTH // S5_GROUP
S5_STATE = 64
MB_HEADS = 4
MB_HD = 128
MB_WIDTH = MB_HEADS * MB_HD
MB_BLOCK = 256
MB_TOPK = 3
PAGE_SIZE = 128
N_EXPERTS = 64
TOP_K = 8
EXPERT_FF = 256
ROUTED_SCALE = 2.5
LN_EPS = 1e-5
NORM_EPS = 1e-6

LANES = 128
S5_T = 8
DN_TB = 256
VMEM_LIMIT = 56 * 1024 * 1024
NEG = -1e30


def _params(*sem):
    return pltpu.CompilerParams(dimension_semantics=sem, vmem_limit_bytes=VMEM_LIMIT)


def _const_spec(shape):
    nd = len(shape)
    return pl.BlockSpec(shape, lambda *_: (0,) * nd)


def _dot(a, b):
    return jnp.dot(a, b, preferred_element_type=F32)


def _dot_nt(a, b, precision=None):
    return lax.dot_general(a, b, (((1,), (1,)), ((), ())), preferred_element_type=F32, precision=precision)


def _dot_tn(a, b):
    return lax.dot_general(a, b, (((0,), (0,)), ((), ())), preferred_element_type=F32)


def _split(a):
    hi = a.astype(BF16)
    return hi, (a - hi.astype(F32)).astype(BF16)


def _split3(a):
    hi = a.astype(BF16)
    r = a - hi.astype(F32)
    mid = r.astype(BF16)
    return hi, mid, (r - mid.astype(F32)).astype(BF16)


def _mm3(a, b):
    a_hi, a_lo = _split(a)
    b_hi, b_lo = _split(b)
    return _dot(jnp.concatenate([a_hi, a_hi, a_lo], axis=1), jnp.concatenate([b_hi, b_lo, b_hi], axis=0))


def _silu(x):
    return x * jax.nn.sigmoid(x)


def _layer_norm(r, g, b):
    mu = jnp.mean(r, axis=-1, keepdims=True)
    c = r - mu
    var = jnp.mean(c * c, axis=-1, keepdims=True)
    return c * lax.rsqrt(var + LN_EPS) * g + b


def _proj_body(x_ref, wdn, wba, ws5, wmq, wk, wv, wg, odn, oba, os5, omq, ok, ov, og, *, q_scale):
    xb = x_ref[...].astype(BF16)

    def mm(w_ref, o_ref, scale=None):
        n = w_ref.shape[1]
        step = min(n, 512)
        for c in range(0, n, step):
            r = _dot(xb, w_ref[:, c:c + step])
            if scale is not None:
                r = r * scale
            o_ref[:, c:c + step] = r.astype(o_ref.dtype)

    mm(wdn, odn)
    mm(wba, oba)
    mm(ws5, os5)
    mm(wmq, omq, q_scale)
    mm(wk, ok)
    mm(wv, ov)
    mm(wg, og)


def _proj(x, w, tm, dn_dtype=BF16):
    m, d = x.shape
    outs = (("dn", dn_dtype), ("ba", F32), ("s5", F32), ("mq", BF16), ("k", F32), ("v", F32), ("g", BF16))
    ws = [w["w_" + k] for k, _ in outs]
    return pl.pallas_call(
        functools.partial(_proj_body, q_scale=MB_HD ** -0.5),
        grid=(m // tm,),
        in_specs=[pl.BlockSpec((tm, d), lambda i: (i, 0))] + [_const_spec(wi.shape) for wi in ws],
        out_specs=[pl.BlockSpec((tm, wi.shape[1]), lambda i: (i, 0)) for wi in ws],
        out_shape=[jax.ShapeDtypeStruct((m, wi.shape[1]), dt) for wi, (_, dt) in zip(ws, outs)],
        compiler_params=_params("parallel"),
        name="proj",
    )(x, *ws)


def _dn_body(pdn_ref, pba_ref, cw_ref, ab_ref, nw_ref, o_ref, s_out_ref, buf, s_scr):
    tb = DN_TB
    t = pl.program_id(1)

    @pl.when(t == 0)
    def _():
        buf[0:8, :] = jnp.zeros((8, buf.shape[1]), F32)
        s_scr[...] = jnp.zeros(s_scr.shape, F32)

    buf[8:8 + tb, :] = pdn_ref[0, :, 0:3 * DN_WIDTH].astype(F32)

    def conv(c0):
        cs = slice(c0, c0 + DN_DK)
        y = cw_ref[3:4, cs] * buf[8:8 + tb, cs]
        for j in range(1, DN_CONV):
            y = y + cw_ref[3 - j:4 - j, cs] * buf[8 - j:8 - j + tb, cs]
        return _silu(y)

    def l2norm(x):
        return x * lax.rsqrt(jnp.sum(x * x, axis=-1, keepdims=True) + NORM_EPS)

    pba = pba_ref[0]
    beta_all = jax.nn.sigmoid(pba)
    g_all = -jnp.exp(ab_ref[0:1, :]) * jax.nn.softplus(pba + ab_ref[1:2, :])

    row = lax.broadcasted_iota(jnp.int32, (tb, tb), 0)
    col = lax.broadcasted_iota(jnp.int32, (tb, tb), 1)
    same64 = (row >> 6) == (col >> 6)
    same32 = (row >> 5) == (col >> 5)
    same16 = (row >> 4) == (col >> 4)
    incl = jnp.logical_and(same64, row >= col)
    strict = jnp.logical_and(same64, row > col)
    g_hi, g_mid, g_lo = _split3(g_all)
    ones_incl = jnp.where(incl, 1.0, 0.0).astype(BF16)
    gc = _dot(jnp.concatenate([ones_incl] * 3, axis=1), jnp.concatenate([g_hi, g_mid, g_lo], axis=0))
    gl = jnp.concatenate(
        [jnp.broadcast_to(gc[(c + 1) * DN_CHUNK - 1:(c + 1) * DN_CHUNK, :], (DN_CHUNK, LANES))
         for c in range(tb // DN_CHUNK)], axis=0)
    gct = gc.T
    eye = jnp.where(row == col, 1.0, 0.0)

    heads = range(DN_HEADS)
    q = [l2norm(conv(h * DN_DK)) * DN_DK ** -0.5 for h in heads]
    k = [l2norm(conv(DN_WIDTH + h * DN_DK)) for h in heads]
    v = [conv(2 * DN_WIDTH + h * DN_DK) for h in heads]
    beta = [beta_all[:, h:h + 1] for h in heads]
    gcol = [gc[:, DN_HEADS + h:DN_HEADS + h + 1] for h in heads]
    glast = [gl[:, DN_HEADS + h:DN_HEADS + h + 1] for h in heads]
    dec = [jnp.exp(jnp.where(incl, gcol[h] - gct[DN_HEADS + h:DN_HEADS + h + 1, :], NEG)) for h in heads]
    kb = [k[h] * beta[h] for h in heads]
    k16 = [k[h].astype(BF16) for h in heads]
    a = [jnp.where(strict, _dot_nt(kb[h].astype(BF16), k16[h]) * dec[h], 0.0) for h in heads]
    qk = [_dot_nt(q[h].astype(BF16), k16[h]) * dec[h] for h in heads]

    n1 = [jnp.where(same16, -a[h], 0.0) for h in heads]
    n2 = [_mm3(n1[h], n1[h]) for h in heads]
    n4 = [_mm3(n2[h], n2[h]) for h in heads]
    n8 = [_mm3(n4[h], n4[h]) for h in heads]
    p = [eye + n1[h] for h in heads]
    p = [p[h] + _mm3(p[h], n2[h]) for h in heads]
    p = [p[h] + _mm3(p[h], n4[h]) for h in heads]
    p = [p[h] + _mm3(p[h], n8[h]) for h in heads]
    off16 = jnp.logical_and(same32, jnp.logical_not(same16))
    t1 = [_mm3(jnp.where(off16, a[h], 0.0), p[h]) for h in heads]
    p = [p[h] - _mm3(p[h], t1[h]) for h in heads]
    t2 = [_mm3(jnp.where(same32, 0.0, a[h]), p[h]) for h in heads]
    p = [p[h] - _mm3(p[h], t2[h]) for h in heads]

    eg = [jnp.exp(gcol[h]) for h in heads]
    sol = [_mm3(p[h], jnp.concatenate([v[h] * beta[h], kb[h] * eg[h]], axis=1)) for h in heads]
    qd = [q[h] * eg[h] for h in heads]
    kd = [k[h] * jnp.exp(glast[h] - gcol[h]) for h in heads]

    sh = [s_scr[h] for h in heads]
    for c in range(tb // DN_CHUNK):
        r = slice(c * DN_CHUNK, (c + 1) * DN_CHUNK)
        for h in heads:
            wq = jnp.concatenate([sol[h][r, DN_DV:], qd[h][r]], axis=0).astype(BF16)
            ws_ = _dot(wq, sh[h].astype(BF16))
            vn16 = (sol[h][r, :DN_DV] - ws_[:DN_CHUNK]).astype(BF16)
            o = ws_[DN_CHUNK:] + _dot(qk[h][r, r].astype(BF16), vn16)
            sh[h] = (sh[h] * jnp.exp(glast[h][c * DN_CHUNK:c * DN_CHUNK + 1, :])
                     + _dot_tn(kd[h][r].astype(BF16), vn16))
            z = pdn_ref[0, r, 3 * DN_WIDTH + h * DN_DV:3 * DN_WIDTH + (h + 1) * DN_DV].astype(F32)
            o = o * lax.rsqrt(jnp.mean(o * o, axis=-1, keepdims=True) + NORM_EPS) * nw_ref[...] * _silu(z)
            o_ref[0, r, h * DN_DV:(h + 1) * DN_DV] = o.astype(o_ref.dtype)
    for h in heads:
        s_scr[h] = sh[h]

    buf[0:8, :] = buf[tb:tb + 8, :]

    @pl.when(t == pl.num_programs(1) - 1)
    def _():
        s_out_ref[0] = s_scr[...]


def _dn_gate_rows(a_log, dt_bias):
    ab = jnp.zeros((2, LANES), F32)
    ab = ab.at[0, DN_HEADS:2 * DN_HEADS].set(a_log)
    return ab.at[1, DN_HEADS:2 * DN_HEADS].set(dt_bias)


def _deltanet_prompt(pdn, pba, conv_w, ab, norm_w):
    n, l, _ = pdn.shape
    assert l % DN_TB == 0
    return pl.pallas_call(
        _dn_body,
        grid=(n, l // DN_TB),
        in_specs=[
            pl.BlockSpec((1, DN_TB, 4 * DN_WIDTH), lambda i, t: (i, t, 0)),
            pl.BlockSpec((1, DN_TB, LANES), lambda i, t: (i, t, 0)),
            _const_spec(conv_w.shape), _const_spec(ab.shape), _const_spec(norm_w.shape),
        ],
        out_specs=[
            pl.BlockSpec((1, DN_TB, DN_WIDTH), lambda i, t: (i, t, 0)),
            pl.BlockSpec((1, DN_HEADS, DN_DK, DN_DV), lambda i, t: (i, 0, 0, 0)),
        ],
        out_shape=[
            jax.ShapeDtypeStruct((n, l, DN_WIDTH), BF16),
            jax.ShapeDtypeStruct((n, DN_HEADS, DN_DK, DN_DV), F32),
        ],
        scratch_shapes=[pltpu.VMEM((DN_TB + 8, 3 * DN_WIDTH), F32), pltpu.VMEM((DN_HEADS, DN_DK, DN_DV), F32)],
        compiler_params=_params("parallel", "arbitrary"),
        name="deltanet_prompt",
    )(pdn, pba, conv_w, ab, norm_w)


def _dn_step_body(pdn_ref, pba_ref, cbuf_ref, s0_ref, cw_ref, ab_ref, nw_ref, o_ref, s_out_ref):
    x = pdn_ref[0, :, 0:3 * DN_WIDTH].astype(F32)
    y = cw_ref[3:4, :] * x
    for j in range(1, DN_CONV):
        y = y + cw_ref[3 - j:4 - j, :] * cbuf_ref[0, 3 - j:4 - j, :]
    qkv = _silu(y)
    pba = pba_ref[0]
    beta_all = jax.nn.sigmoid(pba)
    g_all = -jnp.exp(ab_ref[0:1, :]) * jax.nn.softplus(pba + ab_ref[1:2, :])

    def l2norm(t):
        return t * lax.rsqrt(jnp.sum(t * t, axis=-1, keepdims=True) + NORM_EPS)

    sub = lax.broadcasted_iota(jnp.int32, (8, DN_DK), 0)
    for h in range(DN_HEADS):
        q = l2norm(qkv[:, h * DN_DK:(h + 1) * DN_DK]) * DN_DK ** -0.5
        k = l2norm(qkv[:, DN_WIDTH + h * DN_DK:DN_WIDTH + (h + 1) * DN_DK])
        v = qkv[:, 2 * DN_WIDTH + h * DN_DV:2 * DN_WIDTH + (h + 1) * DN_DV]
        beta = beta_all[:, h:h + 1]
        eg = jnp.exp(g_all[:, DN_HEADS + h:DN_HEADS + h + 1])
        s0 = s0_ref[0, h]
        w = k * beta * eg
        lhs = jnp.where(sub == 0, w, jnp.where(sub == 1, q * eg, 0.0))
        ws_ = jnp.dot(lhs, s0, precision=lax.Precision.HIGHEST, preferred_element_type=F32)
        vn = v * beta - ws_[0:1, :]
        o = ws_[1:2, :] + jnp.sum(q * k, axis=-1, keepdims=True) * vn
        kt = jnp.broadcast_to(k, (DN_DK, DN_DK)).T
        s_out_ref[0, h] = s0 * eg + kt * vn
        z = pdn_ref[0, :, 3 * DN_WIDTH + h * DN_DV:3 * DN_WIDTH + (h + 1) * DN_DV].astype(F32)
        o = o * lax.rsqrt(jnp.mean(o * o, axis=-1, keepdims=True) + NORM_EPS) * nw_ref[...] * _silu(z)
        o_ref[0, :, h * DN_DV:(h + 1) * DN_DV] = o.astype(o_ref.dtype)


def _deltanet_step(pdn, pba, conv_buf, s0, conv_w, ab, norm_w):
    n = pdn.shape[0]
    return pl.pallas_call(
        _dn_step_body,
        grid=(n,),
        in_specs=[
            pl.BlockSpec((1, 1, 4 * DN_WIDTH), lambda i: (i, 0, 0)),
            pl.BlockSpec((1, 1, LANES), lambda i: (i, 0, 0)),
            pl.BlockSpec((1, DN_CONV - 1, 3 * DN_WIDTH), lambda i: (i, 0, 0)),
            pl.BlockSpec((1, DN_HEADS, DN_DK, DN_DV), lambda i: (i, 0, 0, 0)),
            _const_spec(conv_w.shape), _const_spec(ab.shape), _const_spec(norm_w.shape),
        ],
        out_specs=[
            pl.BlockSpec((1, 1, DN_WIDTH), lambda i: (i, 0, 0)),
            pl.BlockSpec((1, DN_HEADS, DN_DK, DN_DV), lambda i: (i, 0, 0, 0)),
        ],
        out_shape=[
            jax.ShapeDtypeStruct((n, 1, DN_WIDTH), BF16),
            jax.ShapeDtypeStruct((n, DN_HEADS, DN_DK, DN_DV), F32),
        ],
        compiler_params=_params("parallel"),
        name="deltanet_step",
    )(pdn, pba, conv_buf, s0, conv_w, ab, norm_w)


def _s5_discretize(a_re, a_im, log_dt, b_re, b_im):
    dt = jnp.exp(log_dt)[:, None]
    mag = jnp.exp(a_re * dt)
    abr, abi = mag * jnp.cos(a_im * dt), mag * jnp.sin(a_im * dt)
    den = a_re * a_re + a_im * a_im
    fr = ((abr - 1.0) * a_re + abi * a_im) / den
    fi = (abi * a_re - (abr - 1.0) * a_im) / den
    bbr = fr[..., None] * b_re - fi[..., None] * b_im
    bbi = fr[..., None] * b_im + fi[..., None] * b_re
    return abr, abi, bbr, bbi


def _s5_powers(a_re, a_im, log_dt, taus):
    dt = jnp.exp(log_dt)[None, :, None]
    tau = jnp.asarray(taus, F32)[:, None, None]
    mag = jnp.exp(a_re[None] * dt * tau)
    ang = a_im[None] * dt * tau
    return mag * jnp.cos(ang), mag * jnp.sin(ang)


S5_GB = LANES // S5_GROUP
S5_NGB = S5_GROUPS // S5_GB
S5_K = S5_T * LANES
S5_HALF = S5_GB * S5_STATE
S5_ROWS = 256


def _s5_prompt_weights(a_re, a_im, log_dt, b_re, b_im, c_re, c_im):
    t = S5_T
    hp = lax.Precision.HIGHEST
    eye = jnp.eye(S5_GB, dtype=F32)
    _, _, bbr, bbi = _s5_discretize(a_re, a_im, log_dt, b_re, b_im)
    lr, li = _s5_powers(a_re, a_im, log_dt, list(range(t + 1)))
    lbr = lr[..., None] * bbr[None] - li[..., None] * bbi[None]
    lbi = lr[..., None] * bbi[None] + li[..., None] * bbr[None]
    kt = (jnp.einsum("ghp,tgpk->tghk", c_re, lbr, precision=hp)
          - jnp.einsum("ghp,tgpk->tghk", c_im, lbi, precision=hp))
    lag = np.arange(t)[None, :] - np.arange(t)[:, None]
    toep = jnp.where((lag >= 0)[:, :, None, None, None], kt[np.clip(lag, 0, t)], 0.0)
    toep = toep.transpose(2, 0, 4, 1, 3).reshape(S5_NGB, S5_GB, t, S5_GROUP, t, S5_GROUP)
    toep = jnp.einsum("Baipjq,am->Biapjmq", toep, eye).reshape(S5_NGB, S5_K, S5_K)
    sel = t - 1 - np.arange(t)
    wz = jnp.stack([lbr[sel], lbi[sel]])
    wz = wz.transpose(2, 1, 4, 0, 3).reshape(S5_NGB, S5_GB, t, S5_GROUP, 2, S5_STATE)
    wz = jnp.einsum("Baipcq,am->Biapcmq", wz, eye).reshape(S5_NGB, S5_K, 2 * S5_HALF)
    lr1, li1 = lr[1:], li[1:]
    wy = jnp.stack([c_re[None] * lr1[:, :, None, :] - c_im[None] * li1[:, :, None, :],
                    -(c_re[None] * li1[:, :, None, :] + c_im[None] * lr1[:, :, None, :])])
    wy = wy.transpose(2, 0, 4, 1, 3).reshape(S5_NGB, S5_GB, 2, S5_STATE, t, S5_GROUP)
    wy = jnp.einsum("Bacpjq,am->Bcapjmq", wy, eye).reshape(S5_NGB, 2 * S5_HALF, S5_K)
    n_steps = int(math.log2(S5_ROWS))
    sr, si = _s5_powers(a_re, a_im, log_dt, [t * (1 << i) for i in range(n_steps)])
    blocks = lambda v: v.reshape(n_steps, S5_NGB, S5_HALF).transpose(1, 0, 2)
    sc_r = jnp.concatenate([blocks(sr), blocks(sr)], axis=-1)
    sc_i = jnp.concatenate([-blocks(si), blocks(si)], axis=-1)
    return toep.astype(BF16), wz.astype(BF16), wy.astype(BF16), sc_r, sc_i


def _s5_body(u_ref, toep_ref, wz_ref, wy_ref, lr_ref, li_ref, y_ref, sfin_ref, carry_s):
    rb, b = pl.program_id(1), pl.program_id(2)
    lanes = [pl.ds(pl.multiple_of(t * S5_WIDTH + b * LANES, LANES), LANES) for t in range(S5_T)]
    u = jnp.concatenate([u_ref[:, lanes[t]] for t in range(S5_T)], axis=1).astype(BF16)

    @pl.when(rb == 0)
    def _():
        carry_s[b] = jnp.zeros((1, 2 * S5_HALF), F32)

    swap = lambda v: pltpu.roll(v, S5_HALF, 1)
    z = _dot(u, wz_ref[...])
    rowi = lax.broadcasted_iota(jnp.int32, z.shape, 0)
    carry = carry_s[b]
    s = z + jnp.where(rowi == 0, lr_ref[0:1, :] * carry + li_ref[0:1, :] * swap(carry), 0.0)
    for i in range(lr_ref.shape[0]):
        d = 1 << i
        sh = jnp.where(rowi >= d, pltpu.roll(s, d, 0), 0.0)
        s = s + lr_ref[i:i + 1, :] * sh + li_ref[i:i + 1, :] * swap(sh)
    last = s[z.shape[0] - 1:z.shape[0], :]
    sprev = jnp.where(rowi >= 1, pltpu.roll(s, 1, 0), carry)
    carry_s[b] = last
    sfin_ref[b] = last
    y = _dot(u, toep_ref[...]) + _dot(sprev.astype(BF16), wy_ref[...])
    for t in range(S5_T):
        y_ref[:, lanes[t]] = y[:, t * LANES:(t + 1) * LANES]


def _s5_prompt(u, wts):
    toep, wz, wy, sc_r, sc_i = wts
    n, l, _ = u.shape
    nc = l // S5_T
    assert nc % S5_ROWS == 0
    width = S5_T * S5_WIDTH
    wspec = lambda shp: pl.BlockSpec((None,) + shp[1:], lambda i, r, b: (b, 0, 0))
    y, sfin = pl.pallas_call(
        _s5_body,
        grid=(n, nc // S5_ROWS, S5_NGB),
        in_specs=[pl.BlockSpec((None, S5_ROWS, width), lambda i, r, b: (i, r, 0)),
                  wspec(toep.shape), wspec(wz.shape), wspec(wy.shape), wspec(sc_r.shape), wspec(sc_i.shape)],
        out_specs=[pl.BlockSpec((None, S5_ROWS, width), lambda i, r, b: (i, r, 0)),
                   pl.BlockSpec((None, S5_NGB, 1, 2 * S5_HALF), lambda i, r, b: (i, 0, 0, 0))],
        out_shape=[jax.ShapeDtypeStruct((n, nc, width), F32),
                   jax.ShapeDtypeStruct((n, S5_NGB, 1, 2 * S5_HALF), F32)],
        scratch_shapes=[pltpu.VMEM((S5_NGB, 1, 2 * S5_HALF), F32)],
        compiler_params=_params("parallel", "arbitrary", "arbitrary"),
        name="s5_prompt",
    )(u.reshape(n, nc, width), toep, wz, wy, sc_r, sc_i)
    sfin = sfin.reshape(n, S5_NGB, 2, S5_GB, S5_STATE).transpose(2, 0, 1, 3, 4).reshape(2, n, S5_GROUPS, S5_STATE)
    return y.reshape(n, l, S5_WIDTH), sfin[0], sfin[1]


def _s5_step_weights(a_re, a_im, log_dt, b_re, b_im, c_re, c_im):
    abr, abi, bbr, bbi = _s5_discretize(a_re, a_im, log_dt, b_re, b_im)
    eye = jnp.eye(S5_GROUPS, dtype=F32)
    gp = S5_GROUPS * S5_STATE
    wb = lambda b: jnp.einsum("gph,gk->ghkp", b, eye).reshape(S5_WIDTH, gp).astype(BF16)
    wc = lambda c: jnp.einsum("ghp,gk->gpkh", c, eye).reshape(gp, S5_WIDTH).astype(BF16)
    return (abr.reshape(1, gp), abi.reshape(1, gp), wb(bbr), wb(bbi), wc(c_re), wc(c_im))


def _s5_step_body(u_ref, s0r_ref, s0i_ref, abr_ref, abi_ref, wbr, wbi, wcr, wci, y_ref, sr_ref, si_ref):
    u = u_ref[...].astype(BF16)
    abr, abi = abr_ref[...], abi_ref[...]
    s0r, s0i = s0r_ref[...], s0i_ref[...]
    sr = abr * s0r - abi * s0i + _dot(u, wbr[...])
    si = abr * s0i + abi * s0r + _dot(u, wbi[...])
    sr_ref[...] = sr
    si_ref[...] = si
    y_ref[...] = _dot(sr.astype(BF16), wcr[...]) - _dot(si.astype(BF16), wci[...])


def _s5_step(u, s0_re, s0_im, wts):
    n = u.shape[0]
    gp = S5_GROUPS * S5_STATE
    args = (u, s0_re.reshape(n, gp), s0_im.reshape(n, gp)) + tuple(wts)
    y, sr, si = pl.pallas_call(
        _s5_step_body,
        grid=(1,),
        in_specs=[_const_spec(a.shape) for a in args],
        out_specs=[_const_spec((n, S5_WIDTH)), _const_spec((n, gp)), _const_spec((n, gp))],
        out_shape=[jax.ShapeDtypeStruct((n, S5_WIDTH), F32), jax.ShapeDtypeStruct((n, gp), F32),
                   jax.ShapeDtypeStruct((n, gp), F32)],
        compiler_params=_params("arbitrary"),
        name="s5_step",
    )(*args)
    return y, sr.reshape(n, S5_GROUPS, S5_STATE), si.reshape(n, S5_GROUPS, S5_STATE)


def _top_mask(gate, pos, count, axis=-1):
    sel = jnp.zeros(gate.shape, F32)
    for _ in range(count):
        m = jnp.max(gate, axis=axis, keepdims=True)
        idx = jnp.min(jnp.where(gate == m, pos, 2 ** 30), axis=axis, keepdims=True)
        hit = jnp.logical_and(pos == idx, m > -jnp.inf)
        sel = jnp.where(hit, 1.0, sel)
        gate = jnp.where(pos == idx, -jnp.inf, gate)
    return sel


def _alibi_slopes():
    s = np.exp2(-8.0 * np.arange(1, MB_HEADS + 1, dtype=np.float32) / MB_HEADS).astype(np.float32)
    return jnp.asarray(np.broadcast_to(s[:, None, None], (MB_HEADS, 1, LANES)))


MB_SLOTS = 16
MB_V_ROWS = MB_HD + 16
MB_GROUP = 2
MB_HEADS_PER_STEP = 4


def _moba_body(q_ref, k_ref, v_ref, slope_ref, o_ref, kaug_s, vaug_s, means_s, *, nb):
    blk = MB_BLOCK
    qi = pl.program_id(2)
    heads = range(MB_HEADS_PER_STEP)
    cols = [slice(h * MB_HD, (h + 1) * MB_HD) for h in heads]
    slope = [slope_ref[h, :, 0:1] for h in heads]

    @pl.when(qi == 0)
    def _():
        lane = lax.broadcasted_iota(jnp.int32, (blk, MB_HD), 1)
        off = lax.broadcasted_iota(jnp.int32, (blk, MB_HD), 0).astype(F32)
        means_s[...] = jnp.zeros(means_s.shape, F32)
        vaug_s[:, :, MB_HD:, :] = jnp.ones((len(heads), nb, MB_V_ROWS - MB_HD, blk), BF16)
        for b in range(nb):
            for h in heads:
                kb = k_ref[0, b * blk:(b + 1) * blk, cols[h]]
                means_s[h, b:b + 1, :] = jnp.mean(kb, axis=0, keepdims=True)
                feat = jnp.where(lane == b, 1.0, 0.0)
                feat = jnp.where(lane == MB_SLOTS, slope[h] * off, feat)
                feat = jnp.where(lane == MB_SLOTS + 1, slope[h] * float(b * blk), feat)
                feat = jnp.where(lane == MB_SLOTS + 2, 1.0, feat)
                kaug_s[h, b] = jnp.concatenate([kb.astype(BF16), feat.astype(BF16)], axis=1)
                vaug_s[h, b, 0:MB_HD, :] = v_ref[0, b * blk:(b + 1) * blk, cols[h]].T.astype(BF16)

    slot = lax.broadcasted_iota(jnp.int32, (MB_SLOTS, blk), 0)
    pad = jnp.zeros((MB_HD - 2 * MB_SLOTS, blk), BF16)
    qt = [q_ref[0, :, cols[h]].astype(F32).T for h in heads]
    gate = [jnp.dot(means_s[h], qt[h], precision=lax.Precision.HIGHEST, preferred_element_type=F32) for h in heads]
    sel = [_top_mask(jnp.where(slot < qi, gate[h], -jnp.inf), slot, MB_TOPK, axis=0) for h in heads]
    bias_rows = [jnp.where(slot < 2, 1.0, jnp.where(slot == 2, -slope[h] * (qi * blk).astype(F32), 0.0)).astype(BF16)
                 for h in heads]
    qt16 = [qt[h].astype(BF16) for h in heads]
    q_past = [jnp.concatenate([qt16[h], jnp.where(sel[h] > 0.0, 0.0, NEG).astype(BF16), bias_rows[h], pad], axis=0)
              for h in heads]
    q_own = [jnp.concatenate([qt16[h], jnp.zeros((MB_SLOTS, blk), BF16), bias_rows[h], pad], axis=0) for h in heads]

    key = lax.broadcasted_iota(jnp.int32, (blk, blk), 0)
    qry = lax.broadcasted_iota(jnp.int32, (blk, blk), 1)
    s = [jnp.where(key <= qry, _dot(kaug_s[h, qi], q_own[h]), NEG) for h in heads]
    m = [jnp.max(s[h], axis=0, keepdims=True) for h in heads]
    acc = [_dot(vaug_s[h, qi], jnp.exp(s[h] - m[h]).astype(BF16)) for h in heads]

    grp = MB_GROUP

    def body(i, carry):
        m_old, acc = carry
        s = [[_dot(kaug_s[h, grp * i + g], q_past[h]) for g in range(grp)] for h in heads]
        m_new = list(m_old)
        for g in range(grp):
            m_new = [jnp.maximum(m_new[h], jnp.max(s[h][g], axis=0, keepdims=True)) for h in heads]
        p = [[jnp.exp(s[h][g] - m_new[h]).astype(BF16) for g in range(grp)] for h in heads]
        acc = [jnp.exp(m_old[h] - m_new[h]) * acc[h] for h in heads]
        for g in range(grp):
            acc = [acc[h] + _dot(vaug_s[h, grp * i + g], p[h][g]) for h in heads]
        return tuple(m_new), tuple(acc)

    m, acc = lax.fori_loop(0, (qi + grp - 1) // grp, body, (tuple(m), tuple(acc)))
    for h in heads:
        o_ref[0, :, cols[h]] = (acc[h][0:MB_HD] / acc[h][MB_HD:MB_HD + 1]).T.astype(o_ref.dtype)


def _moba_prompt(q, k, v):
    n, l, _ = q.shape
    nb = l // MB_BLOCK
    hps = MB_HEADS_PER_STEP
    width = hps * MB_HD
    assert l % MB_BLOCK == 0 and nb <= MB_SLOTS and nb % MB_GROUP == 0 and MB_HEADS % hps == 0
    kv_spec = pl.BlockSpec((1, l, width), lambda i, h, t: (i, 0, h))
    return pl.pallas_call(
        functools.partial(_moba_body, nb=nb),
        grid=(n, MB_HEADS // hps, nb),
        in_specs=[
            pl.BlockSpec((1, MB_BLOCK, width), lambda i, h, t: (i, t, h)),
            kv_spec, kv_spec,
            pl.BlockSpec((hps, 1, LANES), lambda i, h, t: (h, 0, 0)),
        ],
        out_specs=pl.BlockSpec((1, MB_BLOCK, width), lambda i, h, t: (i, t, h)),
        out_shape=jax.ShapeDtypeStruct((n, l, MB_WIDTH), BF16),
        scratch_shapes=[pltpu.VMEM((hps, nb, MB_BLOCK, 2 * MB_HD), BF16),
                        pltpu.VMEM((hps, nb, MB_V_ROWS, MB_BLOCK), BF16),
                        pltpu.VMEM((hps, MB_SLOTS, MB_HD), F32)],
        compiler_params=_params("parallel", "parallel", "arbitrary"),
        name="moba_prompt",
    )(q, k, v, _alibi_slopes())


PAGES_PER_STEP = 32
PAGES_PER_BLOCK = MB_BLOCK // PAGE_SIZE


def _page_sum_body(pt_ref, *refs):
    del pt_ref
    o_ref = refs[-1]
    for b in range(PAGES_PER_STEP // PAGES_PER_BLOCK):
        acc = jnp.sum(refs[PAGES_PER_BLOCK * b][...], axis=0)
        for j in range(1, PAGES_PER_BLOCK):
            acc = acc + jnp.sum(refs[PAGES_PER_BLOCK * b + j][...], axis=0)
        o_ref[b] = acc


def _block_key_sums(pool, page_table, layer):
    n, n_pages = page_table.shape
    assert n_pages % PAGES_PER_STEP == 0
    tail = pool.shape[2:]

    def page_spec(j):
        return pl.BlockSpec((None, None) + tail, lambda i, s, pt: (layer, pt[i, s * PAGES_PER_STEP + j], 0, 0, 0))

    blocks_per_step = PAGES_PER_STEP // PAGES_PER_BLOCK
    return pl.pallas_call(
        _page_sum_body,
        grid_spec=pltpu.PrefetchScalarGridSpec(
            num_scalar_prefetch=1,
            grid=(n, n_pages // PAGES_PER_STEP),
            in_specs=[page_spec(j) for j in range(PAGES_PER_STEP)],
            out_specs=pl.BlockSpec((None, blocks_per_step) + tail[1:], lambda i, s, pt: (i, s, 0, 0)),
        ),
        out_shape=jax.ShapeDtypeStruct((n, n_pages // PAGES_PER_BLOCK) + tail[1:], F32),
        compiler_params=_params("parallel", "arbitrary"),
        name="moba_page_sums",
    )(page_table, *([pool] * PAGES_PER_STEP))


def _moba_pick_body(q_ref, sums_ref, sel_ref):
    nbk = sums_ref.shape[0]
    out = jnp.zeros(sel_ref.shape, jnp.int32)
    osub = lax.broadcasted_iota(jnp.int32, out.shape, 0)
    olane = lax.broadcasted_iota(jnp.int32, out.shape, 1)
    bidx = lax.broadcasted_iota(jnp.int32, (nbk, 1), 0)
    for h in range(MB_HEADS):
        means = sums_ref[:, h, :] / float(MB_BLOCK)
        q = q_ref[:, h * MB_HD:(h + 1) * MB_HD].astype(F32)
        gate = jnp.sum(means * q, axis=-1, keepdims=True)
        for rnk in range(MB_TOPK):
            m = jnp.max(gate, axis=0, keepdims=True)
            idx = jnp.min(jnp.where(gate == m, bidx, 2 ** 30), axis=0, keepdims=True)
            out = jnp.where(jnp.logical_and(osub == h, olane == rnk), idx, out)
            gate = jnp.where(bidx == idx, -jnp.inf, gate)
    sel_ref[...] = out


def _moba_pick(q, sums):
    n = q.shape[0]
    return pl.pallas_call(
        _moba_pick_body,
        grid=(n,),
        in_specs=[pl.BlockSpec((None, 1, MB_WIDTH), lambda i: (i, 0, 0)),
                  pl.BlockSpec((None,) + sums.shape[1:], lambda i: (i, 0, 0, 0))],
        out_specs=pl.BlockSpec((None, 8, LANES), lambda i: (i, 0, 0)),
        out_shape=jax.ShapeDtypeStruct((n, 8, LANES), jnp.int32),
        compiler_params=_params("parallel"),
        name="moba_pick",
    )(q, sums)


def _moba_step_body(sel_ref, pt_ref, q_ref, kn_ref, vn_ref, slope_ref, *refs, past):
    del pt_ref
    pages, o_ref, m_s, l_s, acc_s = refs[:4 * MB_HEADS], refs[4 * MB_HEADS], *refs[4 * MB_HEADS + 1:]
    i, s = pl.program_id(0), pl.program_id(1)
    sub = lax.broadcasted_iota(jnp.int32, (PAGE_SIZE, 1), 0)
    for h in range(MB_HEADS):
        q = q_ref[0, h].astype(F32)
        slope = slope_ref[h, :, 0:1]

        @pl.when(s == 0)
        def _():
            m_s[h] = jnp.sum(q * kn_ref[0, h], axis=-1, keepdims=True)
            l_s[h] = jnp.ones((1, 1), F32)
            acc_s[h] = vn_ref[0, h]

        blk = sel_ref[(i * MB_HEADS + h) * MB_TOPK + s]
        k0, k1, v0, v1 = pages[4 * h:4 * h + 4]
        m_old = m_s[h]
        sc = []
        for half, kr in enumerate((k0, k1)):
            kpos = blk * MB_BLOCK + half * PAGE_SIZE + sub
            sc.append(jnp.sum(kr[:, h, :] * q, axis=-1, keepdims=True) - slope * (past - kpos).astype(F32))
        m_new = jnp.maximum(m_old, jnp.maximum(jnp.max(sc[0], axis=0, keepdims=True),
                                               jnp.max(sc[1], axis=0, keepdims=True)))
        alpha = jnp.exp(m_old - m_new)
        p0 = jnp.exp(sc[0] - m_new)
        p1 = jnp.exp(sc[1] - m_new)
        l_new = alpha * l_s[h] + jnp.sum(p0, axis=0, keepdims=True) + jnp.sum(p1, axis=0, keepdims=True)
        acc_new = (alpha * acc_s[h] + jnp.sum(p0 * v0[:, h, :], axis=0, keepdims=True)
                   + jnp.sum(p1 * v1[:, h, :], axis=0, keepdims=True))
        m_s[h] = m_new
        l_s[h] = l_new
        acc_s[h] = acc_new

        @pl.when(s == MB_TOPK - 1)
        def _():
            o_ref[0, h] = (acc_new / l_new).astype(o_ref.dtype)


def _moba_step(q, k_new, v_new, sel, page_table, k_pool, v_pool, layer):
    n = q.shape[0]
    past = page_table.shape[1] * PAGE_SIZE
    assert PAGES_PER_BLOCK == 2 and past // MB_BLOCK >= MB_TOPK
    tail = k_pool.shape[2:]

    def page_spec(h, half):
        def imap(i, s, sel_r, pt_r):
            blk = sel_r[(i * MB_HEADS + h) * MB_TOPK + s]
            return (layer, pt_r[i, blk * PAGES_PER_BLOCK + half], 0, 0, 0)
        return pl.BlockSpec((None, None) + tail, imap)

    page_specs, page_args = [], []
    for h in range(MB_HEADS):
        for pool in (k_pool, v_pool):
            for half in range(PAGES_PER_BLOCK):
                page_specs.append(page_spec(h, half))
                page_args.append(pool)
    tok = pl.BlockSpec((1, MB_HEADS, 1, MB_HD), lambda i, s, *_: (i, 0, 0, 0))
    return pl.pallas_call(
        functools.partial(_moba_step_body, past=past),
        grid_spec=pltpu.PrefetchScalarGridSpec(
            num_scalar_prefetch=2,
            grid=(n, MB_TOPK),
            in_specs=[tok, tok, tok, _const_spec((MB_HEADS, 1, LANES))] + page_specs,
            out_specs=tok,
            scratch_shapes=[pltpu.VMEM((MB_HEADS, 1, 1), F32), pltpu.VMEM((MB_HEADS, 1, 1), F32),
                            pltpu.VMEM((MB_HEADS, 1, MB_HD), F32)],
        ),
        out_shape=jax.ShapeDtypeStruct((n, MB_HEADS, 1, MB_HD), BF16),
        compiler_params=_params("parallel", "arbitrary"),
        name="moba_step",
    )(sel, page_table, q, k_new, v_new, _alibi_slopes(), *page_args)


def _merge_body(x_ref, a_ref, y5_ref, u_ref, c_ref, g_ref, d_ref, gluw, glub, wa, wb, wc, wo, lng, lnb, h_ref,
                *, alpha):
    y = jax.nn.gelu(y5_ref[...] + d_ref[...] * u_ref[...])
    b = y * jax.nn.sigmoid(_dot(y.astype(BF16), gluw[...]) + glub[...])
    dm = x_ref.shape[1]
    gate = lambda j: jax.nn.sigmoid(g_ref[:, j * dm:(j + 1) * dm].astype(F32))
    merged = (gate(0) * _dot(a_ref[...], wa[...]) + gate(1) * _dot(b.astype(BF16), wb[...])
              + gate(2) * _dot(c_ref[...], wc[...]))
    r = alpha * x_ref[...] + _dot(merged.astype(BF16), wo[...])
    h_ref[...] = _layer_norm(r, lng[...], lnb[...])


def _merge(x, a, y5, u, c, g, w, alpha, tm):
    m, dm = x.shape
    acts = (x, a, y5, u, c, g)
    consts = (w["s5_d"], w["glu_w"], w["glu_b"], w["wa"], w["wb"], w["wc"], w["wo"], w["ln1_g"], w["ln1_b"])
    return pl.pallas_call(
        functools.partial(_merge_body, alpha=alpha),
        grid=(m // tm,),
        in_specs=[pl.BlockSpec((tm, t.shape[1]), lambda i: (i, 0)) for t in acts]
        + [_const_spec(t.shape) for t in consts],
        out_specs=pl.BlockSpec((tm, dm), lambda i: (i, 0)),
        out_shape=jax.ShapeDtypeStruct((m, dm), F32),
        compiler_params=_params("parallel"),
        name="merge",
    )(*acts, *consts)


def _moe_body(h_ref, rw, rb, wg, wu, wd, sgu, sd, lng, lnb, y_ref, xb_s, gate_s, acc_s, *, alpha):
    e = pl.program_id(1)
    lane = lax.broadcasted_iota(jnp.int32, gate_s.shape, 1)

    @pl.when(e == 0)
    def _():
        xb = h_ref[...].astype(BF16)
        xb_s[...] = xb
        scores = jax.nn.sigmoid(_dot(xb, rw[...]))
        ranked = jnp.where(lane < N_EXPERTS, scores + rb[...], -jnp.inf)
        picked = _top_mask(ranked, lane, TOP_K) * scores
        gate_s[...] = picked / jnp.sum(picked, axis=-1, keepdims=True) * ROUTED_SCALE
        hs = _dot(xb, sgu[...])
        ff = sd.shape[0]
        acc_s[...] = _dot((_silu(hs[:, :ff]) * hs[:, ff:]).astype(BF16), sd[...])

    xb = xb_s[...]
    gcol = jnp.sum(jnp.where(lane == e, gate_s[...], 0.0), axis=-1, keepdims=True)
    hm = _silu(_dot(xb, wg[...])) * _dot(xb, wu[...]) * gcol
    acc_s[...] += _dot(hm.astype(BF16), wd[...])

    @pl.when(e == pl.num_programs(1) - 1)
    def _():
        y_ref[...] = _layer_norm(alpha * h_ref[...] + acc_s[...], lng[...], lnb[...])


def _moe(h, w, alpha, tm):
    m, dm = h.shape
    consts_a = (w["router_w"], w["router_b"])
    consts_b = (w["sh_gu"], w["sh_d"], w["ln2_g"], w["ln2_b"])
    layer = w["layer"]
    return pl.pallas_call(
        functools.partial(_moe_body, alpha=alpha),
        grid=(m // tm, N_EXPERTS),
        in_specs=[pl.BlockSpec((tm, dm), lambda i, e: (i, 0))]
        + [_const_spec(t.shape) for t in consts_a]
        + [pl.BlockSpec((None, None, dm, EXPERT_FF), lambda i, e: (layer, e, 0, 0)),
           pl.BlockSpec((None, None, dm, EXPERT_FF), lambda i, e: (layer, e, 0, 0)),
           pl.BlockSpec((None, None, EXPERT_FF, dm), lambda i, e: (layer, e, 0, 0))]
        + [_const_spec(t.shape) for t in consts_b],
        out_specs=pl.BlockSpec((tm, dm), lambda i, e: (i, 0)),
        out_shape=jax.ShapeDtypeStruct((m, dm), F32),
        scratch_shapes=[pltpu.VMEM((tm, dm), BF16), pltpu.VMEM((tm, LANES), F32), pltpu.VMEM((tm, dm), F32)],
        compiler_params=_params("parallel", "arbitrary"),
        name="moe",
    )(h, *consts_a, w["exp_g"], w["exp_u"], w["exp_d"], *consts_b)


def _layer_weights(l, p):
    w_in = p["w_in"][l]
    o = 0
    cols = {}
    for name, size in (("dn", 4 * DN_WIDTH), ("ba", 2 * DN_HEADS), ("s5", S5_WIDTH), ("mq", MB_WIDTH),
                       ("k", MB_WIDTH), ("v", MB_WIDTH), ("g", 3 * w_in.shape[0])):
        cols[name] = w_in[:, o:o + size]
        o += size
    w = {"w_" + k: v.astype(BF16) for k, v in cols.items()}
    w["w_ba"] = jnp.pad(w["w_ba"], ((0, 0), (0, LANES - 2 * DN_HEADS)))
    row = lambda v: v.reshape(1, -1)
    w["conv_w"] = p["dn_conv_w"][l]
    w["dn_ab"] = _dn_gate_rows(p["dn_a_log"][l], p["dn_dt_bias"][l])
    w["dn_norm_w"] = row(p["dn_norm_w"][l])
    s5 = tuple(p[k][l] for k in ("s5_a_re", "s5_a_im", "s5_log_dt", "s5_b_re", "s5_b_im", "s5_c_re", "s5_c_im"))
    w["s5_prompt"] = _s5_prompt_weights(*s5)
    w["s5_step"] = _s5_step_weights(*s5)
    w["s5_d"] = row(p["s5_d"][l])
    w["glu_w"] = p["s5_glu_w"][l].astype(BF16)
    w["glu_b"] = row(p["s5_glu_b"][l])
    w["wa"] = p["w_branch_a"][l].astype(BF16)
    w["wb"] = p["w_branch_b"][l].astype(BF16)
    w["wc"] = p["w_branch_c"][l].astype(BF16)
    w["wo"] = p["w_out"][l].astype(BF16)
    w["ln1_g"], w["ln1_b"] = row(p["ln1_g"][l]), row(p["ln1_b"][l])
    w["ln2_g"], w["ln2_b"] = row(p["ln2_g"][l]), row(p["ln2_b"][l])
    w["router_w"] = jnp.pad(p["router_w"][l].astype(BF16), ((0, 0), (0, LANES - N_EXPERTS)))
    w["router_b"] = jnp.pad(row(p["router_bias"][l]), ((0, 0), (0, LANES - N_EXPERTS)))
    w["layer"] = l
    w["exp_g"], w["exp_u"], w["exp_d"] = p["exp_g16"], p["exp_u16"], p["exp_d16"]
    w["sh_gu"] = jnp.concatenate([p["sh_w_gate"][l], p["sh_w_up"][l]], axis=-1).astype(BF16)
    w["sh_d"] = p["sh_w_down"][l].astype(BF16)
    return w


def _prompt_layer(x, w, alpha, tm, tm_moe):
    n, l, dm = x.shape
    x2 = x.reshape(n * l, dm)
    pdn, pba, ps5, pmq, pk, pv, pg = _proj(x2, w, tm)
    a_out, s_fin = _deltanet_prompt(pdn.reshape(n, l, -1), pba.reshape(n, l, -1), w["conv_w"], w["dn_ab"],
                                    w["dn_norm_w"])
    y5, s5_re, s5_im = _s5_prompt(ps5.reshape(n, l, -1), w["s5_prompt"])
    c_out = _moba_prompt(pmq.reshape(n, l, -1), pk.reshape(n, l, -1), pv.reshape(n, l, -1))
    pk, pv = pk.reshape(n, l, MB_HEADS, MB_HD), pv.reshape(n, l, MB_HEADS, MB_HD)
    h = _merge(x2, a_out.reshape(n * l, -1), y5.reshape(n * l, -1), ps5, c_out.reshape(n * l, -1), pg, w, alpha, tm)
    y = _moe(h, w, alpha, tm_moe)
    conv_new = pdn.reshape(n, l, -1)[:, l - (DN_CONV - 1):, :3 * DN_WIDTH].astype(F32)
    return y.reshape(n, l, dm), (pk, pv, conv_new, s_fin, s5_re, s5_im)


def _sample_layer(x, w, alpha, conv_buf, s_dn, s5_re, s5_im, k_pool, v_pool, page_table, layer):
    n, l, dm = x.shape
    assert l == 1
    x2 = x.reshape(n, dm)
    pdn, pba, ps5, pmq, pk, pv, pg = _proj(x2, w, n, dn_dtype=F32)
    a_out, s_new = _deltanet_step(pdn.reshape(n, 1, -1), pba.reshape(n, 1, -1), conv_buf, s_dn, w["conv_w"],
                                  w["dn_ab"], w["dn_norm_w"])
    y5, s5_re_new, s5_im_new = _s5_step(ps5, s5_re, s5_im, w["s5_step"])
    sums = _block_key_sums(k_pool, page_table, layer)
    sel = _moba_pick(pmq.reshape(n, 1, -1), sums)
    sel = sel[:, :MB_HEADS, :MB_TOPK].reshape(-1)
    heads = lambda t: t.reshape(n, MB_HEADS, 1, MB_HD)
    c_out = _moba_step(heads(pmq), heads(pk), heads(pv), sel, page_table, k_pool, v_pool, layer)
    h = _merge(x2, a_out.reshape(n, -1), y5, ps5, c_out.reshape(n, -1), pg, w, alpha, n)
    y = _moe(h, w, alpha, n)
    conv_new = jnp.concatenate([conv_buf[:, 1:], pdn[:, None, :3 * DN_WIDTH].astype(F32)], axis=1)
    heads4 = lambda t: t.reshape(n, 1, MB_HEADS, MB_HD)
    return y.reshape(n, 1, dm), (heads4(pk), heads4(pv), conv_new, s_new, s5_re_new, s5_im_new)


def kernel(x_prompt, x_sample, cache_k, cache_v, page_table, state_dn_conv, state_dn, state_s5_re, state_s5_im, w_in, dn_conv_w, dn_a_log, dn_dt_bias, dn_norm_w, s5_a_re, s5_a_im, s5_log_dt, s5_b_re, s5_b_im, s5_c_re, s5_c_im, s5_d, s5_glu_w, s5_glu_b, w_branch_a, w_branch_b, w_branch_c, w_out, ln1_g, ln1_b, router_w, router_bias, exp_w_gate, exp_w_up, exp_w_down, sh_w_gate, sh_w_up, sh_w_down, ln2_g, ln2_b):
    p = dict(w_in=w_in, dn_conv_w=dn_conv_w, dn_a_log=dn_a_log, dn_dt_bias=dn_dt_bias, dn_norm_w=dn_norm_w,
             s5_a_re=s5_a_re, s5_a_im=s5_a_im, s5_log_dt=s5_log_dt, s5_b_re=s5_b_re, s5_b_im=s5_b_im,
             s5_c_re=s5_c_re, s5_c_im=s5_c_im, s5_d=s5_d, s5_glu_w=s5_glu_w, s5_glu_b=s5_glu_b,
             w_branch_a=w_branch_a, w_branch_b=w_branch_b, w_branch_c=w_branch_c, w_out=w_out,
             ln1_g=ln1_g, ln1_b=ln1_b, router_w=router_w, router_bias=router_bias, exp_w_gate=exp_w_gate,
             exp_w_up=exp_w_up, exp_w_down=exp_w_down, sh_w_gate=sh_w_gate, sh_w_up=sh_w_up,
             sh_w_down=sh_w_down, ln2_g=ln2_g, ln2_b=ln2_b)
    p["exp_g16"], p["exp_u16"], p["exp_d16"] = (t.astype(BF16) for t in (exp_w_gate, exp_w_up, exp_w_down))
    depth = w_in.shape[0]
    alpha = (2 * depth) ** 0.25
    seq = x_prompt.shape[1]
    tm = min(512, x_prompt.shape[0] * seq)
    tm_moe = min(1024, x_prompt.shape[0] * seq)
    hp, hs = x_prompt, x_sample
    p_st, s_st = [], []
    for l in range(depth):
        w = _layer_weights(l, p)
        hp, st = _prompt_layer(hp, w, alpha, tm, tm_moe)
        p_st.append(st)
        hs, st = _sample_layer(hs, w, alpha, state_dn_conv[l], state_dn[l], state_s5_re[l], state_s5_im[l],
                               cache_k, cache_v, page_table, l)
        s_st.append(st)
    p_out = [jnp.stack(t) for t in zip(*p_st)]
    s_out = [jnp.stack(t) for t in zip(*s_st)]
    return (hp, hs, *p_out, *s_out)
```

```python
import functools
import math

import jax
import jax.numpy as jnp
import numpy as np
from jax import lax
from jax.experimental import pallas as pl
from jax.experimental.pallas import tpu as pltpu

F32 = jnp.float32
BF16 = jnp.bfloat16

DN_HEADS = 4
DN_DK = 128
DN_DV = 128
DN_WIDTH = DN_HEADS * DN_DK
DN_CONV = 4
DN_CHUNK = 64
S5_WIDTH = 512
S5_GROUP = 16
S5_GROUPS = S5_WIDTH // S5_GROUP
S5_STATE = 64
MB_HEADS = 4
MB_HD = 128
MB_WIDTH = MB_HEADS * MB_HD
MB_BLOCK = 256
MB_TOPK = 3
PAGE_SIZE = 128
N_EXPERTS = 64
TOP_K = 8
EXPERT_FF = 256
ROUTED_SCALE = 2.5
LN_EPS = 1e-5
NORM_EPS = 1e-6

LANES = 128
S5_T = 8
DN_TB = 256
VMEM_LIMIT = 56 * 1024 * 1024
NEG = -1e30


def _params(*sem):
    return pltpu.CompilerParams(dimension_semantics=sem, vmem_limit_bytes=VMEM_LIMIT)


def _const_spec(shape):
    nd = len(shape)
    return pl.BlockSpec(shape, lambda *_: (0,) * nd)


def _dot(a, b):
    return jnp.dot(a, b, preferred_element_type=F32)


def _dot_nt(a, b, precision=None):
    return lax.dot_general(a, b, (((1,), (1,)), ((), ())), preferred_element_type=F32, precision=precision)


def _dot_tn(a, b):
    return lax.dot_general(a, b, (((0,), (0,)), ((), ())), preferred_element_type=F32)


def _split(a):
    hi = a.astype(BF16)
    return hi, (a - hi.astype(F32)).astype(BF16)


def _split3(a):
    hi = a.astype(BF16)
    r = a - hi.astype(F32)
    mid = r.astype(BF16)
    return hi, mid, (r - mid.astype(F32)).astype(BF16)


def _mm3(a, b):
    a_hi, a_lo = _split(a)
    b_hi, b_lo = _split(b)
    return _dot(jnp.concatenate([a_hi, a_hi, a_lo], axis=1), jnp.concatenate([b_hi, b_lo, b_hi], axis=0))


def _silu(x):
    return x * jax.nn.sigmoid(x)


def _layer_norm(r, g, b):
    mu = jnp.mean(r, axis=-1, keepdims=True)
    c = r - mu
    var = jnp.mean(c * c, axis=-1, keepdims=True)
    return c * lax.rsqrt(var + LN_EPS) * g + b


def _proj_body(x_ref, wdn, wba, ws5, wmq, wk, wv, wg, odn, oba, os5, omq, ok, ov, og, *, q_scale):
    xb = x_ref[...].astype(BF16)

    def mm(w_ref, o_ref, scale=None):
        n = w_ref.shape[1]
        step = min(n, 512)
        for c in range(0, n, step):
            r = _dot(xb, w_ref[:, c:c + step])
            if scale is not None:
                r = r * scale
            o_ref[:, c:c + step] = r.astype(o_ref.dtype)

    mm(wdn, odn)
    mm(wba, oba)
    mm(ws5, os5)
    mm(wmq, omq, q_scale)
    mm(wk, ok)
    mm(wv, ov)
    mm(wg, og)


def _proj(x, w, tm, dn_dtype=BF16):
    m, d = x.shape
    outs = (("dn", dn_dtype), ("ba", F32), ("s5", F32), ("mq", BF16), ("k", F32), ("v", F32), ("g", BF16))
    ws = [w["w_" + k] for k, _ in outs]
    return pl.pallas_call(
        functools.partial(_proj_body, q_scale=MB_HD ** -0.5),
        grid=(m // tm,),
        in_specs=[pl.BlockSpec((tm, d), lambda i: (i, 0))] + [_const_spec(wi.shape) for wi in ws],
        out_specs=[pl.BlockSpec((tm, wi.shape[1]), lambda i: (i, 0)) for wi in ws],
        out_shape=[jax.ShapeDtypeStruct((m, wi.shape[1]), dt) for wi, (_, dt) in zip(ws, outs)],
        compiler_params=_params("parallel"),
        name="proj",
    )(x, *ws)


def _dn_body(pdn_ref, pba_ref, cw_ref, ab_ref, nw_ref, o_ref, s_out_ref, buf, s_scr):
    tb = DN_TB
    t = pl.program_id(1)

    @pl.when(t == 0)
    def _():
        buf[0:8, :] = jnp.zeros((8, buf.shape[1]), F32)
        s_scr[...] = jnp.zeros(s_scr.shape, F32)

    buf[8:8 + tb, :] = pdn_ref[0, :, 0:3 * DN_WIDTH].astype(F32)

    def conv(c0):
        cs = slice(c0, c0 + DN_DK)
        y = cw_ref[3:4, cs] * buf[8:8 + tb, cs]
        for j in range(1, DN_CONV):
            y = y + cw_ref[3 - j:4 - j, cs] * buf[8 - j:8 - j + tb, cs]
        return _silu(y)

    def l2norm(x):
        return x * lax.rsqrt(jnp.sum(x * x, axis=-1, keepdims=True) + NORM_EPS)

    pba = pba_ref[0]
    beta_all = jax.nn.sigmoid(pba)
    g_all = -jnp.exp(ab_ref[0:1, :]) * jax.nn.softplus(pba + ab_ref[1:2, :])

    row = lax.broadcasted_iota(jnp.int32, (tb, tb), 0)
    col = lax.broadcasted_iota(jnp.int32, (tb, tb), 1)
    same64 = (row >> 6) == (col >> 6)
    same32 = (row >> 5) == (col >> 5)
    same16 = (row >> 4) == (col >> 4)
    incl = jnp.logical_and(same64, row >= col)
    strict = jnp.logical_and(same64, row > col)
    g_hi, g_mid, g_lo = _split3(g_all)
    ones_incl = jnp.where(incl, 1.0, 0.0).astype(BF16)
    gc = _dot(jnp.concatenate([ones_incl] * 3, axis=1), jnp.concatenate([g_hi, g_mid, g_lo], axis=0))
    gl = jnp.concatenate(
        [jnp.broadcast_to(gc[(c + 1) * DN_CHUNK - 1:(c + 1) * DN_CHUNK, :], (DN_CHUNK, LANES))
         for c in range(tb // DN_CHUNK)], axis=0)
    gct = gc.T
    eye = jnp.where(row == col, 1.0, 0.0)

    heads = range(DN_HEADS)
    q = [l2norm(conv(h * DN_DK)) * DN_DK ** -0.5 for h in heads]
    k = [l2norm(conv(DN_WIDTH + h * DN_DK)) for h in heads]
    v = [conv(2 * DN_WIDTH + h * DN_DK) for h in heads]
    beta = [beta_all[:, h:h + 1] for h in heads]
    gcol = [gc[:, DN_HEADS + h:DN_HEADS + h + 1] for h in heads]
    glast = [gl[:, DN_HEADS + h:DN_HEADS + h + 1] for h in heads]
    dec = [jnp.exp(jnp.where(incl, gcol[h] - gct[DN_HEADS + h:DN_HEADS + h + 1, :], NEG)) for h in heads]
    kb = [k[h] * beta[h] for h in heads]
    k16 = [k[h].astype(BF16) for h in heads]
    a = [jnp.where(strict, _dot_nt(kb[h].astype(BF16), k16[h]) * dec[h], 0.0) for h in heads]
    qk = [_dot_nt(q[h].astype(BF16), k16[h]) * dec[h] for h in heads]

    n1 = [jnp.where(same16, -a[h], 0.0) for h in heads]
    n2 = [_mm3(n1[h], n1[h]) for h in heads]
    n4 = [_mm3(n2[h], n2[h]) for h in heads]
    n8 = [_mm3(n4[h], n4[h]) for h in heads]
    p = [eye + n1[h] for h in heads]
    p = [p[h] + _mm3(p[h], n2[h]) for h in heads]
    p = [p[h] + _mm3(p[h], n4[h]) for h in heads]
    p = [p[h] + _mm3(p[h], n8[h]) for h in heads]
    off16 = jnp.logical_and(same32, jnp.logical_not(same16))
    t1 = [_mm3(jnp.where(off16, a[h], 0.0), p[h]) for h in heads]
    p = [p[h] - _mm3(p[h], t1[h]) for h in heads]
    t2 = [_mm3(jnp.where(same32, 0.0, a[h]), p[h]) for h in heads]
    p = [p[h] - _mm3(p[h], t2[h]) for h in heads]

    eg = [jnp.exp(gcol[h]) for h in heads]
    sol = [_mm3(p[h], jnp.concatenate([v[h] * beta[h], kb[h] * eg[h]], axis=1)) for h in heads]
    qd = [q[h] * eg[h] for h in heads]
    kd = [k[h] * jnp.exp(glast[h] - gcol[h]) for h in heads]

    sh = [s_scr[h] for h in heads]
    for c in range(tb // DN_CHUNK):
        r = slice(c * DN_CHUNK, (c + 1) * DN_CHUNK)
        for h in heads:
            wq = jnp.concatenate([sol[h][r, DN_DV:], qd[h][r]], axis=0).astype(BF16)
            ws_ = _dot(wq, sh[h].astype(BF16))
            vn16 = (sol[h][r, :DN_DV] - ws_[:DN_CHUNK]).astype(BF16)
            o = ws_[DN_CHUNK:] + _dot(qk[h][r, r].astype(BF16), vn16)
            sh[h] = (sh[h] * jnp.exp(glast[h][c * DN_CHUNK:c * DN_CHUNK + 1, :])
                     + _dot_tn(kd[h][r].astype(BF16), vn16))
            z = pdn_ref[0, r, 3 * DN_WIDTH + h * DN_DV:3 * DN_WIDTH + (h + 1) * DN_DV].astype(F32)
            o = o * lax.rsqrt(jnp.mean(o * o, axis=-1, keepdims=True) + NORM_EPS) * nw_ref[...] * _silu(z)
            o_ref[0, r, h * DN_DV:(h + 1) * DN_DV] = o.astype(o_ref.dtype)
    for h in heads:
        s_scr[h] = sh[h]

    buf[0:8, :] = buf[tb:tb + 8, :]

    @pl.when(t == pl.num_programs(1) - 1)
    def _():
        s_out_ref[0] = s_scr[...]


def _dn_gate_rows(a_log, dt_bias):
    ab = jnp.zeros((2, LANES), F32)
    ab = ab.at[0, DN_HEADS:2 * DN_HEADS].set(a_log)
    return ab.at[1, DN_HEADS:2 * DN_HEADS].set(dt_bias)


def _deltanet_prompt(pdn, pba, conv_w, ab, norm_w):
    n, l, _ = pdn.shape
    assert l % DN_TB == 0
    return pl.pallas_call(
        _dn_body,
        grid=(n, l // DN_TB),
        in_specs=[
            pl.BlockSpec((1, DN_TB, 4 * DN_WIDTH), lambda i, t: (i, t, 0)),
            pl.BlockSpec((1, DN_TB, LANES), lambda i, t: (i, t, 0)),
            _const_spec(conv_w.shape), _const_spec(ab.shape), _const_spec(norm_w.shape),
        ],
        out_specs=[
            pl.BlockSpec((1, DN_TB, DN_WIDTH), lambda i, t: (i, t, 0)),
            pl.BlockSpec((1, DN_HEADS, DN_DK, DN_DV), lambda i, t: (i, 0, 0, 0)),
        ],
        out_shape=[
            jax.ShapeDtypeStruct((n, l, DN_WIDTH), BF16),
            jax.ShapeDtypeStruct((n, DN_HEADS, DN_DK, DN_DV), F32),
        ],
        scratch_shapes=[pltpu.VMEM((DN_TB + 8, 3 * DN_WIDTH), F32), pltpu.VMEM((DN_HEADS, DN_DK, DN_DV), F32)],
        compiler_params=_params("parallel", "arbitrary"),
        name="deltanet_prompt",
    )(pdn, pba, conv_w, ab, norm_w)


def _dn_step_body(pdn_ref, pba_ref, cbuf_ref, s0_ref, cw_ref, ab_ref, nw_ref, o_ref, s_out_ref):
    x = pdn_ref[0, :, 0:3 * DN_WIDTH].astype(F32)
    y = cw_ref[3:4, :] * x
    for j in range(1, DN_CONV):
        y = y + cw_ref[3 - j:4 - j, :] * cbuf_ref[0, 3 - j:4 - j, :]
    qkv = _silu(y)
    pba = pba_ref[0]
    beta_all = jax.nn.sigmoid(pba)
    g_all = -jnp.exp(ab_ref[0:1, :]) * jax.nn.softplus(pba + ab_ref[1:2, :])

    def l2norm(t):
        return t * lax.rsqrt(jnp.sum(t * t, axis=-1, keepdims=True) + NORM_EPS)

    sub = lax.broadcasted_iota(jnp.int32, (8, DN_DK), 0)
    for h in range(DN_HEADS):
        q = l2norm(qkv[:, h * DN_DK:(h + 1) * DN_DK]) * DN_DK ** -0.5
        k = l2norm(qkv[:, DN_WIDTH + h * DN_DK:DN_WIDTH + (h + 1) * DN_DK])
        v = qkv[:, 2 * DN_WIDTH + h * DN_DV:2 * DN_WIDTH + (h + 1) * DN_DV]
        beta = beta_all[:, h:h + 1]
        eg = jnp.exp(g_all[:, DN_HEADS + h:DN_HEADS + h + 1])
        s0 = s0_ref[0, h]
        w = k * beta * eg
        lhs = jnp.where(sub == 0, w, jnp.where(sub == 1, q * eg, 0.0))
        ws_ = jnp.dot(lhs, s0, precision=lax.Precision.HIGHEST, preferred_element_type=F32)
        vn = v * beta - ws_[0:1, :]
        o = ws_[1:2, :] + jnp.sum(q * k, axis=-1, keepdims=True) * vn
        kt = jnp.broadcast_to(k, (DN_DK, DN_DK)).T
        s_out_ref[0, h] = s0 * eg + kt * vn
        z = pdn_ref[0, :, 3 * DN_WIDTH + h * DN_DV:3 * DN_WIDTH + (h + 1) * DN_DV].astype(F32)
        o = o * lax.rsqrt(jnp.mean(o * o, axis=-1, keepdims=True) + NORM_EPS) * nw_ref[...] * _silu(z)
        o_ref[0, :, h * DN_DV:(h + 1) * DN_DV] = o.astype(o_ref.dtype)


def _deltanet_step(pdn, pba, conv_buf, s0, conv_w, ab, norm_w):
    n = pdn.shape[0]
    return pl.pallas_call(
        _dn_step_body,
        grid=(n,),
        in_specs=[
            pl.BlockSpec((1, 1, 4 * DN_WIDTH), lambda i: (i, 0, 0)),
            pl.BlockSpec((1, 1, LANES), lambda i: (i, 0, 0)),
            pl.BlockSpec((1, DN_CONV - 1, 3 * DN_WIDTH), lambda i: (i, 0, 0)),
            pl.BlockSpec((1, DN_HEADS, DN_DK, DN_DV), lambda i: (i, 0, 0, 0)),
            _const_spec(conv_w.shape), _const_spec(ab.shape), _const_spec(norm_w.shape),
        ],
        out_specs=[
            pl.BlockSpec((1, 1, DN_WIDTH), lambda i: (i, 0, 0)),
            pl.BlockSpec((1, DN_HEADS, DN_DK, DN_DV), lambda i: (i, 0, 0, 0)),
        ],
        out_shape=[
            jax.ShapeDtypeStruct((n, 1, DN_WIDTH), BF16),
            jax.ShapeDtypeStruct((n, DN_HEADS, DN_DK, DN_DV), F32),
        ],
        compiler_params=_params("parallel"),
        name="deltanet_step",
    )(pdn, pba, conv_buf, s0, conv_w, ab, norm_w)


def _s5_discretize(a_re, a_im, log_dt, b_re, b_im):
    dt = jnp.exp(log_dt)[:, None]
    mag = jnp.exp(a_re * dt)
    abr, abi = mag * jnp.cos(a_im * dt), mag * jnp.sin(a_im * dt)
    den = a_re * a_re + a_im * a_im
    fr = ((abr - 1.0) * a_re + abi * a_im) / den
    fi = (abi * a_re - (abr - 1.0) * a_im) / den
    bbr = fr[..., None] * b_re - fi[..., None] * b_im
    bbi = fr[..., None] * b_im + fi[..., None] * b_re
    return abr, abi, bbr, bbi


def _s5_powers(a_re, a_im, log_dt, taus):
    dt = jnp.exp(log_dt)[None, :, None]
    tau = jnp.asarray(taus, F32)[:, None, None]
    mag = jnp.exp(a_re[None] * dt * tau)
    ang = a_im[None] * dt * tau
    return mag * jnp.cos(ang), mag * jnp.sin(ang)


S5_GB = LANES // S5_GROUP
S5_NGB = S5_GROUPS // S5_GB
S5_K = S5_T * LANES
S5_HALF = S5_GB * S5_STATE
S5_ROWS = 256


def _s5_prompt_weights(a_re, a_im, log_dt, b_re, b_im, c_re, c_im):
    t = S5_T
    hp = lax.Precision.HIGHEST
    eye = jnp.eye(S5_GB, dtype=F32)
    _, _, bbr, bbi = _s5_discretize(a_re, a_im, log_dt, b_re, b_im)
    lr, li = _s5_powers(a_re, a_im, log_dt, list(range(t + 1)))
    lbr = lr[..., None] * bbr[None] - li[..., None] * bbi[None]
    lbi = lr[..., None] * bbi[None] + li[..., None] * bbr[None]
    kt = (jnp.einsum("ghp,tgpk->tghk", c_re, lbr, precision=hp)
          - jnp.einsum("ghp,tgpk->tghk", c_im, lbi, precision=hp))
    lag = np.arange(t)[None, :] - np.arange(t)[:, None]
    kblk = kt.transpose(0, 1, 3, 2).reshape(t + 1, S5_NGB, S5_GB, S5_GROUP, S5_GROUP)
    kblk = jnp.einsum("tBaph,am->tBapmh", kblk, eye).reshape(t + 1, S5_NGB, LANES, LANES)
    toep = jnp.where((lag >= 0)[:, :, None, None, None], kblk[np.clip(lag, 0, t)], 0.0)
    toep = toep.transpose(2, 0, 3, 1, 4).reshape(S5_NGB, S5_K, S5_K)
    sel = t - 1 - np.arange(t)
    wz = jnp.stack([lbr[sel], lbi[sel]])
    wz = wz.transpose(2, 1, 4, 0, 3).reshape(S5_NGB, S5_GB, t, S5_GROUP, 2, S5_STATE)
    wz = jnp.einsum("Baipcq,am->Biapcmq", wz, eye).reshape(S5_NGB, S5_K, 2 * S5_HALF)
    lr1, li1 = lr[1:], li[1:]
    wy = jnp.stack([c_re[None] * lr1[:, :, None, :] - c_im[None] * li1[:, :, None, :],
                    -(c_re[None] * li1[:, :, None, :] + c_im[None] * lr1[:, :, None, :])])
    wy = wy.transpose(2, 0, 4, 1, 3).reshape(S5_NGB, S5_GB, 2, S5_STATE, t, S5_GROUP)
    wy = jnp.einsum("Bacpjq,am->Bcapjmq", wy, eye).reshape(S5_NGB, 2 * S5_HALF, S5_K)
    n_steps = int(math.log2(S5_ROWS))
    sr, si = _s5_powers(a_re, a_im, log_dt, [t * (1 << i) for i in range(n_steps)])
    blocks = lambda v: v.reshape(n_steps, S5_NGB, S5_HALF).transpose(1, 0, 2)
    sc_r = jnp.concatenate([blocks(sr), blocks(sr)], axis=-1)
    sc_i = jnp.concatenate([-blocks(si), blocks(si)], axis=-1)
    return toep.astype(BF16), wz.astype(BF16), wy.astype(BF16), sc_r, sc_i


def _s5_body(u_ref, toep_ref, wz_ref, wy_ref, lr_ref, li_ref, y_ref, sfin_ref, carry_s):
    rb, b = pl.program_id(1), pl.program_id(2)
    lanes = [pl.ds(pl.multiple_of(t * S5_WIDTH + b * LANES, LANES), LANES) for t in range(S5_T)]
    u = jnp.concatenate([u_ref[:, lanes[t]] for t in range(S5_T)], axis=1).astype(BF16)

    @pl.when(rb == 0)
    def _():
        carry_s[b] = jnp.zeros((1, 2 * S5_HALF), F32)

    swap = lambda v: pltpu.roll(v, S5_HALF, 1)
    z = _dot(u, wz_ref[...])
    rowi = lax.broadcasted_iota(jnp.int32, z.shape, 0)
    carry = carry_s[b]
    s = z + jnp.where(rowi == 0, lr_ref[0:1, :] * carry + li_ref[0:1, :] * swap(carry), 0.0)
    for i in range(lr_ref.shape[0]):
        d = 1 << i
        sh = jnp.where(rowi >= d, pltpu.roll(s, d, 0), 0.0)
        s = s + lr_ref[i:i + 1, :] * sh + li_ref[i:i + 1, :] * swap(sh)
    last = s[z.shape[0] - 1:z.shape[0], :]
    sprev = jnp.where(rowi >= 1, pltpu.roll(s, 1, 0), carry)
    carry_s[b] = last
    sfin_ref[b] = last
    y = _dot(u, toep_ref[...]) + _dot(sprev.astype(BF16), wy_ref[...])
    for t in range(S5_T):
        y_ref[:, lanes[t]] = y[:, t * LANES:(t + 1) * LANES]


def _s5_prompt(u, wts):
    toep, wz, wy, sc_r, sc_i = wts
    n, l, _ = u.shape
    nc = l // S5_T
    assert nc % S5_ROWS == 0
    width = S5_T * S5_WIDTH
    wspec = lambda shp: pl.BlockSpec((None,) + shp[1:], lambda i, r, b: (b, 0, 0))
    y, sfin = pl.pallas_call(
        _s5_body,
        grid=(n, nc // S5_ROWS, S5_NGB),
        in_specs=[pl.BlockSpec((None, S5_ROWS, width), lambda i, r, b: (i, r, 0)),
                  wspec(toep.shape), wspec(wz.shape), wspec(wy.shape), wspec(sc_r.shape), wspec(sc_i.shape)],
        out_specs=[pl.BlockSpec((None, S5_ROWS, width), lambda i, r, b: (i, r, 0)),
                   pl.BlockSpec((None, S5_NGB, 1, 2 * S5_HALF), lambda i, r, b: (i, 0, 0, 0))],
        out_shape=[jax.ShapeDtypeStruct((n, nc, width), F32),
                   jax.ShapeDtypeStruct((n, S5_NGB, 1, 2 * S5_HALF), F32)],
        scratch_shapes=[pltpu.VMEM((S5_NGB, 1, 2 * S5_HALF), F32)],
        compiler_params=_params("parallel", "arbitrary", "arbitrary"),
        name="s5_prompt",
    )(u.reshape(n, nc, width), toep, wz, wy, sc_r, sc_i)
    sfin = sfin.reshape(n, S5_NGB, 2, S5_GB, S5_STATE).transpose(2, 0, 1, 3, 4).reshape(2, n, S5_GROUPS, S5_STATE)
    return y.reshape(n, l, S5_WIDTH), sfin[0], sfin[1]


def _s5_step_weights(a_re, a_im, log_dt, b_re, b_im, c_re, c_im):
    abr, abi, bbr, bbi = _s5_discretize(a_re, a_im, log_dt, b_re, b_im)
    eye = jnp.eye(S5_GROUPS, dtype=F32)
    gp = S5_GROUPS * S5_STATE
    wb = lambda b: jnp.einsum("gph,gk->ghkp", b, eye).reshape(S5_WIDTH, gp).astype(BF16)
    wc = lambda c: jnp.einsum("ghp,gk->gpkh", c, eye).reshape(gp, S5_WIDTH).astype(BF16)
    return (abr.reshape(1, gp), abi.reshape(1, gp), wb(bbr), wb(bbi), wc(c_re), wc(c_im))


def _s5_step_body(u_ref, s0r_ref, s0i_ref, abr_ref, abi_ref, wbr, wbi, wcr, wci, y_ref, sr_ref, si_ref):
    u = u_ref[...].astype(BF16)
    abr, abi = abr_ref[...], abi_ref[...]
    s0r, s0i = s0r_ref[...], s0i_ref[...]
    sr = abr * s0r - abi * s0i + _dot(u, wbr[...])
    si = abr * s0i + abi * s0r + _dot(u, wbi[...])
    sr_ref[...] = sr
    si_ref[...] = si
    y_ref[...] = _dot(sr.astype(BF16), wcr[...]) - _dot(si.astype(BF16), wci[...])


def _s5_step(u, s0_re, s0_im, wts):
    n = u.shape[0]
    gp = S5_GROUPS * S5_STATE
    args = (u, s0_re.reshape(n, gp), s0_im.reshape(n, gp)) + tuple(wts)
    y, sr, si = pl.pallas_call(
        _s5_step_body,
        grid=(1,),
        in_specs=[_const_spec(a.shape) for a in args],
        out_specs=[_const_spec((n, S5_WIDTH)), _const_spec((n, gp)), _const_spec((n, gp))],
        out_shape=[jax.ShapeDtypeStruct((n, S5_WIDTH), F32), jax.ShapeDtypeStruct((n, gp), F32),
                   jax.ShapeDtypeStruct((n, gp), F32)],
        compiler_params=_params("arbitrary"),
        name="s5_step",
    )(*args)
    return y, sr.reshape(n, S5_GROUPS, S5_STATE), si.reshape(n, S5_GROUPS, S5_STATE)


def _top_mask(gate, pos, count, axis=-1):
    sel = jnp.zeros(gate.shape, F32)
    for _ in range(count):
        m = jnp.max(gate, axis=axis, keepdims=True)
        idx = jnp.min(jnp.where(gate == m, pos, 2 ** 30), axis=axis, keepdims=True)
        hit = jnp.logical_and(pos == idx, m > -jnp.inf)
        sel = jnp.where(hit, 1.0, sel)
        gate = jnp.where(pos == idx, -jnp.inf, gate)
    return sel


def _alibi_slopes():
    s = np.exp2(-8.0 * np.arange(1, MB_HEADS + 1, dtype=np.float32) / MB_HEADS).astype(np.float32)
    return jnp.asarray(np.broadcast_to(s[:, None, None], (MB_HEADS, 1, LANES)))


MB_SLOTS = 16
MB_V_ROWS = MB_HD + 16
MB_GROUP = 2
MB_HEADS_PER_STEP = 4


def _moba_body(q_ref, k_ref, v_ref, slope_ref, o_ref, kaug_s, vaug_s, means_s, *, nb):
    blk = MB_BLOCK
    qi = pl.program_id(2)
    heads = range(MB_HEADS_PER_STEP)
    cols = [slice(h * MB_HD, (h + 1) * MB_HD) for h in heads]
    slope = [slope_ref[h, :, 0:1] for h in heads]

    @pl.when(qi == 0)
    def _():
        lane = lax.broadcasted_iota(jnp.int32, (blk, MB_HD), 1)
        off = lax.broadcasted_iota(jnp.int32, (blk, MB_HD), 0).astype(F32)
        means_s[...] = jnp.zeros(means_s.shape, F32)
        vaug_s[:, :, MB_HD:, :] = jnp.ones((len(heads), nb, MB_V_ROWS - MB_HD, blk), BF16)
        for b in range(nb):
            for h in heads:
                kb = k_ref[0, b * blk:(b + 1) * blk, cols[h]]
                means_s[h, b:b + 1, :] = jnp.mean(kb, axis=0, keepdims=True)
                feat = jnp.where(lane == b, 1.0, 0.0)
                feat = jnp.where(lane == MB_SLOTS, slope[h] * off, feat)
                feat = jnp.where(lane == MB_SLOTS + 1, slope[h] * float(b * blk), feat)
                feat = jnp.where(lane == MB_SLOTS + 2, 1.0, feat)
                kaug_s[h, b] = jnp.concatenate([kb.astype(BF16), feat.astype(BF16)], axis=1)
                vaug_s[h, b, 0:MB_HD, :] = v_ref[0, b * blk:(b + 1) * blk, cols[h]].T.astype(BF16)

    slot = lax.broadcasted_iota(jnp.int32, (MB_SLOTS, blk), 0)
    pad = jnp.zeros((MB_HD - 2 * MB_SLOTS, blk), BF16)
    qt = [q_ref[0, :, cols[h]].astype(F32).T for h in heads]
    gate = [jnp.dot(means_s[h], qt[h], precision=lax.Precision.HIGHEST, preferred_element_type=F32) for h in heads]
    sel = [_top_mask(jnp.where(slot < qi, gate[h], -jnp.inf), slot, MB_TOPK, axis=0) for h in heads]
    bias_rows = [jnp.where(slot < 2, 1.0, jnp.where(slot == 2, -slope[h] * (qi * blk).astype(F32), 0.0)).astype(BF16)
                 for h in heads]
    qt16 = [qt[h].astype(BF16) for h in heads]
    q_past = [jnp.concatenate([qt16[h], jnp.where(sel[h] > 0.0, 0.0, NEG).astype(BF16), bias_rows[h], pad], axis=0)
              for h in heads]
    q_own = [jnp.concatenate([qt16[h], jnp.zeros((MB_SLOTS, blk), BF16), bias_rows[h], pad], axis=0) for h in heads]

    key = lax.broadcasted_iota(jnp.int32, (blk, blk), 0)
    qry = lax.broadcasted_iota(jnp.int32, (blk, blk), 1)
    s = [jnp.where(key <= qry, _dot(kaug_s[h, qi], q_own[h]), NEG) for h in heads]
    m = [jnp.max(s[h], axis=0, keepdims=True) for h in heads]
    acc = [_dot(vaug_s[h, qi], jnp.exp(s[h] - m[h]).astype(BF16)) for h in heads]

    grp = MB_GROUP

    def body(i, carry):
        m_old, acc = carry
        s = [[_dot(kaug_s[h, grp * i + g], q_past[h]) for g in range(grp)] for h in heads]
        m_new = list(m_old)
        for g in range(grp):
            m_new = [jnp.maximum(m_new[h], jnp.max(s[h][g], axis=0, keepdims=True)) for h in heads]
        p = [[jnp.exp(s[h][g] - m_new[h]).astype(BF16) for g in range(grp)] for h in heads]
        acc = [jnp.exp(m_old[h] - m_new[h]) * acc[h] for h in heads]
        for g in range(grp):
            acc = [acc[h] + _dot(vaug_s[h, grp * i + g], p[h][g]) for h in heads]
        return tuple(m_new), tuple(acc)

    m, acc = lax.fori_loop(0, (qi + grp - 1) // grp, body, (tuple(m), tuple(acc)))
    for h in heads:
        o_ref[0, :, cols[h]] = (acc[h][0:MB_HD] / acc[h][MB_HD:MB_HD + 1]).T.astype(o_ref.dtype)


def _moba_prompt(q, k, v):
    n, l, _ = q.shape
    nb = l // MB_BLOCK
    hps = MB_HEADS_PER_STEP
    width = hps * MB_HD
    assert l % MB_BLOCK == 0 and nb <= MB_SLOTS and nb % MB_GROUP == 0 and MB_HEADS % hps == 0
    kv_spec = pl.BlockSpec((1, l, width), lambda i, h, t: (i, 0, h))
    return pl.pallas_call(
        functools.partial(_moba_body, nb=nb),
        grid=(n, MB_HEADS // hps, nb),
        in_specs=[
            pl.BlockSpec((1, MB_BLOCK, width), lambda i, h, t: (i, t, h)),
            kv_spec, kv_spec,
            pl.BlockSpec((hps, 1, LANES), lambda i, h, t: (h, 0, 0)),
        ],
        out_specs=pl.BlockSpec((1, MB_BLOCK, width), lambda i, h, t: (i, t, h)),
        out_shape=jax.ShapeDtypeStruct((n, l, MB_WIDTH), BF16),
        scratch_shapes=[pltpu.VMEM((hps, nb, MB_BLOCK, 2 * MB_HD), BF16),
                        pltpu.VMEM((hps, nb, MB_V_ROWS, MB_BLOCK), BF16),
                        pltpu.VMEM((hps, MB_SLOTS, MB_HD), F32)],
        compiler_params=_params("parallel", "parallel", "arbitrary"),
        name="moba_prompt",
    )(q, k, v, _alibi_slopes())


PAGES_PER_STEP = 32
PAGES_PER_BLOCK = MB_BLOCK // PAGE_SIZE


def _page_sum_body(pt_ref, *refs):
    del pt_ref
    o_ref = refs[-1]
    for b in range(PAGES_PER_STEP // PAGES_PER_BLOCK):
        acc = jnp.sum(refs[PAGES_PER_BLOCK * b][...], axis=0)
        for j in range(1, PAGES_PER_BLOCK):
            acc = acc + jnp.sum(refs[PAGES_PER_BLOCK * b + j][...], axis=0)
        o_ref[b] = acc


def _block_key_sums(pool, page_table, layer):
    n, n_pages = page_table.shape
    assert n_pages % PAGES_PER_STEP == 0
    tail = pool.shape[2:]

    def page_spec(j):
        return pl.BlockSpec((None, None) + tail, lambda i, s, pt: (layer, pt[i, s * PAGES_PER_STEP + j], 0, 0, 0))

    blocks_per_step = PAGES_PER_STEP // PAGES_PER_BLOCK
    return pl.pallas_call(
        _page_sum_body,
        grid_spec=pltpu.PrefetchScalarGridSpec(
            num_scalar_prefetch=1,
            grid=(n, n_pages // PAGES_PER_STEP),
            in_specs=[page_spec(j) for j in range(PAGES_PER_STEP)],
            out_specs=pl.BlockSpec((None, blocks_per_step) + tail[1:], lambda i, s, pt: (i, s, 0, 0)),
        ),
        out_shape=jax.ShapeDtypeStruct((n, n_pages // PAGES_PER_BLOCK) + tail[1:], F32),
        compiler_params=_params("parallel", "arbitrary"),
        name="moba_page_sums",
    )(page_table, *([pool] * PAGES_PER_STEP))


def _moba_pick_body(q_ref, sums_ref, sel_ref):
    nbk = sums_ref.shape[0]
    out = jnp.zeros(sel_ref.shape, jnp.int32)
    osub = lax.broadcasted_iota(jnp.int32, out.shape, 0)
    olane = lax.broadcasted_iota(jnp.int32, out.shape, 1)
    bidx = lax.broadcasted_iota(jnp.int32, (nbk, 1), 0)
    for h in range(MB_HEADS):
        means = sums_ref[:, h, :] / float(MB_BLOCK)
        q = q_ref[:, h * MB_HD:(h + 1) * MB_HD].astype(F32)
        gate = jnp.sum(means * q, axis=-1, keepdims=True)
        for rnk in range(MB_TOPK):
            m = jnp.max(gate, axis=0, keepdims=True)
            idx = jnp.min(jnp.where(gate == m, bidx, 2 ** 30), axis=0, keepdims=True)
            out = jnp.where(jnp.logical_and(osub == h, olane == rnk), idx, out)
            gate = jnp.where(bidx == idx, -jnp.inf, gate)
    sel_ref[...] = out


def _moba_pick(q, sums):
    n = q.shape[0]
    return pl.pallas_call(
        _moba_pick_body,
        grid=(n,),
        in_specs=[pl.BlockSpec((None, 1, MB_WIDTH), lambda i: (i, 0, 0)),
                  pl.BlockSpec((None,) + sums.shape[1:], lambda i: (i, 0, 0, 0))],
        out_specs=pl.BlockSpec((None, 8, LANES), lambda i: (i, 0, 0)),
        out_shape=jax.ShapeDtypeStruct((n, 8, LANES), jnp.int32),
        compiler_params=_params("parallel"),
        name="moba_pick",
    )(q, sums)


def _moba_step_body(sel_ref, pt_ref, q_ref, kn_ref, vn_ref, slope_ref, *refs, past):
    del pt_ref
    pages, o_ref, m_s, l_s, acc_s = refs[:4 * MB_HEADS], refs[4 * MB_HEADS], *refs[4 * MB_HEADS + 1:]
    i, s = pl.program_id(0), pl.program_id(1)
    sub = lax.broadcasted_iota(jnp.int32, (PAGE_SIZE, 1), 0)
    for h in range(MB_HEADS):
        q = q_ref[0, h].astype(F32)
        slope = slope_ref[h, :, 0:1]

        @pl.when(s == 0)
        def _():
            m_s[h] = jnp.sum(q * kn_ref[0, h], axis=-1, keepdims=True)
            l_s[h] = jnp.ones((1, 1), F32)
            acc_s[h] = vn_ref[0, h]

        blk = sel_ref[(i * MB_HEADS + h) * MB_TOPK + s]
        k0, k1, v0, v1 = pages[4 * h:4 * h + 4]
        m_old = m_s[h]
        sc = []
        for half, kr in enumerate((k0, k1)):
            kpos = blk * MB_BLOCK + half * PAGE_SIZE + sub
            sc.append(jnp.sum(kr[:, h, :] * q, axis=-1, keepdims=True) - slope * (past - kpos).astype(F32))
        m_new = jnp.maximum(m_old, jnp.maximum(jnp.max(sc[0], axis=0, keepdims=True),
                                               jnp.max(sc[1], axis=0, keepdims=True)))
        alpha = jnp.exp(m_old - m_new)
        p0 = jnp.exp(sc[0] - m_new)
        p1 = jnp.exp(sc[1] - m_new)
        l_new = alpha * l_s[h] + jnp.sum(p0, axis=0, keepdims=True) + jnp.sum(p1, axis=0, keepdims=True)
        acc_new = (alpha * acc_s[h] + jnp.sum(p0 * v0[:, h, :], axis=0, keepdims=True)
                   + jnp.sum(p1 * v1[:, h, :], axis=0, keepdims=True))
        m_s[h] = m_new
        l_s[h] = l_new
        acc_s[h] = acc_new

        @pl.when(s == MB_TOPK - 1)
        def _():
            o_ref[0, h] = (acc_new / l_new).astype(o_ref.dtype)


def _moba_step(q, k_new, v_new, sel, page_table, k_pool, v_pool, layer):
    n = q.shape[0]
    past = page_table.shape[1] * PAGE_SIZE
    assert PAGES_PER_BLOCK == 2 and past // MB_BLOCK >= MB_TOPK
    tail = k_pool.shape[2:]

    def page_spec(h, half):
        def imap(i, s, sel_r, pt_r):
            blk = sel_r[(i * MB_HEADS + h) * MB_TOPK + s]
            return (layer, pt_r[i, blk * PAGES_PER_BLOCK + half], 0, 0, 0)
        return pl.BlockSpec((None, None) + tail, imap)

    page_specs, page_args = [], []
    for h in range(MB_HEADS):
        for pool in (k_pool, v_pool):
            for half in range(PAGES_PER_BLOCK):
                page_specs.append(page_spec(h, half))
                page_args.append(pool)
    tok = pl.BlockSpec((1, MB_HEADS, 1, MB_HD), lambda i, s, *_: (i, 0, 0, 0))
    return pl.pallas_call(
        functools.partial(_moba_step_body, past=past),
        grid_spec=pltpu.PrefetchScalarGridSpec(
            num_scalar_prefetch=2,
            grid=(n, MB_TOPK),
            in_specs=[tok, tok, tok, _const_spec((MB_HEADS, 1, LANES))] + page_specs,
            out_specs=tok,
            scratch_shapes=[pltpu.VMEM((MB_HEADS, 1, 1), F32), pltpu.VMEM((MB_HEADS, 1, 1), F32),
                            pltpu.VMEM((MB_HEADS, 1, MB_HD), F32)],
        ),
        out_shape=jax.ShapeDtypeStruct((n, MB_HEADS, 1, MB_HD), BF16),
        compiler_params=_params("parallel", "arbitrary"),
        name="moba_step",
    )(sel, page_table, q, k_new, v_new, _alibi_slopes(), *page_args)


def _merge_body(x_ref, a_ref, y5_ref, u_ref, c_ref, g_ref, d_ref, gluw, glub, wa, wb, wc, wo, lng, lnb, h_ref,
                *, alpha):
    y = jax.nn.gelu(y5_ref[...] + d_ref[...] * u_ref[...])
    b = y * jax.nn.sigmoid(_dot(y.astype(BF16), gluw[...]) + glub[...])
    dm = x_ref.shape[1]
    gate = lambda j: jax.nn.sigmoid(g_ref[:, j * dm:(j + 1) * dm].astype(F32))
    merged = (gate(0) * _dot(a_ref[...], wa[...]) + gate(1) * _dot(b.astype(BF16), wb[...])
              + gate(2) * _dot(c_ref[...], wc[...]))
    r = alpha * x_ref[...] + _dot(merged.astype(BF16), wo[...])
    h_ref[...] = _layer_norm(r, lng[...], lnb[...])


def _merge(x, a, y5, u, c, g, w, alpha, tm):
    m, dm = x.shape
    acts = (x, a, y5, u, c, g)
    consts = (w["s5_d"], w["glu_w"], w["glu_b"], w["wa"], w["wb"], w["wc"], w["wo"], w["ln1_g"], w["ln1_b"])
    return pl.pallas_call(
        functools.partial(_merge_body, alpha=alpha),
        grid=(m // tm,),
        in_specs=[pl.BlockSpec((tm, t.shape[1]), lambda i: (i, 0)) for t in acts]
        + [_const_spec(t.shape) for t in consts],
        out_specs=pl.BlockSpec((tm, dm), lambda i: (i, 0)),
        out_shape=jax.ShapeDtypeStruct((m, dm), F32),
        compiler_params=_params("parallel"),
        name="merge",
    )(*acts, *consts)


def _moe_body(h_ref, rw, rb, wg, wu, wd, sgu, sd, lng, lnb, y_ref, xb_s, gate_s, acc_s, *, alpha):
    e = pl.program_id(1)
    lane = lax.broadcasted_iota(jnp.int32, gate_s.shape, 1)

    @pl.when(e == 0)
    def _():
        xb = h_ref[...].astype(BF16)
        xb_s[...] = xb
        scores = jax.nn.sigmoid(_dot(xb, rw[...]))
        ranked = jnp.where(lane < N_EXPERTS, scores + rb[...], -jnp.inf)
        picked = _top_mask(ranked, lane, TOP_K) * scores
        gate_s[...] = picked / jnp.sum(picked, axis=-1, keepdims=True) * ROUTED_SCALE
        hs = _dot(xb, sgu[...])
        ff = sd.shape[0]
        acc_s[...] = _dot((_silu(hs[:, :ff]) * hs[:, ff:]).astype(BF16), sd[...])

    xb = xb_s[...]
    gcol = jnp.sum(jnp.where(lane == e, gate_s[...], 0.0), axis=-1, keepdims=True)
    hm = _silu(_dot(xb, wg[...])) * _dot(xb, wu[...]) * gcol
    acc_s[...] += _dot(hm.astype(BF16), wd[...])

    @pl.when(e == pl.num_programs(1) - 1)
    def _():
        y_ref[...] = _layer_norm(alpha * h_ref[...] + acc_s[...], lng[...], lnb[...])


def _moe(h, w, alpha, tm):
    m, dm = h.shape
    consts_a = (w["router_w"], w["router_b"])
    consts_b = (w["sh_gu"], w["sh_d"], w["ln2_g"], w["ln2_b"])
    layer = w["layer"]
    return pl.pallas_call(
        functools.partial(_moe_body, alpha=alpha),
        grid=(m // tm, N_EXPERTS),
        in_specs=[pl.BlockSpec((tm, dm), lambda i, e: (i, 0))]
        + [_const_spec(t.shape) for t in consts_a]
        + [pl.BlockSpec((None, None, dm, EXPERT_FF), lambda i, e: (layer, e, 0, 0)),
           pl.BlockSpec((None, None, dm, EXPERT_FF), lambda i, e: (layer, e, 0, 0)),
           pl.BlockSpec((None, None, EXPERT_FF, dm), lambda i, e: (layer, e, 0, 0))]
        + [_const_spec(t.shape) for t in consts_b],
        out_specs=pl.BlockSpec((tm, dm), lambda i, e: (i, 0)),
        out_shape=jax.ShapeDtypeStruct((m, dm), F32),
        scratch_shapes=[pltpu.VMEM((tm, dm), BF16), pltpu.VMEM((tm, LANES), F32), pltpu.VMEM((tm, dm), F32)],
        compiler_params=_params("parallel", "arbitrary"),
        name="moe",
    )(h, *consts_a, w["exp_g"], w["exp_u"], w["exp_d"], *consts_b)


MOE_TM = 512
MOE_RT = 1024
MOE_CT = 512


def _moe_route_body(h_ref, rw, rb, hb_ref, idx_ref, rank_ref, gate_ref, cnt_ref, carry_s):
    @pl.when(pl.program_id(0) == 0)
    def _():
        carry_s[...] = jnp.zeros(carry_s.shape, F32)

    xb = h_ref[...].astype(BF16)
    hb_ref[...] = xb
    tm = xb.shape[0]
    lane = lax.broadcasted_iota(jnp.int32, (tm, LANES), 1)
    scores = jax.nn.sigmoid(_dot(xb, rw[...]))
    left = jnp.where(lane < N_EXPERTS, scores + rb[...], -jnp.inf)
    sel = jnp.zeros((tm, LANES), F32)
    picks = []
    for _ in range(TOP_K):
        m = jnp.max(left, axis=-1, keepdims=True)
        idx = jnp.min(jnp.where(left == m, lane, 2 ** 30), axis=-1, keepdims=True)
        hit = lane == idx
        sel = jnp.where(hit, 1.0, sel)
        left = jnp.where(hit, -jnp.inf, left)
        picks.append(idx)
    picked = sel * scores
    gates = picked / jnp.sum(picked, axis=-1, keepdims=True) * ROUTED_SCALE

    r_i = lax.broadcasted_iota(jnp.int32, (tm, tm), 0)
    c_i = lax.broadcasted_iota(jnp.int32, (tm, tm), 1)
    before = jnp.where(c_i < r_i, 1.0, 0.0).astype(BF16)
    rank = _dot(before, sel.astype(BF16)) + carry_s[...]
    carry_s[...] += jnp.sum(sel, axis=0, keepdims=True)
    cnt_ref[...] = carry_s[...]

    idx_out = jnp.zeros((tm, LANES), jnp.int32)
    rank_out = jnp.zeros((tm, LANES), F32)
    gate_out = jnp.zeros((tm, LANES), F32)
    for k, idx in enumerate(picks):
        hit = lane == idx
        idx_out = jnp.where(lane == k, idx, idx_out)
        rank_out = jnp.where(lane == k, jnp.sum(jnp.where(hit, rank, 0.0), axis=-1, keepdims=True), rank_out)
        gate_out = jnp.where(lane == k, jnp.sum(jnp.where(hit, gates, 0.0), axis=-1, keepdims=True), gate_out)
    idx_ref[...] = idx_out
    rank_ref[...] = rank_out.astype(jnp.int32)
    gate_ref[...] = gate_out


def _moe_route(h, w):
    m, dm = h.shape
    tm = MOE_RT
    tok = lambda: pl.BlockSpec((tm, LANES), lambda i: (i, 0))
    return pl.pallas_call(
        _moe_route_body,
        grid=(m // tm,),
        in_specs=[pl.BlockSpec((tm, dm), lambda i: (i, 0)), _const_spec(w["router_w"].shape),
                  _const_spec(w["router_b"].shape)],
        out_specs=[pl.BlockSpec((tm, dm), lambda i: (i, 0)), tok(), tok(), tok(), _const_spec((1, LANES))],
        out_shape=[jax.ShapeDtypeStruct((2 * m, dm), BF16), jax.ShapeDtypeStruct((m, LANES), jnp.int32),
                   jax.ShapeDtypeStruct((m, LANES), jnp.int32), jax.ShapeDtypeStruct((m, LANES), F32),
                   jax.ShapeDtypeStruct((1, LANES), F32)],
        scratch_shapes=[pltpu.VMEM((1, LANES), F32)],
        compiler_params=_params("arbitrary"),
        name="moe_route",
    )(h, w["router_w"], w["router_b"])


def _moe_plan(idx8, rank8, counts, m):
    tm = MOE_TM
    rows = TOP_K * m + N_EXPERTS * tm
    counts = counts[0, :N_EXPERTS].astype(jnp.int32)
    padded = (counts + tm - 1) // tm * tm
    ends = jnp.cumsum(padded)
    starts = ends - padded
    first = jnp.cumsum(counts) - counts
    pos8 = starts[idx8] + rank8
    tile_expert = jnp.minimum(jnp.searchsorted(ends, jnp.arange(rows // tm) * tm, side="right"),
                              N_EXPERTS - 1).astype(jnp.int32)
    n_tiles = (ends[-1] // tm).astype(jnp.int32).reshape(1)
    tokens = jnp.sort((idx8 * m + jnp.arange(m, dtype=jnp.int32)[:, None]).reshape(-1)) % m
    row = jnp.arange(rows, dtype=jnp.int32)
    e_row = tile_expert[row // tm]
    r = row - starts[e_row]
    src = jnp.where(r < counts[e_row], tokens[jnp.clip(first[e_row] + r, 0, TOP_K * m - 1)], 0)
    return pos8, src.astype(jnp.int32), tile_expert, n_tiles


def _moe_expert_body(te_ref, nt_ref, x_ref, wg, wu, wd, y_ref):
    del te_ref
    live = pl.program_id(0) < nt_ref[0]

    @pl.when(live)
    def _():
        x = x_ref[...]
        hm = _silu(_dot(x, wg[...])) * _dot(x, wu[...])
        y_ref[...] = _dot(hm.astype(BF16), wd[...]).astype(y_ref.dtype)

    @pl.when(jnp.logical_not(live))
    def _():
        y_ref[...] = jnp.zeros(y_ref.shape, y_ref.dtype)


def _moe_experts(xs, tile_expert, n_tiles, w):
    rows, dm = xs.shape
    layer = w["layer"]
    wspec = lambda a, b: pl.BlockSpec((None, None, a, b), lambda i, te, nt: (layer, te[i], 0, 0))
    return pl.pallas_call(
        _moe_expert_body,
        grid_spec=pltpu.PrefetchScalarGridSpec(
            num_scalar_prefetch=2,
            grid=(rows // MOE_TM,),
            in_specs=[pl.BlockSpec((MOE_TM, dm), lambda i, te, nt: (i, 0)),
                      wspec(dm, EXPERT_FF), wspec(dm, EXPERT_FF), wspec(EXPERT_FF, dm)],
            out_specs=pl.BlockSpec((MOE_TM, dm), lambda i, te, nt: (i, 0)),
        ),
        out_shape=jax.ShapeDtypeStruct((rows, dm), BF16),
        compiler_params=_params("arbitrary"),
        name="moe_experts",
    )(tile_expert, n_tiles, xs, w["exp_g"], w["exp_u"], w["exp_d"])


def _moe_combine_body(h_ref, yg_ref, gate_ref, sgu, sd, lng, lnb, y_ref, *, alpha):
    h = h_ref[...]
    dm = h.shape[1]
    hs = _dot(h.astype(BF16), sgu[...])
    ff = sd.shape[0]
    acc = _dot((_silu(hs[:, :ff]) * hs[:, ff:]).astype(BF16), sd[...])
    gate = gate_ref[...]
    for k in range(TOP_K):
        acc = acc + gate[:, k:k + 1] * yg_ref[:, k * dm:(k + 1) * dm].astype(F32)
    y_ref[...] = _layer_norm(alpha * h + acc, lng[...], lnb[...])


def _moe_combine(h, yg, gate8, w, alpha):
    m, dm = h.shape
    tm = MOE_CT
    consts = (w["sh_gu"], w["sh_d"], w["ln2_g"], w["ln2_b"])
    return pl.pallas_call(
        functools.partial(_moe_combine_body, alpha=alpha),
        grid=(m // tm,),
        in_specs=[pl.BlockSpec((tm, dm), lambda i: (i, 0)), pl.BlockSpec((tm, TOP_K * dm), lambda i: (i, 0)),
                  pl.BlockSpec((tm, LANES), lambda i: (i, 0))] + [_const_spec(t.shape) for t in consts],
        out_specs=pl.BlockSpec((tm, dm), lambda i: (i, 0)),
        out_shape=jax.ShapeDtypeStruct((m, dm), F32),
        compiler_params=_params("parallel"),
        name="moe_combine",
    )(h, yg, gate8, *consts)


def _moe_sparse(h, w, alpha):
    m, dm = h.shape
    hb, idx, rank, gate, counts = _moe_route(h, w)
    pos8, src, tile_expert, n_tiles = _moe_plan(idx[:, :TOP_K], rank[:, :TOP_K], counts, m)
    xs = hb.at[src].get(mode="promise_in_bounds")
    ys = _moe_experts(xs, tile_expert, n_tiles, w)
    yg = ys.at[pos8.reshape(-1)].get(mode="promise_in_bounds").reshape(m, TOP_K * dm)
    return _moe_combine(h, yg, gate, w, alpha)


def _layer_weights(l, p):
    w_in = p["w_in"][l]
    o = 0
    cols = {}
    for name, size in (("dn", 4 * DN_WIDTH), ("ba", 2 * DN_HEADS), ("s5", S5_WIDTH), ("mq", MB_WIDTH),
                       ("k", MB_WIDTH), ("v", MB_WIDTH), ("g", 3 * w_in.shape[0])):
        cols[name] = w_in[:, o:o + size]
        o += size
    w = {"w_" + k: v.astype(BF16) for k, v in cols.items()}
    w["w_ba"] = jnp.pad(w["w_ba"], ((0, 0), (0, LANES - 2 * DN_HEADS)))
    row = lambda v: v.reshape(1, -1)
    w["conv_w"] = p["dn_conv_w"][l]
    w["dn_ab"] = _dn_gate_rows(p["dn_a_log"][l], p["dn_dt_bias"][l])
    w["dn_norm_w"] = row(p["dn_norm_w"][l])
    s5 = tuple(p[k][l] for k in ("s5_a_re", "s5_a_im", "s5_log_dt", "s5_b_re", "s5_b_im", "s5_c_re", "s5_c_im"))
    w["s5_prompt"] = _s5_prompt_weights(*s5)
    w["s5_step"] = _s5_step_weights(*s5)
    w["s5_d"] = row(p["s5_d"][l])
    w["glu_w"] = p["s5_glu_w"][l].astype(BF16)
    w["glu_b"] = row(p["s5_glu_b"][l])
    w["wa"] = p["w_branch_a"][l].astype(BF16)
    w["wb"] = p["w_branch_b"][l].astype(BF16)
    w["wc"] = p["w_branch_c"][l].astype(BF16)
    w["wo"] = p["w_out"][l].astype(BF16)
    w["ln1_g"], w["ln1_b"] = row(p["ln1_g"][l]), row(p["ln1_b"][l])
    w["ln2_g"], w["ln2_b"] = row(p["ln2_g"][l]), row(p["ln2_b"][l])
    w["router_w"] = jnp.pad(p["router_w"][l].astype(BF16), ((0, 0), (0, LANES - N_EXPERTS)))
    w["router_b"] = jnp.pad(row(p["router_bias"][l]), ((0, 0), (0, LANES - N_EXPERTS)))
    w["layer"] = l
    w["exp_g"], w["exp_u"], w["exp_d"] = p["exp_g16"], p["exp_u16"], p["exp_d16"]
    w["sh_gu"] = jnp.concatenate([p["sh_w_gate"][l], p["sh_w_up"][l]], axis=-1).astype(BF16)
    w["sh_d"] = p["sh_w_down"][l].astype(BF16)
    return w


def _prompt_layer(x, w, alpha, tm, tm_moe):
    n, l, dm = x.shape
    x2 = x.reshape(n * l, dm)
    pdn, pba, ps5, pmq, pk, pv, pg = _proj(x2, w, tm)
    a_out, s_fin = _deltanet_prompt(pdn.reshape(n, l, -1), pba.reshape(n, l, -1), w["conv_w"], w["dn_ab"],
                                    w["dn_norm_w"])
    y5, s5_re, s5_im = _s5_prompt(ps5.reshape(n, l, -1), w["s5_prompt"])
    c_out = _moba_prompt(pmq.reshape(n, l, -1), pk.reshape(n, l, -1), pv.reshape(n, l, -1))
    pk, pv = pk.reshape(n, l, MB_HEADS, MB_HD), pv.reshape(n, l, MB_HEADS, MB_HD)
    h = _merge(x2, a_out.reshape(n * l, -1), y5.reshape(n * l, -1), ps5, c_out.reshape(n * l, -1), pg, w, alpha, tm)
    y = _moe_sparse(h, w, alpha) if h.shape[0] % MOE_RT == 0 else _moe(h, w, alpha, tm_moe)
    conv_new = pdn.reshape(n, l, -1)[:, l - (DN_CONV - 1):, :3 * DN_WIDTH].astype(F32)
    return y.reshape(n, l, dm), (pk, pv, conv_new, s_fin, s5_re, s5_im)


def _sample_layer(x, w, alpha, conv_buf, s_dn, s5_re, s5_im, k_pool, v_pool, page_table, layer):
    n, l, dm = x.shape
    assert l == 1
    x2 = x.reshape(n, dm)
    pdn, pba, ps5, pmq, pk, pv, pg = _proj(x2, w, n, dn_dtype=F32)
    a_out, s_new = _deltanet_step(pdn.reshape(n, 1, -1), pba.reshape(n, 1, -1), conv_buf, s_dn, w["conv_w"],
                                  w["dn_ab"], w["dn_norm_w"])
    y5, s5_re_new, s5_im_new = _s5_step(ps5, s5_re, s5_im, w["s5_step"])
    sums = _block_key_sums(k_pool, page_table, layer)
    sel = _moba_pick(pmq.reshape(n, 1, -1), sums)
    sel = sel[:, :MB_HEADS, :MB_TOPK].reshape(-1)
    heads = lambda t: t.reshape(n, MB_HEADS, 1, MB_HD)
    c_out = _moba_step(heads(pmq), heads(pk), heads(pv), sel, page_table, k_pool, v_pool, layer)
    h = _merge(x2, a_out.reshape(n, -1), y5, ps5, c_out.reshape(n, -1), pg, w, alpha, n)
    y = _moe(h, w, alpha, n)
    conv_new = jnp.concatenate([conv_buf[:, 1:], pdn[:, None, :3 * DN_WIDTH].astype(F32)], axis=1)
    heads4 = lambda t: t.reshape(n, 1, MB_HEADS, MB_HD)
    return y.reshape(n, 1, dm), (heads4(pk), heads4(pv), conv_new, s_new, s5_re_new, s5_im_new)


def kernel(x_prompt, x_sample, cache_k, cache_v, page_table, state_dn_conv, state_dn, state_s5_re, state_s5_im, w_in, dn_conv_w, dn_a_log, dn_dt_bias, dn_norm_w, s5_a_re, s5_a_im, s5_log_dt, s5_b_re, s5_b_im, s5_c_re, s5_c_im, s5_d, s5_glu_w, s5_glu_b, w_branch_a, w_branch_b, w_branch_c, w_out, ln1_g, ln1_b, router_w, router_bias, exp_w_gate, exp_w_up, exp_w_down, sh_w_gate, sh_w_up, sh_w_down, ln2_g, ln2_b):
    p = dict(w_in=w_in, dn_conv_w=dn_conv_w, dn_a_log=dn_a_log, dn_dt_bias=dn_dt_bias, dn_norm_w=dn_norm_w,
             s5_a_re=s5_a_re, s5_a_im=s5_a_im, s5_log_dt=s5_log_dt, s5_b_re=s5_b_re, s5_b_im=s5_b_im,
             s5_c_re=s5_c_re, s5_c_im=s5_c_im, s5_d=s5_d, s5_glu_w=s5_glu_w, s5_glu_b=s5_glu_b,
             w_branch_a=w_branch_a, w_branch_b=w_branch_b, w_branch_c=w_branch_c, w_out=w_out,
             ln1_g=ln1_g, ln1_b=ln1_b, router_w=router_w, router_bias=router_bias, exp_w_gate=exp_w_gate,
             exp_w_up=exp_w_up, exp_w_down=exp_w_down, sh_w_gate=sh_w_gate, sh_w_up=sh_w_up,
             sh_w_down=sh_w_down, ln2_g=ln2_g, ln2_b=ln2_b)
    p["exp_g16"], p["exp_u16"], p["exp_d16"] = (t.astype(BF16) for t in (exp_w_gate, exp_w_up, exp_w_down))
    depth = w_in.shape[0]
    alpha = (2 * depth) ** 0.25
    seq = x_prompt.shape[1]
    tm = min(512, x_prompt.shape[0] * seq)
    tm_moe = min(1024, x_prompt.shape[0] * seq)
    hp, hs = x_prompt, x_sample
    p_st, s_st = [], []
    for l in range(depth):
        w = _layer_weights(l, p)
        hp, st = _prompt_layer(hp, w, alpha, tm, tm_moe)
        p_st.append(st)
        hs, st = _sample_layer(hs, w, alpha, state_dn_conv[l], state_dn[l], state_s5_re[l], state_s5_im[l],
                               cache_k, cache_v, page_table, l)
        s_st.append(st)
    p_out = [jnp.stack(t) for t in zip(*p_st)]
    s_out = [jnp.stack(t) for t in zip(*s_st)]
    return (hp, hs, *p_out, *s_out)
```

```python
import functools
import math
from typing import NamedTuple

import jax
import jax.numpy as jnp
import numpy as np
from jax import lax
from jax.experimental import pallas as pl
from jax.experimental.pallas import tpu as pltpu

F32 = jnp.float32
BF16 = jnp.bfloat16

DN_HEADS = 4
DN_DK = 128
DN_DV = 128
DN_WIDTH = DN_HEADS * DN_DK
DN_CONV = 4
DN_CHUNK = 64
S5_WIDTH = 512
S5_GROUP = 16
S5_GROUPS = S5_WIDTH // S5_GROUP
S5_STATE = 64
MB_HEADS = 4
MB_HD = 128
MB_WIDTH = MB_HEADS * MB_HD
MB_BLOCK = 256
MB_TOPK = 3
PAGE_SIZE = 128
N_EXPERTS = 64
TOP_K = 8
EXPERT_FF = 256
ROUTED_SCALE = 2.5
LN_EPS = 1e-5
NORM_EPS = 1e-6

LANES = 128
S5_T = 8
DN_TB = 256
VMEM_LIMIT = 56 * 1024 * 1024
NEG = -1e30


def _params(*sem):
    return pltpu.CompilerParams(dimension_semantics=sem, vmem_limit_bytes=VMEM_LIMIT)


def _const_spec(shape):
    nd = len(shape)
    return pl.BlockSpec(shape, lambda *_: (0,) * nd)


class _Layered(NamedTuple):
    stack: jax.Array
    layer: int

    @property
    def shape(self):
        return self.stack.shape[1:]


def _wspec(t):
    if isinstance(t, _Layered):
        return pl.BlockSpec((None,) + t.shape, lambda *_: (t.layer,) + (0,) * len(t.shape))
    return _const_spec(t.shape)


def _warr(t):
    return t.stack if isinstance(t, _Layered) else t


def _dot(a, b):
    return jnp.dot(a, b, preferred_element_type=F32)


def _dot_nt(a, b, precision=None):
    return lax.dot_general(a, b, (((1,), (1,)), ((), ())), preferred_element_type=F32, precision=precision)


def _dot_tn(a, b):
    return lax.dot_general(a, b, (((0,), (0,)), ((), ())), preferred_element_type=F32)


def _split(a):
    hi = a.astype(BF16)
    return hi, (a - hi.astype(F32)).astype(BF16)


def _split3(a):
    hi = a.astype(BF16)
    r = a - hi.astype(F32)
    mid = r.astype(BF16)
    return hi, mid, (r - mid.astype(F32)).astype(BF16)


def _mm3(a, b):
    a_hi, a_lo = _split(a)
    b_hi, b_lo = _split(b)
    return _dot(jnp.concatenate([a_hi, a_hi, a_lo], axis=1), jnp.concatenate([b_hi, b_lo, b_hi], axis=0))


def _silu(x):
    return x * jax.nn.sigmoid(x)


def _layer_norm(r, g, b):
    mu = jnp.mean(r, axis=-1, keepdims=True)
    c = r - mu
    var = jnp.mean(c * c, axis=-1, keepdims=True)
    return c * lax.rsqrt(var + LN_EPS) * g + b


def _proj_body(x_ref, wdn, wba, ws5, wmq, wk, wv, wg, odn, oba, os5, omq, ok, ov, og, *, q_scale):
    xb = x_ref[...].astype(BF16)

    def mm(w_ref, o_ref, scale=None):
        n = w_ref.shape[1]
        step = min(n, 512)
        for c in range(0, n, step):
            r = _dot(xb, w_ref[:, c:c + step])
            if scale is not None:
                r = r * scale
            o_ref[:, c:c + step] = r.astype(o_ref.dtype)

    mm(wdn, odn)
    mm(wba, oba)
    mm(ws5, os5)
    mm(wmq, omq, q_scale)
    mm(wk, ok)
    mm(wv, ov)
    mm(wg, og)


def _proj(x, w, tm, dn_dtype=BF16):
    m, d = x.shape
    outs = (("dn", dn_dtype), ("ba", F32), ("s5", F32), ("mq", BF16), ("k", F32), ("v", F32), ("g", BF16))
    ws = [w["w_" + k] for k, _ in outs]
    return pl.pallas_call(
        functools.partial(_proj_body, q_scale=MB_HD ** -0.5),
        grid=(m // tm,),
        in_specs=[pl.BlockSpec((tm, d), lambda i: (i, 0))] + [_wspec(wi) for wi in ws],
        out_specs=[pl.BlockSpec((tm, wi.shape[1]), lambda i: (i, 0)) for wi in ws],
        out_shape=[jax.ShapeDtypeStruct((m, wi.shape[1]), dt) for wi, (_, dt) in zip(ws, outs)],
        compiler_params=_params("parallel"),
        name="proj",
    )(x, *map(_warr, ws))


def _dn_body(pdn_ref, pba_ref, cw_ref, ab_ref, nw_ref, o_ref, s_out_ref, buf, s_scr):
    tb = DN_TB
    t = pl.program_id(1)

    @pl.when(t == 0)
    def _():
        buf[0:8, :] = jnp.zeros((8, buf.shape[1]), F32)
        s_scr[...] = jnp.zeros(s_scr.shape, F32)

    buf[8:8 + tb, :] = pdn_ref[0, :, 0:3 * DN_WIDTH].astype(F32)

    def conv(c0):
        cs = slice(c0, c0 + DN_DK)
        y = cw_ref[3:4, cs] * buf[8:8 + tb, cs]
        for j in range(1, DN_CONV):
            y = y + cw_ref[3 - j:4 - j, cs] * buf[8 - j:8 - j + tb, cs]
        return _silu(y)

    def l2norm(x):
        return x * lax.rsqrt(jnp.sum(x * x, axis=-1, keepdims=True) + NORM_EPS)

    pba = pba_ref[0]
    beta_all = jax.nn.sigmoid(pba)
    g_all = -jnp.exp(ab_ref[0:1, :]) * jax.nn.softplus(pba + ab_ref[1:2, :])

    row = lax.broadcasted_iota(jnp.int32, (tb, tb), 0)
    col = lax.broadcasted_iota(jnp.int32, (tb, tb), 1)
    same64 = (row >> 6) == (col >> 6)
    same32 = (row >> 5) == (col >> 5)
    same16 = (row >> 4) == (col >> 4)
    incl = jnp.logical_and(same64, row >= col)
    strict = jnp.logical_and(same64, row > col)
    g_hi, g_mid, g_lo = _split3(g_all)
    ones_incl = jnp.where(incl, 1.0, 0.0).astype(BF16)
    gc = _dot(jnp.concatenate([ones_incl] * 3, axis=1), jnp.concatenate([g_hi, g_mid, g_lo], axis=0))
    gl = jnp.concatenate(
        [jnp.broadcast_to(gc[(c + 1) * DN_CHUNK - 1:(c + 1) * DN_CHUNK, :], (DN_CHUNK, LANES))
         for c in range(tb // DN_CHUNK)], axis=0)
    gct = gc.T
    eye = jnp.where(row == col, 1.0, 0.0)

    heads = range(DN_HEADS)
    q = [l2norm(conv(h * DN_DK)) * DN_DK ** -0.5 for h in heads]
    k = [l2norm(conv(DN_WIDTH + h * DN_DK)) for h in heads]
    v = [conv(2 * DN_WIDTH + h * DN_DK) for h in heads]
    beta = [beta_all[:, h:h + 1] for h in heads]
    gcol = [gc[:, DN_HEADS + h:DN_HEADS + h + 1] for h in heads]
    glast = [gl[:, DN_HEADS + h:DN_HEADS + h + 1] for h in heads]
    dec = [jnp.exp(jnp.where(incl, gcol[h] - gct[DN_HEADS + h:DN_HEADS + h + 1, :], NEG)) for h in heads]
    kb = [k[h] * beta[h] for h in heads]
    k16 = [k[h].astype(BF16) for h in heads]
    a = [jnp.where(strict, _dot_nt(kb[h].astype(BF16), k16[h]) * dec[h], 0.0) for h in heads]
    qk = [_dot_nt(q[h].astype(BF16), k16[h]) * dec[h] for h in heads]

    n1 = [jnp.where(same16, -a[h], 0.0) for h in heads]
    n2 = [_mm3(n1[h], n1[h]) for h in heads]
    n4 = [_mm3(n2[h], n2[h]) for h in heads]
    n8 = [_mm3(n4[h], n4[h]) for h in heads]
    p = [eye + n1[h] for h in heads]
    p = [p[h] + _mm3(p[h], n2[h]) for h in heads]
    p = [p[h] + _mm3(p[h], n4[h]) for h in heads]
    p = [p[h] + _mm3(p[h], n8[h]) for h in heads]
    off16 = jnp.logical_and(same32, jnp.logical_not(same16))
    t1 = [_mm3(jnp.where(off16, a[h], 0.0), p[h]) for h in heads]
    p = [p[h] - _mm3(p[h], t1[h]) for h in heads]
    t2 = [_mm3(jnp.where(same32, 0.0, a[h]), p[h]) for h in heads]
    p = [p[h] - _mm3(p[h], t2[h]) for h in heads]

    eg = [jnp.exp(gcol[h]) for h in heads]
    sol = [_mm3(p[h], jnp.concatenate([v[h] * beta[h], kb[h] * eg[h]], axis=1)) for h in heads]
    qd = [q[h] * eg[h] for h in heads]
    kd = [k[h] * jnp.exp(glast[h] - gcol[h]) for h in heads]

    sh = [s_scr[h] for h in heads]
    for c in range(tb // DN_CHUNK):
        r = slice(c * DN_CHUNK, (c + 1) * DN_CHUNK)
        for h in heads:
            wq = jnp.concatenate([sol[h][r, DN_DV:], qd[h][r]], axis=0).astype(BF16)
            ws_ = _dot(wq, sh[h].astype(BF16))
            vn16 = (sol[h][r, :DN_DV] - ws_[:DN_CHUNK]).astype(BF16)
            o = ws_[DN_CHUNK:] + _dot(qk[h][r, r].astype(BF16), vn16)
            sh[h] = (sh[h] * jnp.exp(glast[h][c * DN_CHUNK:c * DN_CHUNK + 1, :])
                     + _dot_tn(kd[h][r].astype(BF16), vn16))
            z = pdn_ref[0, r, 3 * DN_WIDTH + h * DN_DV:3 * DN_WIDTH + (h + 1) * DN_DV].astype(F32)
            o = o * lax.rsqrt(jnp.mean(o * o, axis=-1, keepdims=True) + NORM_EPS) * nw_ref[...] * _silu(z)
            o_ref[0, r, h * DN_DV:(h + 1) * DN_DV] = o.astype(o_ref.dtype)
    for h in heads:
        s_scr[h] = sh[h]

    buf[0:8, :] = buf[tb:tb + 8, :]

    @pl.when(t == pl.num_programs(1) - 1)
    def _():
        s_out_ref[0] = s_scr[...]


def _dn_gate_rows(a_log, dt_bias):
    ab = jnp.zeros((a_log.shape[0], 2, LANES), F32)
    ab = ab.at[:, 0, DN_HEADS:2 * DN_HEADS].set(a_log)
    return ab.at[:, 1, DN_HEADS:2 * DN_HEADS].set(dt_bias)


def _deltanet_prompt(pdn, pba, conv_w, ab, norm_w):
    n, l, _ = pdn.shape
    assert l % DN_TB == 0
    return pl.pallas_call(
        _dn_body,
        grid=(n, l // DN_TB),
        in_specs=[
            pl.BlockSpec((1, DN_TB, 4 * DN_WIDTH), lambda i, t: (i, t, 0)),
            pl.BlockSpec((1, DN_TB, LANES), lambda i, t: (i, t, 0)),
            _wspec(conv_w), _wspec(ab), _wspec(norm_w),
        ],
        out_specs=[
            pl.BlockSpec((1, DN_TB, DN_WIDTH), lambda i, t: (i, t, 0)),
            pl.BlockSpec((1, DN_HEADS, DN_DK, DN_DV), lambda i, t: (i, 0, 0, 0)),
        ],
        out_shape=[
            jax.ShapeDtypeStruct((n, l, DN_WIDTH), BF16),
            jax.ShapeDtypeStruct((n, DN_HEADS, DN_DK, DN_DV), F32),
        ],
        scratch_shapes=[pltpu.VMEM((DN_TB + 8, 3 * DN_WIDTH), F32), pltpu.VMEM((DN_HEADS, DN_DK, DN_DV), F32)],
        compiler_params=_params("parallel", "arbitrary"),
        name="deltanet_prompt",
    )(pdn, pba, _warr(conv_w), _warr(ab), _warr(norm_w))


def _dn_step_body(pdn_ref, pba_ref, cbuf_ref, s0_ref, cw_ref, ab_ref, nw_ref, o_ref, s_out_ref):
    x = pdn_ref[0, :, 0:3 * DN_WIDTH].astype(F32)
    y = cw_ref[3:4, :] * x
    for j in range(1, DN_CONV):
        y = y + cw_ref[3 - j:4 - j, :] * cbuf_ref[0, 3 - j:4 - j, :]
    qkv = _silu(y)
    pba = pba_ref[0]
    beta_all = jax.nn.sigmoid(pba)
    g_all = -jnp.exp(ab_ref[0:1, :]) * jax.nn.softplus(pba + ab_ref[1:2, :])

    def l2norm(t):
        return t * lax.rsqrt(jnp.sum(t * t, axis=-1, keepdims=True) + NORM_EPS)

    sub = lax.broadcasted_iota(jnp.int32, (8, DN_DK), 0)
    for h in range(DN_HEADS):
        q = l2norm(qkv[:, h * DN_DK:(h + 1) * DN_DK]) * DN_DK ** -0.5
        k = l2norm(qkv[:, DN_WIDTH + h * DN_DK:DN_WIDTH + (h + 1) * DN_DK])
        v = qkv[:, 2 * DN_WIDTH + h * DN_DV:2 * DN_WIDTH + (h + 1) * DN_DV]
        beta = beta_all[:, h:h + 1]
        eg = jnp.exp(g_all[:, DN_HEADS + h:DN_HEADS + h + 1])
        s0 = s0_ref[0, h]
        w = k * beta * eg
        lhs = jnp.where(sub == 0, w, jnp.where(sub == 1, q * eg, 0.0))
        ws_ = jnp.dot(lhs, s0, precision=lax.Precision.HIGHEST, preferred_element_type=F32)
        vn = v * beta - ws_[0:1, :]
        o = ws_[1:2, :] + jnp.sum(q * k, axis=-1, keepdims=True) * vn
        kt = jnp.broadcast_to(k, (DN_DK, DN_DK)).T
        s_out_ref[0, h] = s0 * eg + kt * vn
        z = pdn_ref[0, :, 3 * DN_WIDTH + h * DN_DV:3 * DN_WIDTH + (h + 1) * DN_DV].astype(F32)
        o = o * lax.rsqrt(jnp.mean(o * o, axis=-1, keepdims=True) + NORM_EPS) * nw_ref[...] * _silu(z)
        o_ref[0, :, h * DN_DV:(h + 1) * DN_DV] = o.astype(o_ref.dtype)


def _deltanet_step(pdn, pba, conv_buf, s0, conv_w, ab, norm_w):
    n = pdn.shape[0]
    return pl.pallas_call(
        _dn_step_body,
        grid=(n,),
        in_specs=[
            pl.BlockSpec((1, 1, 4 * DN_WIDTH), lambda i: (i, 0, 0)),
            pl.BlockSpec((1, 1, LANES), lambda i: (i, 0, 0)),
            pl.BlockSpec((1, DN_CONV - 1, 3 * DN_WIDTH), lambda i: (i, 0, 0)),
            pl.BlockSpec((1, DN_HEADS, DN_DK, DN_DV), lambda i: (i, 0, 0, 0)),
            _wspec(conv_w), _wspec(ab), _wspec(norm_w),
        ],
        out_specs=[
            pl.BlockSpec((1, 1, DN_WIDTH), lambda i: (i, 0, 0)),
            pl.BlockSpec((1, DN_HEADS, DN_DK, DN_DV), lambda i: (i, 0, 0, 0)),
        ],
        out_shape=[
            jax.ShapeDtypeStruct((n, 1, DN_WIDTH), BF16),
            jax.ShapeDtypeStruct((n, DN_HEADS, DN_DK, DN_DV), F32),
        ],
        compiler_params=_params("parallel"),
        name="deltanet_step",
    )(pdn, pba, conv_buf, s0, _warr(conv_w), _warr(ab), _warr(norm_w))


def _s5_discretize(a_re, a_im, log_dt, b_re, b_im):
    dt = jnp.exp(log_dt)[:, None]
    mag = jnp.exp(a_re * dt)
    abr, abi = mag * jnp.cos(a_im * dt), mag * jnp.sin(a_im * dt)
    den = a_re * a_re + a_im * a_im
    fr = ((abr - 1.0) * a_re + abi * a_im) / den
    fi = (abi * a_re - (abr - 1.0) * a_im) / den
    bbr = fr[..., None] * b_re - fi[..., None] * b_im
    bbi = fr[..., None] * b_im + fi[..., None] * b_re
    return abr, abi, bbr, bbi


def _s5_powers(a_re, a_im, log_dt, taus):
    dt = jnp.exp(log_dt)[None, :, None]
    tau = jnp.asarray(taus, F32)[:, None, None]
    mag = jnp.exp(a_re[None] * dt * tau)
    ang = a_im[None] * dt * tau
    return mag * jnp.cos(ang), mag * jnp.sin(ang)


S5_GB = LANES // S5_GROUP
S5_NGB = S5_GROUPS // S5_GB
S5_K = S5_T * LANES
S5_HALF = S5_GB * S5_STATE
S5_ROWS = 256


def _s5_prompt_weights(a_re, a_im, log_dt, b_re, b_im, c_re, c_im):
    t = S5_T
    hp = lax.Precision.HIGHEST
    eye = jnp.eye(S5_GB, dtype=F32)
    _, _, bbr, bbi = _s5_discretize(a_re, a_im, log_dt, b_re, b_im)
    lr, li = _s5_powers(a_re, a_im, log_dt, list(range(t + 1)))
    lbr = lr[..., None] * bbr[None] - li[..., None] * bbi[None]
    lbi = lr[..., None] * bbi[None] + li[..., None] * bbr[None]
    kt = (jnp.einsum("ghp,tgpk->tghk", c_re, lbr, precision=hp)
          - jnp.einsum("ghp,tgpk->tghk", c_im, lbi, precision=hp))
    lag = np.arange(t)[None, :] - np.arange(t)[:, None]
    kblk = kt.transpose(0, 1, 3, 2).reshape(t + 1, S5_NGB, S5_GB, S5_GROUP, S5_GROUP)
    kblk = jnp.einsum("tBaph,am->tBapmh", kblk, eye).reshape(t + 1, S5_NGB, LANES, LANES)
    toep = jnp.where((lag >= 0)[:, :, None, None, None], kblk[np.clip(lag, 0, t)], 0.0)
    toep = toep.transpose(2, 0, 3, 1, 4).reshape(S5_NGB, S5_K, S5_K)
    sel = t - 1 - np.arange(t)
    wz = jnp.stack([lbr[sel], lbi[sel]])
    wz = wz.transpose(2, 1, 4, 0, 3).reshape(S5_NGB, S5_GB, t, S5_GROUP, 2, S5_STATE)
    wz = jnp.einsum("Baipcq,am->Biapcmq", wz, eye).reshape(S5_NGB, S5_K, 2 * S5_HALF)
    lr1, li1 = lr[1:], li[1:]
    wy = jnp.stack([c_re[None] * lr1[:, :, None, :] - c_im[None] * li1[:, :, None, :],
                    -(c_re[None] * li1[:, :, None, :] + c_im[None] * lr1[:, :, None, :])])
    wy = wy.transpose(2, 0, 4, 1, 3).reshape(S5_NGB, S5_GB, 2, S5_STATE, t, S5_GROUP)
    wy = jnp.einsum("Bacpjq,am->Bcapjmq", wy, eye).reshape(S5_NGB, 2 * S5_HALF, S5_K)
    n_steps = int(math.log2(S5_ROWS))
    sr, si = _s5_powers(a_re, a_im, log_dt, [t * (1 << i) for i in range(n_steps)])
    blocks = lambda v: v.reshape(n_steps, S5_NGB, S5_HALF).transpose(1, 0, 2)
    sc_r = jnp.concatenate([blocks(sr), blocks(sr)], axis=-1)
    sc_i = jnp.concatenate([-blocks(si), blocks(si)], axis=-1)
    return toep.astype(BF16), wz.astype(BF16), wy.astype(BF16), sc_r, sc_i


def _s5_body(u_ref, toep_ref, wz_ref, wy_ref, lr_ref, li_ref, y_ref, sfin_ref, carry_s):
    rb, b = pl.program_id(1), pl.program_id(2)
    lanes = [pl.ds(pl.multiple_of(t * S5_WIDTH + b * LANES, LANES), LANES) for t in range(S5_T)]
    u = jnp.concatenate([u_ref[:, lanes[t]] for t in range(S5_T)], axis=1).astype(BF16)

    @pl.when(rb == 0)
    def _():
        carry_s[b] = jnp.zeros((1, 2 * S5_HALF), F32)

    swap = lambda v: pltpu.roll(v, S5_HALF, 1)
    z = _dot(u, wz_ref[...])
    rowi = lax.broadcasted_iota(jnp.int32, z.shape, 0)
    carry = carry_s[b]
    s = z + jnp.where(rowi == 0, lr_ref[0:1, :] * carry + li_ref[0:1, :] * swap(carry), 0.0)
    for i in range(lr_ref.shape[0]):
        d = 1 << i
        sh = jnp.where(rowi >= d, pltpu.roll(s, d, 0), 0.0)
        s = s + lr_ref[i:i + 1, :] * sh + li_ref[i:i + 1, :] * swap(sh)
    last = s[z.shape[0] - 1:z.shape[0], :]
    sprev = jnp.where(rowi >= 1, pltpu.roll(s, 1, 0), carry)
    carry_s[b] = last
    sfin_ref[b] = last
    y = _dot(u, toep_ref[...]) + _dot(sprev.astype(BF16), wy_ref[...])
    for t in range(S5_T):
        y_ref[:, lanes[t]] = y[:, t * LANES:(t + 1) * LANES]


def _s5_prompt(u, wts):
    n, l, _ = u.shape
    nc = l // S5_T
    assert nc % S5_ROWS == 0
    width = S5_T * S5_WIDTH
    wspec = lambda t: pl.BlockSpec((None, None) + t.shape[1:], lambda i, r, b: (t.layer, b, 0, 0))
    y, sfin = pl.pallas_call(
        _s5_body,
        grid=(n, nc // S5_ROWS, S5_NGB),
        in_specs=[pl.BlockSpec((None, S5_ROWS, width), lambda i, r, b: (i, r, 0)),
                  *map(wspec, wts)],
        out_specs=[pl.BlockSpec((None, S5_ROWS, width), lambda i, r, b: (i, r, 0)),
                   pl.BlockSpec((None, S5_NGB, 1, 2 * S5_HALF), lambda i, r, b: (i, 0, 0, 0))],
        out_shape=[jax.ShapeDtypeStruct((n, nc, width), F32),
                   jax.ShapeDtypeStruct((n, S5_NGB, 1, 2 * S5_HALF), F32)],
        scratch_shapes=[pltpu.VMEM((S5_NGB, 1, 2 * S5_HALF), F32)],
        compiler_params=_params("parallel", "arbitrary", "arbitrary"),
        name="s5_prompt",
    )(u.reshape(n, nc, width), *map(_warr, wts))
    sfin = sfin.reshape(n, S5_NGB, 2, S5_GB, S5_STATE).transpose(2, 0, 1, 3, 4).reshape(2, n, S5_GROUPS, S5_STATE)
    return y.reshape(n, l, S5_WIDTH), sfin[0], sfin[1]


def _s5_step_weights(a_re, a_im, log_dt, b_re, b_im, c_re, c_im):
    abr, abi, bbr, bbi = _s5_discretize(a_re, a_im, log_dt, b_re, b_im)
    eye = jnp.eye(S5_GROUPS, dtype=F32)
    gp = S5_GROUPS * S5_STATE
    wb = lambda b: jnp.einsum("gph,gk->ghkp", b, eye).reshape(S5_WIDTH, gp).astype(BF16)
    wc = lambda c: jnp.einsum("ghp,gk->gpkh", c, eye).reshape(gp, S5_WIDTH).astype(BF16)
    return (abr.reshape(1, gp), abi.reshape(1, gp), wb(bbr), wb(bbi), wc(c_re), wc(c_im))


def _s5_step_body(u_ref, s0r_ref, s0i_ref, abr_ref, abi_ref, wbr, wbi, wcr, wci, y_ref, sr_ref, si_ref):
    u = u_ref[...].astype(BF16)
    abr, abi = abr_ref[...], abi_ref[...]
    s0r, s0i = s0r_ref[...], s0i_ref[...]
    sr = abr * s0r - abi * s0i + _dot(u, wbr[...])
    si = abr * s0i + abi * s0r + _dot(u, wbi[...])
    sr_ref[...] = sr
    si_ref[...] = si
    y_ref[...] = _dot(sr.astype(BF16), wcr[...]) - _dot(si.astype(BF16), wci[...])


def _s5_step(u, s0_re, s0_im, wts):
    n = u.shape[0]
    gp = S5_GROUPS * S5_STATE
    args = (u, s0_re.reshape(n, gp), s0_im.reshape(n, gp)) + tuple(wts)
    y, sr, si = pl.pallas_call(
        _s5_step_body,
        grid=(1,),
        in_specs=[_wspec(a) for a in args],
        out_specs=[_const_spec((n, S5_WIDTH)), _const_spec((n, gp)), _const_spec((n, gp))],
        out_shape=[jax.ShapeDtypeStruct((n, S5_WIDTH), F32), jax.ShapeDtypeStruct((n, gp), F32),
                   jax.ShapeDtypeStruct((n, gp), F32)],
        compiler_params=_params("arbitrary"),
        name="s5_step",
    )(*map(_warr, args))
    return y, sr.reshape(n, S5_GROUPS, S5_STATE), si.reshape(n, S5_GROUPS, S5_STATE)


def _top_mask(gate, pos, count, axis=-1):
    sel = jnp.zeros(gate.shape, F32)
    for _ in range(count):
        m = jnp.max(gate, axis=axis, keepdims=True)
        idx = jnp.min(jnp.where(gate == m, pos, 2 ** 30), axis=axis, keepdims=True)
        hit = jnp.logical_and(pos == idx, m > -jnp.inf)
        sel = jnp.where(hit, 1.0, sel)
        gate = jnp.where(pos == idx, -jnp.inf, gate)
    return sel


def _alibi_slopes():
    s = np.exp2(-8.0 * np.arange(1, MB_HEADS + 1, dtype=np.float32) / MB_HEADS).astype(np.float32)
    return jnp.asarray(np.broadcast_to(s[:, None, None], (MB_HEADS, 1, LANES)))


MB_SLOTS = 16
MB_V_ROWS = MB_HD + 16
MB_GROUP = 2
MB_HEADS_PER_STEP = 4


def _moba_body(q_ref, k_ref, v_ref, slope_ref, o_ref, kaug_s, vaug_s, means_s, *, nb):
    blk = MB_BLOCK
    qi = pl.program_id(2)
    heads = range(MB_HEADS_PER_STEP)
    cols = [slice(h * MB_HD, (h + 1) * MB_HD) for h in heads]
    slope = [slope_ref[h, :, 0:1] for h in heads]

    @pl.when(qi == 0)
    def _():
        lane = lax.broadcasted_iota(jnp.int32, (blk, MB_HD), 1)
        off = lax.broadcasted_iota(jnp.int32, (blk, MB_HD), 0).astype(F32)
        means_s[...] = jnp.zeros(means_s.shape, F32)
        vaug_s[:, :, MB_HD:, :] = jnp.ones((len(heads), nb, MB_V_ROWS - MB_HD, blk), BF16)
        for b in range(nb):
            for h in heads:
                kb = k_ref[0, b * blk:(b + 1) * blk, cols[h]]
                means_s[h, b:b + 1, :] = jnp.mean(kb, axis=0, keepdims=True)
                feat = jnp.where(lane == b, 1.0, 0.0)
                feat = jnp.where(lane == MB_SLOTS, slope[h] * off, feat)
                feat = jnp.where(lane == MB_SLOTS + 1, slope[h] * float(b * blk), feat)
                feat = jnp.where(lane == MB_SLOTS + 2, 1.0, feat)
                kaug_s[h, b] = jnp.concatenate([kb.astype(BF16), feat.astype(BF16)], axis=1)
                vaug_s[h, b, 0:MB_HD, :] = v_ref[0, b * blk:(b + 1) * blk, cols[h]].T.astype(BF16)

    slot = lax.broadcasted_iota(jnp.int32, (MB_SLOTS, blk), 0)
    pad = jnp.zeros((MB_HD - 2 * MB_SLOTS, blk), BF16)
    qt = [q_ref[0, :, cols[h]].astype(F32).T for h in heads]
    gate = [jnp.dot(means_s[h], qt[h], precision=lax.Precision.HIGHEST, preferred_element_type=F32) for h in heads]
    sel = [_top_mask(jnp.where(slot < qi, gate[h], -jnp.inf), slot, MB_TOPK, axis=0) for h in heads]
    bias_rows = [jnp.where(slot < 2, 1.0, jnp.where(slot == 2, -slope[h] * (qi * blk).astype(F32), 0.0)).astype(BF16)
                 for h in heads]
    qt16 = [qt[h].astype(BF16) for h in heads]
    q_past = [jnp.concatenate([qt16[h], jnp.where(sel[h] > 0.0, 0.0, NEG).astype(BF16), bias_rows[h], pad], axis=0)
              for h in heads]
    q_own = [jnp.concatenate([qt16[h], jnp.zeros((MB_SLOTS, blk), BF16), bias_rows[h], pad], axis=0) for h in heads]

    key = lax.broadcasted_iota(jnp.int32, (blk, blk), 0)
    qry = lax.broadcasted_iota(jnp.int32, (blk, blk), 1)
    s = [jnp.where(key <= qry, _dot(kaug_s[h, qi], q_own[h]), NEG) for h in heads]
    m = [jnp.max(s[h], axis=0, keepdims=True) for h in heads]
    acc = [_dot(vaug_s[h, qi], jnp.exp(s[h] - m[h]).astype(BF16)) for h in heads]

    grp = MB_GROUP

    def body(i, carry):
        m_old, acc = carry
        s = [[_dot(kaug_s[h, grp * i + g], q_past[h]) for g in range(grp)] for h in heads]
        m_new = list(m_old)
        for g in range(grp):
            m_new = [jnp.maximum(m_new[h], jnp.max(s[h][g], axis=0, keepdims=True)) for h in heads]
        p = [[jnp.exp(s[h][g] - m_new[h]).astype(BF16) for g in range(grp)] for h in heads]
        acc = [jnp.exp(m_old[h] - m_new[h]) * acc[h] for h in heads]
        for g in range(grp):
            acc = [acc[h] + _dot(vaug_s[h, grp * i + g], p[h][g]) for h in heads]
        return tuple(m_new), tuple(acc)

    m, acc = lax.fori_loop(0, (qi + grp - 1) // grp, body, (tuple(m), tuple(acc)))
    for h in heads:
        o_ref[0, :, cols[h]] = (acc[h][0:MB_HD] / acc[h][MB_HD:MB_HD + 1]).T.astype(o_ref.dtype)


def _moba_prompt(q, k, v):
    n, l, _ = q.shape
    nb = l // MB_BLOCK
    hps = MB_HEADS_PER_STEP
    width = hps * MB_HD
    assert l % MB_BLOCK == 0 and nb <= MB_SLOTS and nb % MB_GROUP == 0 and MB_HEADS % hps == 0
    kv_spec = pl.BlockSpec((1, l, width), lambda i, h, t: (i, 0, h))
    return pl.pallas_call(
        functools.partial(_moba_body, nb=nb),
        grid=(n, MB_HEADS // hps, nb),
        in_specs=[
            pl.BlockSpec((1, MB_BLOCK, width), lambda i, h, t: (i, t, h)),
            kv_spec, kv_spec,
            pl.BlockSpec((hps, 1, LANES), lambda i, h, t: (h, 0, 0)),
        ],
        out_specs=pl.BlockSpec((1, MB_BLOCK, width), lambda i, h, t: (i, t, h)),
        out_shape=jax.ShapeDtypeStruct((n, l, MB_WIDTH), BF16),
        scratch_shapes=[pltpu.VMEM((hps, nb, MB_BLOCK, 2 * MB_HD), BF16),
                        pltpu.VMEM((hps, nb, MB_V_ROWS, MB_BLOCK), BF16),
                        pltpu.VMEM((hps, MB_SLOTS, MB_HD), F32)],
        compiler_params=_params("parallel", "parallel", "arbitrary"),
        name="moba_prompt",
    )(q, k, v, _alibi_slopes())


PAGES_PER_STEP = 32
PAGES_PER_BLOCK = MB_BLOCK // PAGE_SIZE


def _page_sum_body(pt_ref, *refs):
    del pt_ref
    o_ref = refs[-1]
    for b in range(PAGES_PER_STEP // PAGES_PER_BLOCK):
        acc = jnp.sum(refs[PAGES_PER_BLOCK * b][...], axis=0)
        for j in range(1, PAGES_PER_BLOCK):
            acc = acc + jnp.sum(refs[PAGES_PER_BLOCK * b + j][...], axis=0)
        o_ref[b] = acc


def _block_key_sums(pool, page_table, layer):
    n, n_pages = page_table.shape
    assert n_pages % PAGES_PER_STEP == 0
    tail = pool.shape[2:]

    def page_spec(j):
        return pl.BlockSpec((None, None) + tail, lambda i, s, pt: (layer, pt[i, s * PAGES_PER_STEP + j], 0, 0, 0))

    blocks_per_step = PAGES_PER_STEP // PAGES_PER_BLOCK
    return pl.pallas_call(
        _page_sum_body,
        grid_spec=pltpu.PrefetchScalarGridSpec(
            num_scalar_prefetch=1,
            grid=(n, n_pages // PAGES_PER_STEP),
            in_specs=[page_spec(j) for j in range(PAGES_PER_STEP)],
            out_specs=pl.BlockSpec((None, blocks_per_step) + tail[1:], lambda i, s, pt: (i, s, 0, 0)),
        ),
        out_shape=jax.ShapeDtypeStruct((n, n_pages // PAGES_PER_BLOCK) + tail[1:], F32),
        compiler_params=_params("parallel", "arbitrary"),
        name="moba_page_sums",
    )(page_table, *([pool] * PAGES_PER_STEP))


def _moba_pick_body(q_ref, sums_ref, sel_ref):
    nbk = sums_ref.shape[0]
    out = jnp.zeros(sel_ref.shape, jnp.int32)
    osub = lax.broadcasted_iota(jnp.int32, out.shape, 0)
    olane = lax.broadcasted_iota(jnp.int32, out.shape, 1)
    bidx = lax.broadcasted_iota(jnp.int32, (nbk, 1), 0)
    for h in range(MB_HEADS):
        means = sums_ref[:, h, :] / float(MB_BLOCK)
        q = q_ref[:, h * MB_HD:(h + 1) * MB_HD].astype(F32)
        gate = jnp.sum(means * q, axis=-1, keepdims=True)
        for rnk in range(MB_TOPK):
            m = jnp.max(gate, axis=0, keepdims=True)
            idx = jnp.min(jnp.where(gate == m, bidx, 2 ** 30), axis=0, keepdims=True)
            out = jnp.where(jnp.logical_and(osub == h, olane == rnk), idx, out)
            gate = jnp.where(bidx == idx, -jnp.inf, gate)
    sel_ref[...] = out


def _moba_pick(q, sums):
    n = q.shape[0]
    return pl.pallas_call(
        _moba_pick_body,
        grid=(n,),
        in_specs=[pl.BlockSpec((None, 1, MB_WIDTH), lambda i: (i, 0, 0)),
                  pl.BlockSpec((None,) + sums.shape[1:], lambda i: (i, 0, 0, 0))],
        out_specs=pl.BlockSpec((None, 8, LANES), lambda i: (i, 0, 0)),
        out_shape=jax.ShapeDtypeStruct((n, 8, LANES), jnp.int32),
        compiler_params=_params("parallel"),
        name="moba_pick",
    )(q, sums)


def _moba_step_body(sel_ref, pt_ref, q_ref, kn_ref, vn_ref, slope_ref, *refs, past):
    del pt_ref
    pages, o_ref, m_s, l_s, acc_s = refs[:4 * MB_HEADS], refs[4 * MB_HEADS], *refs[4 * MB_HEADS + 1:]
    i, s = pl.program_id(0), pl.program_id(1)
    sub = lax.broadcasted_iota(jnp.int32, (PAGE_SIZE, 1), 0)
    for h in range(MB_HEADS):
        q = q_ref[0, h].astype(F32)
        slope = slope_ref[h, :, 0:1]

        @pl.when(s == 0)
        def _():
            m_s[h] = jnp.sum(q * kn_ref[0, h], axis=-1, keepdims=True)
            l_s[h] = jnp.ones((1, 1), F32)
            acc_s[h] = vn_ref[0, h]

        blk = sel_ref[(i * MB_HEADS + h) * MB_TOPK + s]
        k0, k1, v0, v1 = pages[4 * h:4 * h + 4]
        m_old = m_s[h]
        sc = []
        for half, kr in enumerate((k0, k1)):
            kpos = blk * MB_BLOCK + half * PAGE_SIZE + sub
            sc.append(jnp.sum(kr[:, h, :] * q, axis=-1, keepdims=True) - slope * (past - kpos).astype(F32))
        m_new = jnp.maximum(m_old, jnp.maximum(jnp.max(sc[0], axis=0, keepdims=True),
                                               jnp.max(sc[1], axis=0, keepdims=True)))
        alpha = jnp.exp(m_old - m_new)
        p0 = jnp.exp(sc[0] - m_new)
        p1 = jnp.exp(sc[1] - m_new)
        l_new = alpha * l_s[h] + jnp.sum(p0, axis=0, keepdims=True) + jnp.sum(p1, axis=0, keepdims=True)
        acc_new = (alpha * acc_s[h] + jnp.sum(p0 * v0[:, h, :], axis=0, keepdims=True)
                   + jnp.sum(p1 * v1[:, h, :], axis=0, keepdims=True))
        m_s[h] = m_new
        l_s[h] = l_new
        acc_s[h] = acc_new

        @pl.when(s == MB_TOPK - 1)
        def _():
            o_ref[0, h] = (acc_new / l_new).astype(o_ref.dtype)


def _moba_step(q, k_new, v_new, sel, page_table, k_pool, v_pool, layer):
    n = q.shape[0]
    past = page_table.shape[1] * PAGE_SIZE
    assert PAGES_PER_BLOCK == 2 and past // MB_BLOCK >= MB_TOPK
    tail = k_pool.shape[2:]

    def page_spec(h, half):
        def imap(i, s, sel_r, pt_r):
            blk = sel_r[(i * MB_HEADS + h) * MB_TOPK + s]
            return (layer, pt_r[i, blk * PAGES_PER_BLOCK + half], 0, 0, 0)
        return pl.BlockSpec((None, None) + tail, imap)

    page_specs, page_args = [], []
    for h in range(MB_HEADS):
        for pool in (k_pool, v_pool):
            for half in range(PAGES_PER_BLOCK):
                page_specs.append(page_spec(h, half))
                page_args.append(pool)
    tok = pl.BlockSpec((1, MB_HEADS, 1, MB_HD), lambda i, s, *_: (i, 0, 0, 0))
    return pl.pallas_call(
        functools.partial(_moba_step_body, past=past),
        grid_spec=pltpu.PrefetchScalarGridSpec(
            num_scalar_prefetch=2,
            grid=(n, MB_TOPK),
            in_specs=[tok, tok, tok, _const_spec((MB_HEADS, 1, LANES))] + page_specs,
            out_specs=tok,
            scratch_shapes=[pltpu.VMEM((MB_HEADS, 1, 1), F32), pltpu.VMEM((MB_HEADS, 1, 1), F32),
                            pltpu.VMEM((MB_HEADS, 1, MB_HD), F32)],
        ),
        out_shape=jax.ShapeDtypeStruct((n, MB_HEADS, 1, MB_HD), BF16),
        compiler_params=_params("parallel", "arbitrary"),
        name="moba_step",
    )(sel, page_table, q, k_new, v_new, _alibi_slopes(), *page_args)


def _merge_body(x_ref, a_ref, y5_ref, u_ref, c_ref, g_ref, d_ref, gluw, glub, wa, wb, wc, wo, lng, lnb, h_ref,
                *, alpha):
    y = jax.nn.gelu(y5_ref[...] + d_ref[...] * u_ref[...])
    b = y * jax.nn.sigmoid(_dot(y.astype(BF16), gluw[...]) + glub[...])
    dm = x_ref.shape[1]
    gate = lambda j: jax.nn.sigmoid(g_ref[:, j * dm:(j + 1) * dm].astype(F32))
    merged = (gate(0) * _dot(a_ref[...], wa[...]) + gate(1) * _dot(b.astype(BF16), wb[...])
              + gate(2) * _dot(c_ref[...], wc[...]))
    r = alpha * x_ref[...] + _dot(merged.astype(BF16), wo[...])
    h_ref[...] = _layer_norm(r, lng[...], lnb[...])


def _merge(x, a, y5, u, c, g, w, alpha, tm):
    m, dm = x.shape
    acts = (x, a, y5, u, c, g)
    consts = (w["s5_d"], w["glu_w"], w["glu_b"], w["wa"], w["wb"], w["wc"], w["wo"], w["ln1_g"], w["ln1_b"])
    return pl.pallas_call(
        functools.partial(_merge_body, alpha=alpha),
        grid=(m // tm,),
        in_specs=[pl.BlockSpec((tm, t.shape[1]), lambda i: (i, 0)) for t in acts]
        + [_wspec(t) for t in consts],
        out_specs=pl.BlockSpec((tm, dm), lambda i: (i, 0)),
        out_shape=jax.ShapeDtypeStruct((m, dm), F32),
        compiler_params=_params("parallel"),
        name="merge",
    )(*acts, *map(_warr, consts))


def _moe_body(h_ref, rw, rb, wg, wu, wd, sgu, sd, lng, lnb, y_ref, xb_s, gate_s, acc_s, *, alpha):
    e = pl.program_id(1)
    lane = lax.broadcasted_iota(jnp.int32, gate_s.shape, 1)

    @pl.when(e == 0)
    def _():
        xb = h_ref[...].astype(BF16)
        xb_s[...] = xb
        scores = jax.nn.sigmoid(_dot(xb, rw[...]))
        ranked = jnp.where(lane < N_EXPERTS, scores + rb[...], -jnp.inf)
        picked = _top_mask(ranked, lane, TOP_K) * scores
        gate_s[...] = picked / jnp.sum(picked, axis=-1, keepdims=True) * ROUTED_SCALE
        hs = _dot(xb, sgu[...])
        ff = sd.shape[0]
        acc_s[...] = _dot((_silu(hs[:, :ff]) * hs[:, ff:]).astype(BF16), sd[...])

    xb = xb_s[...]
    gcol = jnp.sum(jnp.where(lane == e, gate_s[...], 0.0), axis=-1, keepdims=True)
    hm = _silu(_dot(xb, wg[...])) * _dot(xb, wu[...]) * gcol
    acc_s[...] += _dot(hm.astype(BF16), wd[...])

    @pl.when(e == pl.num_programs(1) - 1)
    def _():
        y_ref[...] = _layer_norm(alpha * h_ref[...] + acc_s[...], lng[...], lnb[...])


def _moe(h, w, alpha, tm):
    m, dm = h.shape
    consts_a = (w["router_w"], w["router_b"])
    consts_b = (w["sh_gu"], w["sh_d"], w["ln2_g"], w["ln2_b"])
    layer = w["layer"]
    return pl.pallas_call(
        functools.partial(_moe_body, alpha=alpha),
        grid=(m // tm, N_EXPERTS),
        in_specs=[pl.BlockSpec((tm, dm), lambda i, e: (i, 0))]
        + [_wspec(t) for t in consts_a]
        + [pl.BlockSpec((None, None, dm, EXPERT_FF), lambda i, e: (layer, e, 0, 0)),
           pl.BlockSpec((None, None, dm, EXPERT_FF), lambda i, e: (layer, e, 0, 0)),
           pl.BlockSpec((None, None, EXPERT_FF, dm), lambda i, e: (layer, e, 0, 0))]
        + [_wspec(t) for t in consts_b],
        out_specs=pl.BlockSpec((tm, dm), lambda i, e: (i, 0)),
        out_shape=jax.ShapeDtypeStruct((m, dm), F32),
        scratch_shapes=[pltpu.VMEM((tm, dm), BF16), pltpu.VMEM((tm, LANES), F32), pltpu.VMEM((tm, dm), F32)],
        compiler_params=_params("parallel", "arbitrary"),
        name="moe",
    )(h, *map(_warr, consts_a), w["exp_g"], w["exp_u"], w["exp_d"], *map(_warr, consts_b))


def _stacked_weights(p):
    w_in = p["w_in"]
    o = 0
    w = {}
    for name, size in (("dn", 4 * DN_WIDTH), ("ba", 2 * DN_HEADS), ("s5", S5_WIDTH), ("mq", MB_WIDTH),
                       ("k", MB_WIDTH), ("v", MB_WIDTH), ("g", 3 * w_in.shape[1])):
        w["w_" + name] = w_in[:, :, o:o + size].astype(BF16)
        o += size
    w["w_ba"] = jnp.pad(w["w_ba"], ((0, 0), (0, 0), (0, LANES - 2 * DN_HEADS)))
    row = lambda v: v[:, None, :]
    w["conv_w"] = p["dn_conv_w"]
    w["dn_ab"] = _dn_gate_rows(p["dn_a_log"], p["dn_dt_bias"])
    w["dn_norm_w"] = row(p["dn_norm_w"])
    s5 = tuple(p[k] for k in ("s5_a_re", "s5_a_im", "s5_log_dt", "s5_b_re", "s5_b_im", "s5_c_re", "s5_c_im"))
    w["s5_prompt"] = jax.vmap(_s5_prompt_weights)(*s5)
    w["s5_step"] = jax.vmap(_s5_step_weights)(*s5)
    w["s5_d"] = row(p["s5_d"])
    w["glu_w"] = p["s5_glu_w"].astype(BF16)
    w["glu_b"] = row(p["s5_glu_b"])
    w["wa"] = p["w_branch_a"].astype(BF16)
    w["wb"] = p["w_branch_b"].astype(BF16)
    w["wc"] = p["w_branch_c"].astype(BF16)
    w["wo"] = p["w_out"].astype(BF16)
    for k in ("ln1_g", "ln1_b", "ln2_g", "ln2_b"):
        w[k] = row(p[k])
    w["router_w"] = jnp.pad(p["router_w"].astype(BF16), ((0, 0), (0, 0), (0, LANES - N_EXPERTS)))
    w["router_b"] = jnp.pad(row(p["router_bias"]), ((0, 0), (0, 0), (0, LANES - N_EXPERTS)))
    w["sh_gu"] = jnp.concatenate([p["sh_w_gate"], p["sh_w_up"]], axis=-1).astype(BF16)
    w["sh_d"] = p["sh_w_down"].astype(BF16)
    w["exp_g"], w["exp_u"], w["exp_d"] = (p[k].astype(BF16) for k in ("exp_w_gate", "exp_w_up", "exp_w_down"))
    return w


def _layer_weights(l, stacked):
    view = lambda t: tuple(_Layered(a, l) for a in t) if isinstance(t, tuple) else _Layered(t, l)
    w = {k: (v if k.startswith("exp_") else view(v)) for k, v in stacked.items()}
    w["layer"] = l
    return w


def _prompt_layer(x, w, alpha, tm, tm_moe):
    n, l, dm = x.shape
    x2 = x.reshape(n * l, dm)
    pdn, pba, ps5, pmq, pk, pv, pg = _proj(x2, w, tm)
    a_out, s_fin = _deltanet_prompt(pdn.reshape(n, l, -1), pba.reshape(n, l, -1), w["conv_w"], w["dn_ab"],
                                    w["dn_norm_w"])
    y5, s5_re, s5_im = _s5_prompt(ps5.reshape(n, l, -1), w["s5_prompt"])
    c_out = _moba_prompt(pmq.reshape(n, l, -1), pk.reshape(n, l, -1), pv.reshape(n, l, -1))
    pk, pv = pk.reshape(n, l, MB_HEADS, MB_HD), pv.reshape(n, l, MB_HEADS, MB_HD)
    h = _merge(x2, a_out.reshape(n * l, -1), y5.reshape(n * l, -1), ps5, c_out.reshape(n * l, -1), pg, w, alpha, tm)
    y = _moe(h, w, alpha, tm_moe)
    conv_new = pdn.reshape(n, l, -1)[:, l - (DN_CONV - 1):, :3 * DN_WIDTH].astype(F32)
    return y.reshape(n, l, dm), (pk, pv, conv_new, s_fin, s5_re, s5_im)


def _sample_layer(x, w, alpha, conv_buf, s_dn, s5_re, s5_im, k_pool, v_pool, page_table, layer):
    n, l, dm = x.shape
    assert l == 1
    x2 = x.reshape(n, dm)
    pdn, pba, ps5, pmq, pk, pv, pg = _proj(x2, w, n, dn_dtype=F32)
    a_out, s_new = _deltanet_step(pdn.reshape(n, 1, -1), pba.reshape(n, 1, -1), conv_buf, s_dn, w["conv_w"],
                                  w["dn_ab"], w["dn_norm_w"])
    y5, s5_re_new, s5_im_new = _s5_step(ps5, s5_re, s5_im, w["s5_step"])
    sums = _block_key_sums(k_pool, page_table, layer)
    sel = _moba_pick(pmq.reshape(n, 1, -1), sums)
    sel = sel[:, :MB_HEADS, :MB_TOPK].reshape(-1)
    heads = lambda t: t.reshape(n, MB_HEADS, 1, MB_HD)
    c_out = _moba_step(heads(pmq), heads(pk), heads(pv), sel, page_table, k_pool, v_pool, layer)
    h = _merge(x2, a_out.reshape(n, -1), y5, ps5, c_out.reshape(n, -1), pg, w, alpha, n)
    y = _moe(h, w, alpha, n)
    conv_new = jnp.concatenate([conv_buf[:, 1:], pdn[:, None, :3 * DN_WIDTH].astype(F32)], axis=1)
    heads4 = lambda t: t.reshape(n, 1, MB_HEADS, MB_HD)
    return y.reshape(n, 1, dm), (heads4(pk), heads4(pv), conv_new, s_new, s5_re_new, s5_im_new)


def kernel(x_prompt, x_sample, cache_k, cache_v, page_table, state_dn_conv, state_dn, state_s5_re, state_s5_im, w_in, dn_conv_w, dn_a_log, dn_dt_bias, dn_norm_w, s5_a_re, s5_a_im, s5_log_dt, s5_b_re, s5_b_im, s5_c_re, s5_c_im, s5_d, s5_glu_w, s5_glu_b, w_branch_a, w_branch_b, w_branch_c, w_out, ln1_g, ln1_b, router_w, router_bias, exp_w_gate, exp_w_up, exp_w_down, sh_w_gate, sh_w_up, sh_w_down, ln2_g, ln2_b):
    p = dict(w_in=w_in, dn_conv_w=dn_conv_w, dn_a_log=dn_a_log, dn_dt_bias=dn_dt_bias, dn_norm_w=dn_norm_w,
             s5_a_re=s5_a_re, s5_a_im=s5_a_im, s5_log_dt=s5_log_dt, s5_b_re=s5_b_re, s5_b_im=s5_b_im,
             s5_c_re=s5_c_re, s5_c_im=s5_c_im, s5_d=s5_d, s5_glu_w=s5_glu_w, s5_glu_b=s5_glu_b,
             w_branch_a=w_branch_a, w_branch_b=w_branch_b, w_branch_c=w_branch_c, w_out=w_out,
             ln1_g=ln1_g, ln1_b=ln1_b, router_w=router_w, router_bias=router_bias, exp_w_gate=exp_w_gate,
             exp_w_up=exp_w_up, exp_w_down=exp_w_down, sh_w_gate=sh_w_gate, sh_w_up=sh_w_up,
             sh_w_down=sh_w_down, ln2_g=ln2_g, ln2_b=ln2_b)
    stacked = _stacked_weights(p)
    depth = w_in.shape[0]
    alpha = (2 * depth) ** 0.25
    seq = x_prompt.shape[1]
    tm = min(512, x_prompt.shape[0] * seq)
    tm_moe = min(1024, x_prompt.shape[0] * seq)
    hp, hs = x_prompt, x_sample
    p_st, s_st = [], []
    for l in range(depth):
        w = _layer_weights(l, stacked)
        hp, st = _prompt_layer(hp, w, alpha, tm, tm_moe)
        p_st.append(st)
        hs, st = _sample_layer(hs, w, alpha, state_dn_conv[l], state_dn[l], state_s5_re[l], state_s5_im[l],
                               cache_k, cache_v, page_table, l)
        s_st.append(st)
    p_out = [jnp.stack(t) for t in zip(*p_st)]
    s_out = [jnp.stack(t) for t in zip(*s_st)]
    return (hp, hs, *p_out, *s_out)
```

```python
import functools
import math
from typing import NamedTuple

import jax
import jax.numpy as jnp
import numpy as np
from jax import lax
from jax.experimental import pallas as pl
from jax.experimental.pallas import tpu as pltpu

F32 = jnp.float32
BF16 = jnp.bfloat16

DN_HEADS = 4
DN_DK = 128
DN_DV = 128
DN_WIDTH = DN_HEADS * DN_DK
DN_CONV = 4
DN_CHUNK = 64
S5_WIDTH = 512
S5_GROUP = 16
S5_GROUPS = S5_WIDTH // S5_GROUP
S5_STATE = 64
MB_HEADS = 4
MB_HD = 128
MB_WIDTH = MB_HEADS * MB_HD
MB_BLOCK = 256
MB_TOPK = 3
PAGE_SIZE = 128
N_EXPERTS = 64
TOP_K = 8
EXPERT_FF = 256
ROUTED_SCALE = 2.5
LN_EPS = 1e-5
NORM_EPS = 1e-6

LANES = 128
S5_T = 8
DN_TB = 256
VMEM_LIMIT = 56 * 1024 * 1024
NEG = -1e30


def _params(*sem):
    return pltpu.CompilerParams(dimension_semantics=sem, vmem_limit_bytes=VMEM_LIMIT)


def _const_spec(shape):
    nd = len(shape)
    return pl.BlockSpec(shape, lambda *_: (0,) * nd)


class _Layered(NamedTuple):
    stack: jax.Array
    layer: int

    @property
    def shape(self):
        return self.stack.shape[1:]


def _wspec(t):
    if isinstance(t, _Layered):
        return pl.BlockSpec((None,) + t.shape, lambda *_: (t.layer,) + (0,) * len(t.shape))
    return _const_spec(t.shape)


def _warr(t):
    return t.stack if isinstance(t, _Layered) else t


def _dot(a, b):
    return jnp.dot(a, b, preferred_element_type=F32)


def _dot_nt(a, b, precision=None):
    return lax.dot_general(a, b, (((1,), (1,)), ((), ())), preferred_element_type=F32, precision=precision)


def _dot_tn(a, b):
    return lax.dot_general(a, b, (((0,), (0,)), ((), ())), preferred_element_type=F32)


def _split(a):
    hi = a.astype(BF16)
    return hi, (a - hi.astype(F32)).astype(BF16)


def _split3(a):
    hi = a.astype(BF16)
    r = a - hi.astype(F32)
    mid = r.astype(BF16)
    return hi, mid, (r - mid.astype(F32)).astype(BF16)


def _mm3(a, b):
    a_hi, a_lo = _split(a)
    b_hi, b_lo = _split(b)
    return _dot(jnp.concatenate([a_hi, a_hi, a_lo], axis=1), jnp.concatenate([b_hi, b_lo, b_hi], axis=0))


def _mm2(a, b, full):
    a_hi, a_lo = _split(a)
    b_hi, b_lo = _split(b)
    if full == "a":
        return _dot(jnp.concatenate([a_hi, a_lo], axis=1), jnp.concatenate([b_hi, b_hi], axis=0))
    return _dot(jnp.concatenate([a_hi, a_hi], axis=1), jnp.concatenate([b_hi, b_lo], axis=0))


def _silu(x):
    return x * jax.nn.sigmoid(x)


def _layer_norm(r, g, b):
    mu = jnp.mean(r, axis=-1, keepdims=True)
    c = r - mu
    var = jnp.mean(c * c, axis=-1, keepdims=True)
    return c * lax.rsqrt(var + LN_EPS) * g + b


def _proj_body(x_ref, wdn, wba, ws5, wmq, wk, wv, wg, odn, oba, os5, omq, ok, ov, og, *, q_scale):
    xb = x_ref[...].astype(BF16)

    def mm(w_ref, o_ref, scale=None):
        n = w_ref.shape[1]
        step = min(n, 512)
        for c in range(0, n, step):
            r = _dot(xb, w_ref[:, c:c + step])
            if scale is not None:
                r = r * scale
            o_ref[:, c:c + step] = r.astype(o_ref.dtype)

    mm(wdn, odn)
    mm(wba, oba)
    mm(ws5, os5)
    mm(wmq, omq, q_scale)
    mm(wk, ok)
    mm(wv, ov)
    mm(wg, og)


def _proj(x, w, tm, dn_dtype=BF16):
    m, d = x.shape
    outs = (("dn", dn_dtype), ("ba", F32), ("s5", F32), ("mq", BF16), ("k", F32), ("v", F32), ("g", BF16))
    ws = [w["w_" + k] for k, _ in outs]
    return pl.pallas_call(
        functools.partial(_proj_body, q_scale=MB_HD ** -0.5),
        grid=(m // tm,),
        in_specs=[pl.BlockSpec((tm, d), lambda i: (i, 0))] + [_wspec(wi) for wi in ws],
        out_specs=[pl.BlockSpec((tm, wi.shape[1]), lambda i: (i, 0)) for wi in ws],
        out_shape=[jax.ShapeDtypeStruct((m, wi.shape[1]), dt) for wi, (_, dt) in zip(ws, outs)],
        compiler_params=_params("parallel"),
        name="proj",
    )(x, *map(_warr, ws))


def _dn_body(pdn_ref, pba_ref, cw_ref, ab_ref, nw_ref, o_ref, s_out_ref, buf, s_scr):
    tb = DN_TB
    t = pl.program_id(1)

    @pl.when(t == 0)
    def _():
        buf[0:8, :] = jnp.zeros((8, buf.shape[1]), F32)
        s_scr[...] = jnp.zeros(s_scr.shape, F32)

    buf[8:8 + tb, :] = pdn_ref[0, :, 0:3 * DN_WIDTH].astype(F32)

    def conv(c0):
        cs = slice(c0, c0 + DN_DK)
        y = cw_ref[3:4, cs] * buf[8:8 + tb, cs]
        for j in range(1, DN_CONV):
            y = y + cw_ref[3 - j:4 - j, cs] * buf[8 - j:8 - j + tb, cs]
        return _silu(y)

    def l2norm(x):
        return x * lax.rsqrt(jnp.sum(x * x, axis=-1, keepdims=True) + NORM_EPS)

    pba = pba_ref[0]
    beta_all = jax.nn.sigmoid(pba)
    g_all = -jnp.exp(ab_ref[0:1, :]) * jax.nn.softplus(pba + ab_ref[1:2, :])

    row = lax.broadcasted_iota(jnp.int32, (tb, tb), 0)
    col = lax.broadcasted_iota(jnp.int32, (tb, tb), 1)
    same64 = (row >> 6) == (col >> 6)
    same32 = (row >> 5) == (col >> 5)
    same16 = (row >> 4) == (col >> 4)
    incl = jnp.logical_and(same64, row >= col)
    strict = jnp.logical_and(same64, row > col)
    g_hi, g_mid, g_lo = _split3(g_all)
    ones_incl = jnp.where(incl, 1.0, 0.0).astype(BF16)
    gc = _dot(jnp.concatenate([ones_incl] * 3, axis=1), jnp.concatenate([g_hi, g_mid, g_lo], axis=0))
    gl = jnp.concatenate(
        [jnp.broadcast_to(gc[(c + 1) * DN_CHUNK - 1:(c + 1) * DN_CHUNK, :], (DN_CHUNK, LANES))
         for c in range(tb // DN_CHUNK)], axis=0)
    gct = gc.T
    eye = jnp.where(row == col, 1.0, 0.0)

    heads = range(DN_HEADS)
    q = [l2norm(conv(h * DN_DK)) * DN_DK ** -0.5 for h in heads]
    k = [l2norm(conv(DN_WIDTH + h * DN_DK)) for h in heads]
    v = [conv(2 * DN_WIDTH + h * DN_DK) for h in heads]
    beta = [beta_all[:, h:h + 1] for h in heads]
    gcol = [gc[:, DN_HEADS + h:DN_HEADS + h + 1] for h in heads]
    glast = [gl[:, DN_HEADS + h:DN_HEADS + h + 1] for h in heads]
    dec = [jnp.exp(jnp.where(incl, gcol[h] - gct[DN_HEADS + h:DN_HEADS + h + 1, :], NEG)) for h in heads]
    kb = [k[h] * beta[h] for h in heads]
    k16 = [k[h].astype(BF16) for h in heads]
    a = [jnp.where(strict, _dot_nt(kb[h].astype(BF16), k16[h]) * dec[h], 0.0) for h in heads]
    qk = [_dot_nt(q[h].astype(BF16), k16[h]) * dec[h] for h in heads]

    n1 = [jnp.where(same16, -a[h], 0.0) for h in heads]
    n2 = [_mm3(n1[h], n1[h]) for h in heads]
    n4 = [_mm3(n2[h], n2[h]) for h in heads]
    n8 = [_mm3(n4[h], n4[h]) for h in heads]
    p = [eye + n1[h] for h in heads]
    p = [p[h] + _mm2(p[h], n2[h], "a") for h in heads]
    p = [p[h] + _mm2(p[h], n4[h], "a") for h in heads]
    p = [p[h] + _mm2(p[h], n8[h], "a") for h in heads]
    off16 = jnp.logical_and(same32, jnp.logical_not(same16))
    t1 = [_mm2(jnp.where(off16, a[h], 0.0), p[h], "b") for h in heads]
    p = [p[h] - _mm2(p[h], t1[h], "a") for h in heads]
    t2 = [_mm2(jnp.where(same32, 0.0, a[h]), p[h], "b") for h in heads]
    p = [p[h] - _mm2(p[h], t2[h], "a") for h in heads]

    eg = [jnp.exp(gcol[h]) for h in heads]
    sol = [_mm2(p[h], jnp.concatenate([v[h] * beta[h], kb[h] * eg[h]], axis=1), "a") for h in heads]
    qd = [q[h] * eg[h] for h in heads]
    kd = [k[h] * jnp.exp(glast[h] - gcol[h]) for h in heads]

    sh = [s_scr[h] for h in heads]
    for c in range(tb // DN_CHUNK):
        r = slice(c * DN_CHUNK, (c + 1) * DN_CHUNK)
        for h in heads:
            wq = jnp.concatenate([sol[h][r, DN_DV:], qd[h][r]], axis=0).astype(BF16)
            ws_ = _dot(wq, sh[h].astype(BF16))
            vn16 = (sol[h][r, :DN_DV] - ws_[:DN_CHUNK]).astype(BF16)
            o = ws_[DN_CHUNK:] + _dot(qk[h][r, r].astype(BF16), vn16)
            sh[h] = (sh[h] * jnp.exp(glast[h][c * DN_CHUNK:c * DN_CHUNK + 1, :])
                     + _dot_tn(kd[h][r].astype(BF16), vn16))
            z = pdn_ref[0, r, 3 * DN_WIDTH + h * DN_DV:3 * DN_WIDTH + (h + 1) * DN_DV].astype(F32)
            o = o * lax.rsqrt(jnp.mean(o * o, axis=-1, keepdims=True) + NORM_EPS) * nw_ref[...] * _silu(z)
            o_ref[0, r, h * DN_DV:(h + 1) * DN_DV] = o.astype(o_ref.dtype)
    for h in heads:
        s_scr[h] = sh[h]

    buf[0:8, :] = buf[tb:tb + 8, :]

    @pl.when(t == pl.num_programs(1) - 1)
    def _():
        s_out_ref[0] = s_scr[...]


def _dn_gate_rows(a_log, dt_bias):
    ab = jnp.zeros((a_log.shape[0], 2, LANES), F32)
    ab = ab.at[:, 0, DN_HEADS:2 * DN_HEADS].set(a_log)
    return ab.at[:, 1, DN_HEADS:2 * DN_HEADS].set(dt_bias)


def _deltanet_prompt(pdn, pba, conv_w, ab, norm_w):
    n, l, _ = pdn.shape
    assert l % DN_TB == 0
    return pl.pallas_call(
        _dn_body,
        grid=(n, l // DN_TB),
        in_specs=[
            pl.BlockSpec((1, DN_TB, 4 * DN_WIDTH), lambda i, t: (i, t, 0)),
            pl.BlockSpec((1, DN_TB, LANES), lambda i, t: (i, t, 0)),
            _wspec(conv_w), _wspec(ab), _wspec(norm_w),
        ],
        out_specs=[
            pl.BlockSpec((1, DN_TB, DN_WIDTH), lambda i, t: (i, t, 0)),
            pl.BlockSpec((1, DN_HEADS, DN_DK, DN_DV), lambda i, t: (i, 0, 0, 0)),
        ],
        out_shape=[
            jax.ShapeDtypeStruct((n, l, DN_WIDTH), BF16),
            jax.ShapeDtypeStruct((n, DN_HEADS, DN_DK, DN_DV), F32),
        ],
        scratch_shapes=[pltpu.VMEM((DN_TB + 8, 3 * DN_WIDTH), F32), pltpu.VMEM((DN_HEADS, DN_DK, DN_DV), F32)],
        compiler_params=_params("parallel", "arbitrary"),
        name="deltanet_prompt",
    )(pdn, pba, _warr(conv_w), _warr(ab), _warr(norm_w))


def _dn_step_body(pdn_ref, pba_ref, cbuf_ref, s0_ref, cw_ref, ab_ref, nw_ref, o_ref, s_out_ref):
    x = pdn_ref[0, :, 0:3 * DN_WIDTH].astype(F32)
    y = cw_ref[3:4, :] * x
    for j in range(1, DN_CONV):
        y = y + cw_ref[3 - j:4 - j, :] * cbuf_ref[0, 3 - j:4 - j, :]
    qkv = _silu(y)
    pba = pba_ref[0]
    beta_all = jax.nn.sigmoid(pba)
    g_all = -jnp.exp(ab_ref[0:1, :]) * jax.nn.softplus(pba + ab_ref[1:2, :])

    def l2norm(t):
        return t * lax.rsqrt(jnp.sum(t * t, axis=-1, keepdims=True) + NORM_EPS)

    sub = lax.broadcasted_iota(jnp.int32, (8, DN_DK), 0)
    for h in range(DN_HEADS):
        q = l2norm(qkv[:, h * DN_DK:(h + 1) * DN_DK]) * DN_DK ** -0.5
        k = l2norm(qkv[:, DN_WIDTH + h * DN_DK:DN_WIDTH + (h + 1) * DN_DK])
        v = qkv[:, 2 * DN_WIDTH + h * DN_DV:2 * DN_WIDTH + (h + 1) * DN_DV]
        beta = beta_all[:, h:h + 1]
        eg = jnp.exp(g_all[:, DN_HEADS + h:DN_HEADS + h + 1])
        s0 = s0_ref[0, h]
        w = k * beta * eg
        lhs = jnp.where(sub == 0, w, jnp.where(sub == 1, q * eg, 0.0))
        ws_ = jnp.dot(lhs, s0, precision=lax.Precision.HIGHEST, preferred_element_type=F32)
        vn = v * beta - ws_[0:1, :]
        o = ws_[1:2, :] + jnp.sum(q * k, axis=-1, keepdims=True) * vn
        kt = jnp.broadcast_to(k, (DN_DK, DN_DK)).T
        s_out_ref[0, h] = s0 * eg + kt * vn
        z = pdn_ref[0, :, 3 * DN_WIDTH + h * DN_DV:3 * DN_WIDTH + (h + 1) * DN_DV].astype(F32)
        o = o * lax.rsqrt(jnp.mean(o * o, axis=-1, keepdims=True) + NORM_EPS) * nw_ref[...] * _silu(z)
        o_ref[0, :, h * DN_DV:(h + 1) * DN_DV] = o.astype(o_ref.dtype)


def _deltanet_step(pdn, pba, conv_buf, s0, conv_w, ab, norm_w):
    n = pdn.shape[0]
    return pl.pallas_call(
        _dn_step_body,
        grid=(n,),
        in_specs=[
            pl.BlockSpec((1, 1, 4 * DN_WIDTH), lambda i: (i, 0, 0)),
            pl.BlockSpec((1, 1, LANES), lambda i: (i, 0, 0)),
            pl.BlockSpec((1, DN_CONV - 1, 3 * DN_WIDTH), lambda i: (i, 0, 0)),
            pl.BlockSpec((1, DN_HEADS, DN_DK, DN_DV), lambda i: (i, 0, 0, 0)),
            _wspec(conv_w), _wspec(ab), _wspec(norm_w),
        ],
        out_specs=[
            pl.BlockSpec((1, 1, DN_WIDTH), lambda i: (i, 0, 0)),
            pl.BlockSpec((1, DN_HEADS, DN_DK, DN_DV), lambda i: (i, 0, 0, 0)),
        ],
        out_shape=[
            jax.ShapeDtypeStruct((n, 1, DN_WIDTH), BF16),
            jax.ShapeDtypeStruct((n, DN_HEADS, DN_DK, DN_DV), F32),
        ],
        compiler_params=_params("parallel"),
        name="deltanet_step",
    )(pdn, pba, conv_buf, s0, _warr(conv_w), _warr(ab), _warr(norm_w))


def _s5_discretize(a_re, a_im, log_dt, b_re, b_im):
    dt = jnp.exp(log_dt)[:, None]
    mag = jnp.exp(a_re * dt)
    abr, abi = mag * jnp.cos(a_im * dt), mag * jnp.sin(a_im * dt)
    den = a_re * a_re + a_im * a_im
    fr = ((abr - 1.0) * a_re + abi * a_im) / den
    fi = (abi * a_re - (abr - 1.0) * a_im) / den
    bbr = fr[..., None] * b_re - fi[..., None] * b_im
    bbi = fr[..., None] * b_im + fi[..., None] * b_re
    return abr, abi, bbr, bbi


def _s5_powers(a_re, a_im, log_dt, taus):
    dt = jnp.exp(log_dt)[None, :, None]
    tau = jnp.asarray(taus, F32)[:, None, None]
    mag = jnp.exp(a_re[None] * dt * tau)
    ang = a_im[None] * dt * tau
    return mag * jnp.cos(ang), mag * jnp.sin(ang)


S5_GB = LANES // S5_GROUP
S5_NGB = S5_GROUPS // S5_GB
S5_K = S5_T * LANES
S5_HALF = S5_GB * S5_STATE
S5_ROWS = 256


def _s5_prompt_weights(a_re, a_im, log_dt, b_re, b_im, c_re, c_im):
    t = S5_T
    hp = lax.Precision.HIGHEST
    eye = jnp.eye(S5_GB, dtype=F32)
    _, _, bbr, bbi = _s5_discretize(a_re, a_im, log_dt, b_re, b_im)
    lr, li = _s5_powers(a_re, a_im, log_dt, list(range(t + 1)))
    lbr = lr[..., None] * bbr[None] - li[..., None] * bbi[None]
    lbi = lr[..., None] * bbi[None] + li[..., None] * bbr[None]
    kt = (jnp.einsum("ghp,tgpk->tghk", c_re, lbr, precision=hp)
          - jnp.einsum("ghp,tgpk->tghk", c_im, lbi, precision=hp))
    lag = np.arange(t)[None, :] - np.arange(t)[:, None]
    kblk = kt.transpose(0, 1, 3, 2).reshape(t + 1, S5_NGB, S5_GB, S5_GROUP, S5_GROUP)
    kblk = jnp.einsum("tBaph,am->tBapmh", kblk, eye).reshape(t + 1, S5_NGB, LANES, LANES).astype(BF16)
    toep = jnp.where((lag >= 0)[:, :, None, None, None], kblk[np.clip(lag, 0, t)], 0.0)
    toep = toep.transpose(2, 0, 3, 1, 4).reshape(S5_NGB, S5_K, S5_K)
    sel = t - 1 - np.arange(t)
    wz = jnp.stack([lbr[sel], lbi[sel]])
    wz = wz.transpose(2, 1, 4, 0, 3).reshape(S5_NGB, S5_GB, t, S5_GROUP, 2, S5_STATE)
    wz = jnp.einsum("Baipcq,am->Biapcmq", wz.astype(BF16), eye.astype(BF16)).reshape(S5_NGB, S5_K, 2 * S5_HALF)
    lr1, li1 = lr[1:], li[1:]
    wy = jnp.stack([c_re[None] * lr1[:, :, None, :] - c_im[None] * li1[:, :, None, :],
                    -(c_re[None] * li1[:, :, None, :] + c_im[None] * lr1[:, :, None, :])])
    wy = wy.transpose(2, 0, 4, 1, 3).reshape(S5_NGB, S5_GB, 2, S5_STATE, t, S5_GROUP)
    wy = jnp.einsum("Bacpjq,am->Bcapjmq", wy.astype(BF16), eye.astype(BF16)).reshape(S5_NGB, 2 * S5_HALF, S5_K)
    n_steps = int(math.log2(S5_ROWS))
    sr, si = _s5_powers(a_re, a_im, log_dt, [t * (1 << i) for i in range(n_steps)])
    blocks = lambda v: v.reshape(n_steps, S5_NGB, S5_HALF).transpose(1, 0, 2)
    sc_r = jnp.concatenate([blocks(sr), blocks(sr)], axis=-1)
    sc_i = jnp.concatenate([-blocks(si), blocks(si)], axis=-1)
    return toep, wz, wy, sc_r, sc_i


def _s5_body(u_ref, toep_ref, wz_ref, wy_ref, lr_ref, li_ref, y_ref, sfin_ref, carry_s):
    rb, b = pl.program_id(1), pl.program_id(2)
    lanes = [pl.ds(pl.multiple_of(t * S5_WIDTH + b * LANES, LANES), LANES) for t in range(S5_T)]
    u = jnp.concatenate([u_ref[:, lanes[t]] for t in range(S5_T)], axis=1).astype(BF16)

    @pl.when(rb == 0)
    def _():
        carry_s[b] = jnp.zeros((1, 2 * S5_HALF), F32)

    swap = lambda v: pltpu.roll(v, S5_HALF, 1)
    z = _dot(u, wz_ref[...])
    rowi = lax.broadcasted_iota(jnp.int32, z.shape, 0)
    carry = carry_s[b]
    s = z + jnp.where(rowi == 0, lr_ref[0:1, :] * carry + li_ref[0:1, :] * swap(carry), 0.0)
    for i in range(lr_ref.shape[0]):
        d = 1 << i
        sh = jnp.where(rowi >= d, pltpu.roll(s, d, 0), 0.0)
        s = s + lr_ref[i:i + 1, :] * sh + li_ref[i:i + 1, :] * swap(sh)
    last = s[z.shape[0] - 1:z.shape[0], :]
    sprev = jnp.where(rowi >= 1, pltpu.roll(s, 1, 0), carry)
    carry_s[b] = last
    sfin_ref[b] = last
    y = _dot(u, toep_ref[...]) + _dot(sprev.astype(BF16), wy_ref[...])
    for t in range(S5_T):
        y_ref[:, lanes[t]] = y[:, t * LANES:(t + 1) * LANES]


def _s5_prompt(u, wts):
    n, l, _ = u.shape
    nc = l // S5_T
    assert nc % S5_ROWS == 0
    width = S5_T * S5_WIDTH
    wspec = lambda t: pl.BlockSpec((None, None) + t.shape[1:], lambda i, r, b: (t.layer, b, 0, 0))
    y, sfin = pl.pallas_call(
        _s5_body,
        grid=(n, nc // S5_ROWS, S5_NGB),
        in_specs=[pl.BlockSpec((None, S5_ROWS, width), lambda i, r, b: (i, r, 0)),
                  *map(wspec, wts)],
        out_specs=[pl.BlockSpec((None, S5_ROWS, width), lambda i, r, b: (i, r, 0)),
                   pl.BlockSpec((None, S5_NGB, 1, 2 * S5_HALF), lambda i, r, b: (i, 0, 0, 0))],
        out_shape=[jax.ShapeDtypeStruct((n, nc, width), F32),
                   jax.ShapeDtypeStruct((n, S5_NGB, 1, 2 * S5_HALF), F32)],
        scratch_shapes=[pltpu.VMEM((S5_NGB, 1, 2 * S5_HALF), F32)],
        compiler_params=_params("parallel", "arbitrary", "arbitrary"),
        name="s5_prompt",
    )(u.reshape(n, nc, width), *map(_warr, wts))
    sfin = sfin.reshape(n, S5_NGB, 2, S5_GB, S5_STATE).transpose(2, 0, 1, 3, 4).reshape(2, n, S5_GROUPS, S5_STATE)
    return y.reshape(n, l, S5_WIDTH), sfin[0], sfin[1]


def _s5_step_weights(a_re, a_im, log_dt, b_re, b_im, c_re, c_im):
    abr, abi, bbr, bbi = _s5_discretize(a_re, a_im, log_dt, b_re, b_im)
    eye = jnp.eye(S5_GROUPS, dtype=F32)
    gp = S5_GROUPS * S5_STATE
    wb = lambda b: jnp.einsum("gph,gk->ghkp", b, eye).reshape(S5_WIDTH, gp).astype(BF16)
    wc = lambda c: jnp.einsum("ghp,gk->gpkh", c, eye).reshape(gp, S5_WIDTH).astype(BF16)
    return (abr.reshape(1, gp), abi.reshape(1, gp), wb(bbr), wb(bbi), wc(c_re), wc(c_im))


def _s5_step_body(u_ref, s0r_ref, s0i_ref, abr_ref, abi_ref, wbr, wbi, wcr, wci, y_ref, sr_ref, si_ref):
    u = u_ref[...].astype(BF16)
    abr, abi = abr_ref[...], abi_ref[...]
    s0r, s0i = s0r_ref[...], s0i_ref[...]
    sr = abr * s0r - abi * s0i + _dot(u, wbr[...])
    si = abr * s0i + abi * s0r + _dot(u, wbi[...])
    sr_ref[...] = sr
    si_ref[...] = si
    y_ref[...] = _dot(sr.astype(BF16), wcr[...]) - _dot(si.astype(BF16), wci[...])


def _s5_step(u, s0_re, s0_im, wts):
    n = u.shape[0]
    gp = S5_GROUPS * S5_STATE
    args = (u, s0_re.reshape(n, gp), s0_im.reshape(n, gp)) + tuple(wts)
    y, sr, si = pl.pallas_call(
        _s5_step_body,
        grid=(1,),
        in_specs=[_wspec(a) for a in args],
        out_specs=[_const_spec((n, S5_WIDTH)), _const_spec((n, gp)), _const_spec((n, gp))],
        out_shape=[jax.ShapeDtypeStruct((n, S5_WIDTH), F32), jax.ShapeDtypeStruct((n, gp), F32),
                   jax.ShapeDtypeStruct((n, gp), F32)],
        compiler_params=_params("arbitrary"),
        name="s5_step",
    )(*map(_warr, args))
    return y, sr.reshape(n, S5_GROUPS, S5_STATE), si.reshape(n, S5_GROUPS, S5_STATE)


def _top_mask(gate, pos, count, axis=-1):
    sel = jnp.zeros(gate.shape, F32)
    for _ in range(count):
        m = jnp.max(gate, axis=axis, keepdims=True)
        idx = jnp.min(jnp.where(gate == m, pos, 2 ** 30), axis=axis, keepdims=True)
        hit = jnp.logical_and(pos == idx, m > -jnp.inf)
        sel = jnp.where(hit, 1.0, sel)
        gate = jnp.where(pos == idx, -jnp.inf, gate)
    return sel


def _alibi_slopes():
    s = np.exp2(-8.0 * np.arange(1, MB_HEADS + 1, dtype=np.float32) / MB_HEADS).astype(np.float32)
    return jnp.asarray(np.broadcast_to(s[:, None, None], (MB_HEADS, 1, LANES)))


MB_SLOTS = 16
MB_V_ROWS = MB_HD + 16
MB_GROUP = 2
MB_HEADS_PER_STEP = 4


def _moba_body(q_ref, k_ref, v_ref, slope_ref, o_ref, kaug_s, vaug_s, means_s, *, nb):
    blk = MB_BLOCK
    qi = pl.program_id(2)
    heads = range(MB_HEADS_PER_STEP)
    cols = [slice(h * MB_HD, (h + 1) * MB_HD) for h in heads]
    slope = [slope_ref[h, :, 0:1] for h in heads]

    @pl.when(qi == 0)
    def _():
        lane = lax.broadcasted_iota(jnp.int32, (blk, MB_HD), 1)
        off = lax.broadcasted_iota(jnp.int32, (blk, MB_HD), 0).astype(F32)
        means_s[...] = jnp.zeros(means_s.shape, F32)
        vaug_s[:, :, MB_HD:, :] = jnp.ones((len(heads), nb, MB_V_ROWS - MB_HD, blk), BF16)
        for b in range(nb):
            for h in heads:
                kb = k_ref[0, b * blk:(b + 1) * blk, cols[h]]
                means_s[h, b:b + 1, :] = jnp.mean(kb, axis=0, keepdims=True)
                feat = jnp.where(lane == b, 1.0, 0.0)
                feat = jnp.where(lane == MB_SLOTS, slope[h] * off, feat)
                feat = jnp.where(lane == MB_SLOTS + 1, slope[h] * float(b * blk), feat)
                feat = jnp.where(lane == MB_SLOTS + 2, 1.0, feat)
                kaug_s[h, b] = jnp.concatenate([kb.astype(BF16), feat.astype(BF16)], axis=1)
                vaug_s[h, b, 0:MB_HD, :] = v_ref[0, b * blk:(b + 1) * blk, cols[h]].T.astype(BF16)

    slot = lax.broadcasted_iota(jnp.int32, (MB_SLOTS, blk), 0)
    pad = jnp.zeros((MB_HD - 2 * MB_SLOTS, blk), BF16)
    qt = [q_ref[0, :, cols[h]].astype(F32).T for h in heads]
    gate = [jnp.dot(means_s[h], qt[h], precision=lax.Precision.HIGHEST, preferred_element_type=F32) for h in heads]
    sel = [_top_mask(jnp.where(slot < qi, gate[h], -jnp.inf), slot, MB_TOPK, axis=0) for h in heads]
    bias_rows = [jnp.where(slot < 2, 1.0, jnp.where(slot == 2, -slope[h] * (qi * blk).astype(F32), 0.0)).astype(BF16)
                 for h in heads]
    qt16 = [qt[h].astype(BF16) for h in heads]
    q_past = [jnp.concatenate([qt16[h], jnp.where(sel[h] > 0.0, 0.0, NEG).astype(BF16), bias_rows[h], pad], axis=0)
              for h in heads]
    q_own = [jnp.concatenate([qt16[h], jnp.zeros((MB_SLOTS, blk), BF16), bias_rows[h], pad], axis=0) for h in heads]

    key = lax.broadcasted_iota(jnp.int32, (blk, blk), 0)
    qry = lax.broadcasted_iota(jnp.int32, (blk, blk), 1)
    s = [jnp.where(key <= qry, _dot(kaug_s[h, qi], q_own[h]), NEG) for h in heads]
    m = [jnp.max(s[h], axis=0, keepdims=True) for h in heads]
    acc = [_dot(vaug_s[h, qi], jnp.exp(s[h] - m[h]).astype(BF16)) for h in heads]

    grp = MB_GROUP

    def body(i, carry):
        m_old, acc = carry
        s = [[_dot(kaug_s[h, grp * i + g], q_past[h]) for g in range(grp)] for h in heads]
        m_new = list(m_old)
        for g in range(grp):
            m_new = [jnp.maximum(m_new[h], jnp.max(s[h][g], axis=0, keepdims=True)) for h in heads]
        p = [[jnp.exp(s[h][g] - m_new[h]).astype(BF16) for g in range(grp)] for h in heads]
        acc = [jnp.exp(m_old[h] - m_new[h]) * acc[h] for h in heads]
        for g in range(grp):
            acc = [acc[h] + _dot(vaug_s[h, grp * i + g], p[h][g]) for h in heads]
        return tuple(m_new), tuple(acc)

    m, acc = lax.fori_loop(0, (qi + grp - 1) // grp, body, (tuple(m), tuple(acc)))
    for h in heads:
        o_ref[0, :, cols[h]] = (acc[h][0:MB_HD] / acc[h][MB_HD:MB_HD + 1]).T.astype(o_ref.dtype)


def _moba_prompt(q, k, v):
    n, l, _ = q.shape
    nb = l // MB_BLOCK
    hps = MB_HEADS_PER_STEP
    width = hps * MB_HD
    assert l % MB_BLOCK == 0 and nb <= MB_SLOTS and nb % MB_GROUP == 0 and MB_HEADS % hps == 0
    kv_spec = pl.BlockSpec((1, l, width), lambda i, h, t: (i, 0, h))
    return pl.pallas_call(
        functools.partial(_moba_body, nb=nb),
        grid=(n, MB_HEADS // hps, nb),
        in_specs=[
            pl.BlockSpec((1, MB_BLOCK, width), lambda i, h, t: (i, t, h)),
            kv_spec, kv_spec,
            pl.BlockSpec((hps, 1, LANES), lambda i, h, t: (h, 0, 0)),
        ],
        out_specs=pl.BlockSpec((1, MB_BLOCK, width), lambda i, h, t: (i, t, h)),
        out_shape=jax.ShapeDtypeStruct((n, l, MB_WIDTH), BF16),
        scratch_shapes=[pltpu.VMEM((hps, nb, MB_BLOCK, 2 * MB_HD), BF16),
                        pltpu.VMEM((hps, nb, MB_V_ROWS, MB_BLOCK), BF16),
                        pltpu.VMEM((hps, MB_SLOTS, MB_HD), F32)],
        compiler_params=_params("parallel", "parallel", "arbitrary"),
        name="moba_prompt",
    )(q, k, v, _alibi_slopes())


PAGES_PER_STEP = 32
PAGES_PER_BLOCK = MB_BLOCK // PAGE_SIZE


def _page_sum_body(pt_ref, *refs):
    del pt_ref
    o_ref = refs[-1]
    for b in range(PAGES_PER_STEP // PAGES_PER_BLOCK):
        acc = jnp.sum(refs[PAGES_PER_BLOCK * b][...], axis=0)
        for j in range(1, PAGES_PER_BLOCK):
            acc = acc + jnp.sum(refs[PAGES_PER_BLOCK * b + j][...], axis=0)
        o_ref[b] = acc


def _block_key_sums(pool, page_table, layer):
    n, n_pages = page_table.shape
    assert n_pages % PAGES_PER_STEP == 0
    tail = pool.shape[2:]

    def page_spec(j):
        return pl.BlockSpec((None, None) + tail, lambda i, s, pt: (layer, pt[i, s * PAGES_PER_STEP + j], 0, 0, 0))

    blocks_per_step = PAGES_PER_STEP // PAGES_PER_BLOCK
    return pl.pallas_call(
        _page_sum_body,
        grid_spec=pltpu.PrefetchScalarGridSpec(
            num_scalar_prefetch=1,
            grid=(n, n_pages // PAGES_PER_STEP),
            in_specs=[page_spec(j) for j in range(PAGES_PER_STEP)],
            out_specs=pl.BlockSpec((None, blocks_per_step) + tail[1:], lambda i, s, pt: (i, s, 0, 0)),
        ),
        out_shape=jax.ShapeDtypeStruct((n, n_pages // PAGES_PER_BLOCK) + tail[1:], F32),
        compiler_params=_params("parallel", "arbitrary"),
        name="moba_page_sums",
    )(page_table, *([pool] * PAGES_PER_STEP))


def _moba_pick_body(q_ref, sums_ref, sel_ref):
    nbk = sums_ref.shape[0]
    out = jnp.zeros(sel_ref.shape, jnp.int32)
    osub = lax.broadcasted_iota(jnp.int32, out.shape, 0)
    olane = lax.broadcasted_iota(jnp.int32, out.shape, 1)
    bidx = lax.broadcasted_iota(jnp.int32, (nbk, 1), 0)
    for h in range(MB_HEADS):
        means = sums_ref[:, h, :] / float(MB_BLOCK)
        q = q_ref[:, h * MB_HD:(h + 1) * MB_HD].astype(F32)
        gate = jnp.sum(means * q, axis=-1, keepdims=True)
        for rnk in range(MB_TOPK):
            m = jnp.max(gate, axis=0, keepdims=True)
            idx = jnp.min(jnp.where(gate == m, bidx, 2 ** 30), axis=0, keepdims=True)
            out = jnp.where(jnp.logical_and(osub == h, olane == rnk), idx, out)
            gate = jnp.where(bidx == idx, -jnp.inf, gate)
    sel_ref[...] = out


def _moba_pick(q, sums):
    n = q.shape[0]
    return pl.pallas_call(
        _moba_pick_body,
        grid=(n,),
        in_specs=[pl.BlockSpec((None, 1, MB_WIDTH), lambda i: (i, 0, 0)),
                  pl.BlockSpec((None,) + sums.shape[1:], lambda i: (i, 0, 0, 0))],
        out_specs=pl.BlockSpec((None, 8, LANES), lambda i: (i, 0, 0)),
        out_shape=jax.ShapeDtypeStruct((n, 8, LANES), jnp.int32),
        compiler_params=_params("parallel"),
        name="moba_pick",
    )(q, sums)


def _moba_step_body(sel_ref, pt_ref, q_ref, kn_ref, vn_ref, slope_ref, *refs, past):
    del pt_ref
    pages, o_ref, m_s, l_s, acc_s = refs[:4 * MB_HEADS], refs[4 * MB_HEADS], *refs[4 * MB_HEADS + 1:]
    i, s = pl.program_id(0), pl.program_id(1)
    sub = lax.broadcasted_iota(jnp.int32, (PAGE_SIZE, 1), 0)
    for h in range(MB_HEADS):
        q = q_ref[0, h].astype(F32)
        slope = slope_ref[h, :, 0:1]

        @pl.when(s == 0)
        def _():
            m_s[h] = jnp.sum(q * kn_ref[0, h], axis=-1, keepdims=True)
            l_s[h] = jnp.ones((1, 1), F32)
            acc_s[h] = vn_ref[0, h]

        blk = sel_ref[(i * MB_HEADS + h) * MB_TOPK + s]
        k0, k1, v0, v1 = pages[4 * h:4 * h + 4]
        m_old = m_s[h]
        sc = []
        for half, kr in enumerate((k0, k1)):
            kpos = blk * MB_BLOCK + half * PAGE_SIZE + sub
            sc.append(jnp.sum(kr[:, h, :] * q, axis=-1, keepdims=True) - slope * (past - kpos).astype(F32))
        m_new = jnp.maximum(m_old, jnp.maximum(jnp.max(sc[0], axis=0, keepdims=True),
                                               jnp.max(sc[1], axis=0, keepdims=True)))
        alpha = jnp.exp(m_old - m_new)
        p0 = jnp.exp(sc[0] - m_new)
        p1 = jnp.exp(sc[1] - m_new)
        l_new = alpha * l_s[h] + jnp.sum(p0, axis=0, keepdims=True) + jnp.sum(p1, axis=0, keepdims=True)
        acc_new = (alpha * acc_s[h] + jnp.sum(p0 * v0[:, h, :], axis=0, keepdims=True)
                   + jnp.sum(p1 * v1[:, h, :], axis=0, keepdims=True))
        m_s[h] = m_new
        l_s[h] = l_new
        acc_s[h] = acc_new

        @pl.when(s == MB_TOPK - 1)
        def _():
            o_ref[0, h] = (acc_new / l_new).astype(o_ref.dtype)


def _moba_step(q, k_new, v_new, sel, page_table, k_pool, v_pool, layer):
    n = q.shape[0]
    past = page_table.shape[1] * PAGE_SIZE
    assert PAGES_PER_BLOCK == 2 and past // MB_BLOCK >= MB_TOPK
    tail = k_pool.shape[2:]

    def page_spec(h, half):
        def imap(i, s, sel_r, pt_r):
            blk = sel_r[(i * MB_HEADS + h) * MB_TOPK + s]
            return (layer, pt_r[i, blk * PAGES_PER_BLOCK + half], 0, 0, 0)
        return pl.BlockSpec((None, None) + tail, imap)

    page_specs, page_args = [], []
    for h in range(MB_HEADS):
        for pool in (k_pool, v_pool):
            for half in range(PAGES_PER_BLOCK):
                page_specs.append(page_spec(h, half))
                page_args.append(pool)
    tok = pl.BlockSpec((1, MB_HEADS, 1, MB_HD), lambda i, s, *_: (i, 0, 0, 0))
    return pl.pallas_call(
        functools.partial(_moba_step_body, past=past),
        grid_spec=pltpu.PrefetchScalarGridSpec(
            num_scalar_prefetch=2,
            grid=(n, MB_TOPK),
            in_specs=[tok, tok, tok, _const_spec((MB_HEADS, 1, LANES))] + page_specs,
            out_specs=tok,
            scratch_shapes=[pltpu.VMEM((MB_HEADS, 1, 1), F32), pltpu.VMEM((MB_HEADS, 1, 1), F32),
                            pltpu.VMEM((MB_HEADS, 1, MB_HD), F32)],
        ),
        out_shape=jax.ShapeDtypeStruct((n, MB_HEADS, 1, MB_HD), BF16),
        compiler_params=_params("parallel", "arbitrary"),
        name="moba_step",
    )(sel, page_table, q, k_new, v_new, _alibi_slopes(), *page_args)


def _merge_body(x_ref, a_ref, y5_ref, u_ref, c_ref, g_ref, d_ref, gluw, glub, wa, wb, wc, wo, lng, lnb, h_ref,
                *, alpha):
    y = jax.nn.gelu(y5_ref[...] + d_ref[...] * u_ref[...])
    b = y * jax.nn.sigmoid(_dot(y.astype(BF16), gluw[...]) + glub[...])
    dm = x_ref.shape[1]
    gate = lambda j: jax.nn.sigmoid(g_ref[:, j * dm:(j + 1) * dm].astype(F32))
    merged = (gate(0) * _dot(a_ref[...], wa[...]) + gate(1) * _dot(b.astype(BF16), wb[...])
              + gate(2) * _dot(c_ref[...], wc[...]))
    r = alpha * x_ref[...] + _dot(merged.astype(BF16), wo[...])
    h_ref[...] = _layer_norm(r, lng[...], lnb[...])


def _merge(x, a, y5, u, c, g, w, alpha, tm):
    m, dm = x.shape
    acts = (x, a, y5, u, c, g)
    consts = (w["s5_d"], w["glu_w"], w["glu_b"], w["wa"], w["wb"], w["wc"], w["wo"], w["ln1_g"], w["ln1_b"])
    return pl.pallas_call(
        functools.partial(_merge_body, alpha=alpha),
        grid=(m // tm,),
        in_specs=[pl.BlockSpec((tm, t.shape[1]), lambda i: (i, 0)) for t in acts]
        + [_wspec(t) for t in consts],
        out_specs=pl.BlockSpec((tm, dm), lambda i: (i, 0)),
        out_shape=jax.ShapeDtypeStruct((m, dm), F32),
        compiler_params=_params("parallel"),
        name="merge",
    )(*acts, *map(_warr, consts))


def _moe_body(h_ref, rw, rb, wg, wu, wd, sgu, sd, lng, lnb, y_ref, xb_s, gate_s, acc_s, *, alpha):
    e = pl.program_id(1)
    lane = lax.broadcasted_iota(jnp.int32, gate_s.shape, 1)

    @pl.when(e == 0)
    def _():
        xb = h_ref[...].astype(BF16)
        xb_s[...] = xb
        scores = jax.nn.sigmoid(_dot_nt(rw[...], xb))
        slot = lax.broadcasted_iota(jnp.int32, scores.shape, 0)
        ranked = jnp.where(slot < N_EXPERTS, scores + rb[...], -jnp.inf)
        picked = _top_mask(ranked, slot, TOP_K, axis=0) * scores
        gate_s[...] = (picked / jnp.sum(picked, axis=0, keepdims=True) * ROUTED_SCALE).T
        hs = _dot(xb, sgu[...])
        ff = sd.shape[0]
        acc_s[...] = _dot((_silu(hs[:, :ff]) * hs[:, ff:]).astype(BF16), sd[...])

    xb = xb_s[...]
    gcol = jnp.sum(jnp.where(lane == e, gate_s[...], 0.0), axis=-1, keepdims=True)
    hm = _silu(_dot(xb, wg[...])) * _dot(xb, wu[...]) * gcol
    acc_s[...] += _dot(hm.astype(BF16), wd[...])

    @pl.when(e == pl.num_programs(1) - 1)
    def _():
        y_ref[...] = _layer_norm(alpha * h_ref[...] + acc_s[...], lng[...], lnb[...])


def _moe(h, w, alpha, tm):
    m, dm = h.shape
    consts_a = (w["router_w"], w["router_b"])
    consts_b = (w["sh_gu"], w["sh_d"], w["ln2_g"], w["ln2_b"])
    layer = w["layer"]
    return pl.pallas_call(
        functools.partial(_moe_body, alpha=alpha),
        grid=(m // tm, N_EXPERTS),
        in_specs=[pl.BlockSpec((tm, dm), lambda i, e: (i, 0))]
        + [_wspec(t) for t in consts_a]
        + [pl.BlockSpec((None, None, dm, EXPERT_FF), lambda i, e: (layer, e, 0, 0)),
           pl.BlockSpec((None, None, dm, EXPERT_FF), lambda i, e: (layer, e, 0, 0)),
           pl.BlockSpec((None, None, EXPERT_FF, dm), lambda i, e: (layer, e, 0, 0))]
        + [_wspec(t) for t in consts_b],
        out_specs=pl.BlockSpec((tm, dm), lambda i, e: (i, 0)),
        out_shape=jax.ShapeDtypeStruct((m, dm), F32),
        scratch_shapes=[pltpu.VMEM((tm, dm), BF16), pltpu.VMEM((tm, LANES), F32), pltpu.VMEM((tm, dm), F32)],
        compiler_params=_params("parallel", "arbitrary"),
        name="moe",
    )(h, *map(_warr, consts_a), w["exp_g"], w["exp_u"], w["exp_d"], *map(_warr, consts_b))


def _stacked_weights(p):
    w_in = p["w_in"]
    o = 0
    w = {}
    for name, size in (("dn", 4 * DN_WIDTH), ("ba", 2 * DN_HEADS), ("s5", S5_WIDTH), ("mq", MB_WIDTH),
                       ("k", MB_WIDTH), ("v", MB_WIDTH), ("g", 3 * w_in.shape[1])):
        w["w_" + name] = w_in[:, :, o:o + size].astype(BF16)
        o += size
    w["w_ba"] = jnp.pad(w["w_ba"], ((0, 0), (0, 0), (0, LANES - 2 * DN_HEADS)))
    row = lambda v: v[:, None, :]
    w["conv_w"] = p["dn_conv_w"]
    w["dn_ab"] = _dn_gate_rows(p["dn_a_log"], p["dn_dt_bias"])
    w["dn_norm_w"] = row(p["dn_norm_w"])
    s5 = tuple(p[k] for k in ("s5_a_re", "s5_a_im", "s5_log_dt", "s5_b_re", "s5_b_im", "s5_c_re", "s5_c_im"))
    w["s5_prompt"] = jax.vmap(_s5_prompt_weights)(*s5)
    w["s5_step"] = jax.vmap(_s5_step_weights)(*s5)
    w["s5_d"] = row(p["s5_d"])
    w["glu_w"] = p["s5_glu_w"].astype(BF16)
    w["glu_b"] = row(p["s5_glu_b"])
    w["wa"] = p["w_branch_a"].astype(BF16)
    w["wb"] = p["w_branch_b"].astype(BF16)
    w["wc"] = p["w_branch_c"].astype(BF16)
    w["wo"] = p["w_out"].astype(BF16)
    for k in ("ln1_g", "ln1_b", "ln2_g", "ln2_b"):
        w[k] = row(p[k])
    w["router_w"] = jnp.pad(p["router_w"].astype(BF16), ((0, 0), (0, 0), (0, LANES - N_EXPERTS))).transpose(0, 2, 1)
    w["router_b"] = jnp.pad(p["router_bias"], ((0, 0), (0, LANES - N_EXPERTS)))[:, :, None]
    w["sh_gu"] = jnp.concatenate([p["sh_w_gate"], p["sh_w_up"]], axis=-1).astype(BF16)
    w["sh_d"] = p["sh_w_down"].astype(BF16)
    w["exp_g"], w["exp_u"], w["exp_d"] = (p[k].astype(BF16) for k in ("exp_w_gate", "exp_w_up", "exp_w_down"))
    return w


def _layer_weights(l, stacked):
    view = lambda t: tuple(_Layered(a, l) for a in t) if isinstance(t, tuple) else _Layered(t, l)
    w = {k: (v if k.startswith("exp_") else view(v)) for k, v in stacked.items()}
    w["layer"] = l
    return w


def _prompt_layer(x, w, alpha, tm, tm_moe):
    n, l, dm = x.shape
    x2 = x.reshape(n * l, dm)
    pdn, pba, ps5, pmq, pk, pv, pg = _proj(x2, w, tm)
    a_out, s_fin = _deltanet_prompt(pdn.reshape(n, l, -1), pba.reshape(n, l, -1), w["conv_w"], w["dn_ab"],
                                    w["dn_norm_w"])
    y5, s5_re, s5_im = _s5_prompt(ps5.reshape(n, l, -1), w["s5_prompt"])
    c_out = _moba_prompt(pmq.reshape(n, l, -1), pk.reshape(n, l, -1), pv.reshape(n, l, -1))
    pk, pv = pk.reshape(n, l, MB_HEADS, MB_HD), pv.reshape(n, l, MB_HEADS, MB_HD)
    h = _merge(x2, a_out.reshape(n * l, -1), y5.reshape(n * l, -1), ps5, c_out.reshape(n * l, -1), pg, w, alpha, tm)
    y = _moe(h, w, alpha, tm_moe)
    conv_new = pdn.reshape(n, l, -1)[:, l - (DN_CONV - 1):, :3 * DN_WIDTH].astype(F32)
    return y.reshape(n, l, dm), (pk, pv, conv_new, s_fin, s5_re, s5_im)


def _sample_layer(x, w, alpha, conv_buf, s_dn, s5_re, s5_im, k_pool, v_pool, page_table, layer):
    n, l, dm = x.shape
    assert l == 1
    x2 = x.reshape(n, dm)
    pdn, pba, ps5, pmq, pk, pv, pg = _proj(x2, w, n, dn_dtype=F32)
    a_out, s_new = _deltanet_step(pdn.reshape(n, 1, -1), pba.reshape(n, 1, -1), conv_buf, s_dn, w["conv_w"],
                                  w["dn_ab"], w["dn_norm_w"])
    y5, s5_re_new, s5_im_new = _s5_step(ps5, s5_re, s5_im, w["s5_step"])
    sums = _block_key_sums(k_pool, page_table, layer)
    sel = _moba_pick(pmq.reshape(n, 1, -1), sums)
    sel = sel[:, :MB_HEADS, :MB_TOPK].reshape(-1)
    heads = lambda t: t.reshape(n, MB_HEADS, 1, MB_HD)
    c_out = _moba_step(heads(pmq), heads(pk), heads(pv), sel, page_table, k_pool, v_pool, layer)
    h = _merge(x2, a_out.reshape(n, -1), y5, ps5, c_out.reshape(n, -1), pg, w, alpha, n)
    y = _moe(h, w, alpha, n)
    conv_new = jnp.concatenate([conv_buf[:, 1:], pdn[:, None, :3 * DN_WIDTH].astype(F32)], axis=1)
    heads4 = lambda t: t.reshape(n, 1, MB_HEADS, MB_HD)
    return y.reshape(n, 1, dm), (heads4(pk), heads4(pv), conv_new, s_new, s5_re_new, s5_im_new)


def kernel(x_prompt, x_sample, cache_k, cache_v, page_table, state_dn_conv, state_dn, state_s5_re, state_s5_im, w_in, dn_conv_w, dn_a_log, dn_dt_bias, dn_norm_w, s5_a_re, s5_a_im, s5_log_dt, s5_b_re, s5_b_im, s5_c_re, s5_c_im, s5_d, s5_glu_w, s5_glu_b, w_branch_a, w_branch_b, w_branch_c, w_out, ln1_g, ln1_b, router_w, router_bias, exp_w_gate, exp_w_up, exp_w_down, sh_w_gate, sh_w_up, sh_w_down, ln2_g, ln2_b):
    p = dict(w_in=w_in, dn_conv_w=dn_conv_w, dn_a_log=dn_a_log, dn_dt_bias=dn_dt_bias, dn_norm_w=dn_norm_w,
             s5_a_re=s5_a_re, s5_a_im=s5_a_im, s5_log_dt=s5_log_dt, s5_b_re=s5_b_re, s5_b_im=s5_b_im,
             s5_c_re=s5_c_re, s5_c_im=s5_c_im, s5_d=s5_d, s5_glu_w=s5_glu_w, s5_glu_b=s5_glu_b,
             w_branch_a=w_branch_a, w_branch_b=w_branch_b, w_branch_c=w_branch_c, w_out=w_out,
             ln1_g=ln1_g, ln1_b=ln1_b, router_w=router_w, router_bias=router_bias, exp_w_gate=exp_w_gate,
             exp_w_up=exp_w_up, exp_w_down=exp_w_down, sh_w_gate=sh_w_gate, sh_w_up=sh_w_up,
             sh_w_down=sh_w_down, ln2_g=ln2_g, ln2_b=ln2_b)
    stacked = _stacked_weights(p)
    depth = w_in.shape[0]
    alpha = (2 * depth) ** 0.25
    seq = x_prompt.shape[1]
    tm = min(512, x_prompt.shape[0] * seq)
    tm_moe = min(1024, x_prompt.shape[0] * seq)
    hp, hs = x_prompt, x_sample
    p_st, s_st = [], []
    for l in range(depth):
        w = _layer_weights(l, stacked)
        hp, st = _prompt_layer(hp, w, alpha, tm, tm_moe)
        p_st.append(st)
        hs, st = _sample_layer(hs, w, alpha, state_dn_conv[l], state_dn[l], state_s5_re[l], state_s5_im[l],
                               cache_k, cache_v, page_table, l)
        s_st.append(st)
    p_out = [jnp.stack(t) for t in zip(*p_st)]
    s_out = [jnp.stack(t) for t in zip(*s_st)]
    return (hp, hs, *p_out, *s_out)
```

```python
import functools
import math
from typing import NamedTuple

import jax
import jax.numpy as jnp
import numpy as np
from jax import lax
from jax.experimental import pallas as pl
from jax.experimental.pallas import tpu as pltpu

F32 = jnp.float32
BF16 = jnp.bfloat16

DN_HEADS = 4
DN_DK = 128
DN_DV = 128
DN_WIDTH = DN_HEADS * DN_DK
DN_CONV = 4
DN_CHUNK = 64
S5_WIDTH = 512
S5_GROUP = 16
S5_GROUPS = S5_WIDTH // S5_GROUP
S5_STATE = 64
MB_HEADS = 4
MB_HD = 128
MB_WIDTH = MB_HEADS * MB_HD
MB_BLOCK = 256
MB_TOPK = 3
PAGE_SIZE = 128
N_EXPERTS = 64
TOP_K = 8
EXPERT_FF = 256
ROUTED_SCALE = 2.5
LN_EPS = 1e-5
NORM_EPS = 1e-6

LANES = 128
S5_T = 8
DN_TB = 256
VMEM_LIMIT = 56 * 1024 * 1024
NEG = -1e30


def _params(*sem):
    return pltpu.CompilerParams(dimension_semantics=sem, vmem_limit_bytes=VMEM_LIMIT)


def _const_spec(shape):
    nd = len(shape)
    return pl.BlockSpec(shape, lambda *_: (0,) * nd)


class _Layered(NamedTuple):
    stack: jax.Array
    layer: int

    @property
    def shape(self):
        return self.stack.shape[1:]


def _wspec(t):
    if isinstance(t, _Layered):
        return pl.BlockSpec((None,) + t.shape, lambda *_: (t.layer,) + (0,) * len(t.shape))
    return _const_spec(t.shape)


def _warr(t):
    return t.stack if isinstance(t, _Layered) else t


def _dot(a, b):
    return jnp.dot(a, b, preferred_element_type=F32)


def _dot_nt(a, b, precision=None):
    return lax.dot_general(a, b, (((1,), (1,)), ((), ())), preferred_element_type=F32, precision=precision)


def _dot_tn(a, b):
    return lax.dot_general(a, b, (((0,), (0,)), ((), ())), preferred_element_type=F32)


def _split(a):
    hi = a.astype(BF16)
    return hi, (a - hi.astype(F32)).astype(BF16)


def _split3(a):
    hi = a.astype(BF16)
    r = a - hi.astype(F32)
    mid = r.astype(BF16)
    return hi, mid, (r - mid.astype(F32)).astype(BF16)


def _mm2(a, b, full):
    a_hi, a_lo = _split(a)
    b_hi, b_lo = _split(b)
    if full == "a":
        return _dot(jnp.concatenate([a_hi, a_lo], axis=1), jnp.concatenate([b_hi, b_hi], axis=0))
    return _dot(jnp.concatenate([a_hi, a_hi], axis=1), jnp.concatenate([b_hi, b_lo], axis=0))


def _silu(x):
    return x * jax.nn.sigmoid(x)


def _layer_norm(r, g, b):
    mu = jnp.mean(r, axis=-1, keepdims=True)
    c = r - mu
    var = jnp.mean(c * c, axis=-1, keepdims=True)
    return c * lax.rsqrt(var + LN_EPS) * g + b


def _proj_body(x_ref, wdn, wba, ws5, wmq, wk, wv, wg, odn, oba, os5, omq, ok, ov, og, *, q_scale):
    xb = x_ref[...].astype(BF16)

    def mm(w_ref, o_ref, scale=None):
        n = w_ref.shape[1]
        step = min(n, 512)
        for c in range(0, n, step):
            r = _dot(xb, w_ref[:, c:c + step])
            if scale is not None:
                r = r * scale
            o_ref[:, c:c + step] = r.astype(o_ref.dtype)

    mm(wdn, odn)
    mm(wba, oba)
    mm(ws5, os5)
    mm(wmq, omq, q_scale)
    mm(wk, ok)
    mm(wv, ov)
    mm(wg, og)


def _proj(x, w, tm, dn_dtype=BF16):
    m, d = x.shape
    outs = (("dn", dn_dtype), ("ba", F32), ("s5", F32), ("mq", BF16), ("k", F32), ("v", F32), ("g", BF16))
    ws = [w["w_" + k] for k, _ in outs]
    return pl.pallas_call(
        functools.partial(_proj_body, q_scale=MB_HD ** -0.5),
        grid=(m // tm,),
        in_specs=[pl.BlockSpec((tm, d), lambda i: (i, 0))] + [_wspec(wi) for wi in ws],
        out_specs=[pl.BlockSpec((tm, wi.shape[1]), lambda i: (i, 0)) for wi in ws],
        out_shape=[jax.ShapeDtypeStruct((m, wi.shape[1]), dt) for wi, (_, dt) in zip(ws, outs)],
        compiler_params=_params("parallel"),
        name="proj",
    )(x, *map(_warr, ws))


def _dn_body(pdn_ref, pba_ref, cw_ref, ab_ref, nw_ref, o_ref, s_out_ref, buf, s_scr):
    tb = DN_TB
    t = pl.program_id(1)

    @pl.when(t == 0)
    def _():
        buf[0:8, :] = jnp.zeros((8, buf.shape[1]), F32)
        s_scr[...] = jnp.zeros(s_scr.shape, F32)

    buf[8:8 + tb, :] = pdn_ref[0, :, 0:3 * DN_WIDTH].astype(F32)

    def conv(c0):
        cs = slice(c0, c0 + DN_DK)
        y = cw_ref[3:4, cs] * buf[8:8 + tb, cs]
        for j in range(1, DN_CONV):
            y = y + cw_ref[3 - j:4 - j, cs] * buf[8 - j:8 - j + tb, cs]
        return _silu(y)

    def l2norm(x):
        return x * lax.rsqrt(jnp.sum(x * x, axis=-1, keepdims=True) + NORM_EPS)

    pba = pba_ref[0]
    beta_all = jax.nn.sigmoid(pba)
    g_all = -jnp.exp(ab_ref[0:1, :]) * jax.nn.softplus(pba + ab_ref[1:2, :])

    row = lax.broadcasted_iota(jnp.int32, (tb, tb), 0)
    col = lax.broadcasted_iota(jnp.int32, (tb, tb), 1)
    same64 = (row >> 6) == (col >> 6)
    same32 = (row >> 5) == (col >> 5)
    same16 = (row >> 4) == (col >> 4)
    incl = jnp.logical_and(same64, row >= col)
    strict = jnp.logical_and(same64, row > col)
    g_hi, g_mid, g_lo = _split3(g_all)
    ones_incl = jnp.where(incl, 1.0, 0.0).astype(BF16)
    gc = _dot(jnp.concatenate([ones_incl] * 3, axis=1), jnp.concatenate([g_hi, g_mid, g_lo], axis=0))
    gl = jnp.concatenate(
        [jnp.broadcast_to(gc[(c + 1) * DN_CHUNK - 1:(c + 1) * DN_CHUNK, :], (DN_CHUNK, LANES))
         for c in range(tb // DN_CHUNK)], axis=0)
    gct = gc.T
    eye = jnp.where(row == col, 1.0, 0.0)

    heads = range(DN_HEADS)
    q = [l2norm(conv(h * DN_DK)) * DN_DK ** -0.5 for h in heads]
    k = [l2norm(conv(DN_WIDTH + h * DN_DK)) for h in heads]
    v = [conv(2 * DN_WIDTH + h * DN_DK) for h in heads]
    beta = [beta_all[:, h:h + 1] for h in heads]
    gcol = [gc[:, DN_HEADS + h:DN_HEADS + h + 1] for h in heads]
    glast = [gl[:, DN_HEADS + h:DN_HEADS + h + 1] for h in heads]
    dec = [jnp.exp(jnp.where(incl, gcol[h] - gct[DN_HEADS + h:DN_HEADS + h + 1, :], NEG)) for h in heads]
    kb = [k[h] * beta[h] for h in heads]
    k16 = [k[h].astype(BF16) for h in heads]
    a = [jnp.where(strict, _dot_nt(kb[h].astype(BF16), k16[h]) * dec[h], 0.0) for h in heads]
    qk = [_dot_nt(q[h].astype(BF16), k16[h]) * dec[h] for h in heads]

    n1 = [jnp.where(same16, -a[h], 0.0) for h in heads]
    n2 = [_mm2(n1[h], n1[h], "a") for h in heads]
    n4 = [_mm2(n2[h], n2[h], "a") for h in heads]
    n8 = [_mm2(n4[h], n4[h], "a") for h in heads]
    p = [eye + n1[h] for h in heads]
    p = [p[h] + _mm2(p[h], n2[h], "a") for h in heads]
    p = [p[h] + _mm2(p[h], n4[h], "a") for h in heads]
    p = [p[h] + _mm2(p[h], n8[h], "a") for h in heads]
    off16 = jnp.logical_and(same32, jnp.logical_not(same16))
    t1 = [_mm2(jnp.where(off16, a[h], 0.0), p[h], "b") for h in heads]
    p = [p[h] - _mm2(p[h], t1[h], "a") for h in heads]
    t2 = [_mm2(jnp.where(same32, 0.0, a[h]), p[h], "b") for h in heads]
    p = [p[h] - _mm2(p[h], t2[h], "a") for h in heads]

    eg = [jnp.exp(gcol[h]) for h in heads]
    sol = [_mm2(p[h], jnp.concatenate([v[h] * beta[h], kb[h] * eg[h]], axis=1), "a") for h in heads]
    qd = [q[h] * eg[h] for h in heads]
    kd = [k[h] * jnp.exp(glast[h] - gcol[h]) for h in heads]

    sh = [s_scr[h] for h in heads]
    for c in range(tb // DN_CHUNK):
        r = slice(c * DN_CHUNK, (c + 1) * DN_CHUNK)
        for h in heads:
            wq = jnp.concatenate([sol[h][r, DN_DV:], qd[h][r]], axis=0).astype(BF16)
            ws_ = _dot(wq, sh[h].astype(BF16))
            vn16 = (sol[h][r, :DN_DV] - ws_[:DN_CHUNK]).astype(BF16)
            o = ws_[DN_CHUNK:] + _dot(qk[h][r, r].astype(BF16), vn16)
            sh[h] = (sh[h] * jnp.exp(glast[h][c * DN_CHUNK:c * DN_CHUNK + 1, :])
                     + _dot_tn(kd[h][r].astype(BF16), vn16))
            z = pdn_ref[0, r, 3 * DN_WIDTH + h * DN_DV:3 * DN_WIDTH + (h + 1) * DN_DV].astype(F32)
            o = o * lax.rsqrt(jnp.mean(o * o, axis=-1, keepdims=True) + NORM_EPS) * nw_ref[...] * _silu(z)
            o_ref[0, r, h * DN_DV:(h + 1) * DN_DV] = o.astype(o_ref.dtype)
    for h in heads:
        s_scr[h] = sh[h]

    buf[0:8, :] = buf[tb:tb + 8, :]

    @pl.when(t == pl.num_programs(1) - 1)
    def _():
        s_out_ref[0] = s_scr[...]


def _dn_gate_rows(a_log, dt_bias):
    ab = jnp.zeros((a_log.shape[0], 2, LANES), F32)
    ab = ab.at[:, 0, DN_HEADS:2 * DN_HEADS].set(a_log)
    return ab.at[:, 1, DN_HEADS:2 * DN_HEADS].set(dt_bias)


def _deltanet_prompt(pdn, pba, conv_w, ab, norm_w):
    n, l, _ = pdn.shape
    assert l % DN_TB == 0
    return pl.pallas_call(
        _dn_body,
        grid=(n, l // DN_TB),
        in_specs=[
            pl.BlockSpec((1, DN_TB, 4 * DN_WIDTH), lambda i, t: (i, t, 0)),
            pl.BlockSpec((1, DN_TB, LANES), lambda i, t: (i, t, 0)),
            _wspec(conv_w), _wspec(ab), _wspec(norm_w),
        ],
        out_specs=[
            pl.BlockSpec((1, DN_TB, DN_WIDTH), lambda i, t: (i, t, 0)),
            pl.BlockSpec((1, DN_HEADS, DN_DK, DN_DV), lambda i, t: (i, 0, 0, 0)),
        ],
        out_shape=[
            jax.ShapeDtypeStruct((n, l, DN_WIDTH), BF16),
            jax.ShapeDtypeStruct((n, DN_HEADS, DN_DK, DN_DV), F32),
        ],
        scratch_shapes=[pltpu.VMEM((DN_TB + 8, 3 * DN_WIDTH), F32), pltpu.VMEM((DN_HEADS, DN_DK, DN_DV), F32)],
        compiler_params=_params("parallel", "arbitrary"),
        name="deltanet_prompt",
    )(pdn, pba, _warr(conv_w), _warr(ab), _warr(norm_w))


def _dn_step_body(pdn_ref, pba_ref, cbuf_ref, s0_ref, cw_ref, ab_ref, nw_ref, o_ref, s_out_ref):
    x = pdn_ref[0, :, 0:3 * DN_WIDTH].astype(F32)
    y = cw_ref[3:4, :] * x
    for j in range(1, DN_CONV):
        y = y + cw_ref[3 - j:4 - j, :] * cbuf_ref[0, 3 - j:4 - j, :]
    qkv = _silu(y)
    pba = pba_ref[0]
    beta_all = jax.nn.sigmoid(pba)
    g_all = -jnp.exp(ab_ref[0:1, :]) * jax.nn.softplus(pba + ab_ref[1:2, :])

    def l2norm(t):
        return t * lax.rsqrt(jnp.sum(t * t, axis=-1, keepdims=True) + NORM_EPS)

    sub = lax.broadcasted_iota(jnp.int32, (8, DN_DK), 0)
    for h in range(DN_HEADS):
        q = l2norm(qkv[:, h * DN_DK:(h + 1) * DN_DK]) * DN_DK ** -0.5
        k = l2norm(qkv[:, DN_WIDTH + h * DN_DK:DN_WIDTH + (h + 1) * DN_DK])
        v = qkv[:, 2 * DN_WIDTH + h * DN_DV:2 * DN_WIDTH + (h + 1) * DN_DV]
        beta = beta_all[:, h:h + 1]
        eg = jnp.exp(g_all[:, DN_HEADS + h:DN_HEADS + h + 1])
        s0 = s0_ref[0, h]
        w = k * beta * eg
        lhs = jnp.where(sub == 0, w, jnp.where(sub == 1, q * eg, 0.0))
        ws_ = jnp.dot(lhs, s0, precision=lax.Precision.HIGHEST, preferred_element_type=F32)
        vn = v * beta - ws_[0:1, :]
        o = ws_[1:2, :] + jnp.sum(q * k, axis=-1, keepdims=True) * vn
        kt = jnp.broadcast_to(k, (DN_DK, DN_DK)).T
        s_out_ref[0, h] = s0 * eg + kt * vn
        z = pdn_ref[0, :, 3 * DN_WIDTH + h * DN_DV:3 * DN_WIDTH + (h + 1) * DN_DV].astype(F32)
        o = o * lax.rsqrt(jnp.mean(o * o, axis=-1, keepdims=True) + NORM_EPS) * nw_ref[...] * _silu(z)
        o_ref[0, :, h * DN_DV:(h + 1) * DN_DV] = o.astype(o_ref.dtype)


def _deltanet_step(pdn, pba, conv_buf, s0, conv_w, ab, norm_w):
    n = pdn.shape[0]
    return pl.pallas_call(
        _dn_step_body,
        grid=(n,),
        in_specs=[
            pl.BlockSpec((1, 1, 4 * DN_WIDTH), lambda i: (i, 0, 0)),
            pl.BlockSpec((1, 1, LANES), lambda i: (i, 0, 0)),
            pl.BlockSpec((1, DN_CONV - 1, 3 * DN_WIDTH), lambda i: (i, 0, 0)),
            pl.BlockSpec((1, DN_HEADS, DN_DK, DN_DV), lambda i: (i, 0, 0, 0)),
            _wspec(conv_w), _wspec(ab), _wspec(norm_w),
        ],
        out_specs=[
            pl.BlockSpec((1, 1, DN_WIDTH), lambda i: (i, 0, 0)),
            pl.BlockSpec((1, DN_HEADS, DN_DK, DN_DV), lambda i: (i, 0, 0, 0)),
        ],
        out_shape=[
            jax.ShapeDtypeStruct((n, 1, DN_WIDTH), BF16),
            jax.ShapeDtypeStruct((n, DN_HEADS, DN_DK, DN_DV), F32),
        ],
        compiler_params=_params("parallel"),
        name="deltanet_step",
    )(pdn, pba, conv_buf, s0, _warr(conv_w), _warr(ab), _warr(norm_w))


def _s5_discretize(a_re, a_im, log_dt, b_re, b_im):
    dt = jnp.exp(log_dt)[:, None]
    mag = jnp.exp(a_re * dt)
    abr, abi = mag * jnp.cos(a_im * dt), mag * jnp.sin(a_im * dt)
    den = a_re * a_re + a_im * a_im
    fr = ((abr - 1.0) * a_re + abi * a_im) / den
    fi = (abi * a_re - (abr - 1.0) * a_im) / den
    bbr = fr[..., None] * b_re - fi[..., None] * b_im
    bbi = fr[..., None] * b_im + fi[..., None] * b_re
    return abr, abi, bbr, bbi


def _s5_powers(a_re, a_im, log_dt, taus):
    dt = jnp.exp(log_dt)[None, :, None]
    tau = jnp.asarray(taus, F32)[:, None, None]
    mag = jnp.exp(a_re[None] * dt * tau)
    ang = a_im[None] * dt * tau
    return mag * jnp.cos(ang), mag * jnp.sin(ang)


S5_GB = LANES // S5_GROUP
S5_NGB = S5_GROUPS // S5_GB
S5_K = S5_T * LANES
S5_HALF = S5_GB * S5_STATE
S5_ROWS = 256


def _s5_prompt_weights(a_re, a_im, log_dt, b_re, b_im, c_re, c_im):
    t = S5_T
    hp = lax.Precision.HIGHEST
    eye = jnp.eye(S5_GB, dtype=F32)
    _, _, bbr, bbi = _s5_discretize(a_re, a_im, log_dt, b_re, b_im)
    lr, li = _s5_powers(a_re, a_im, log_dt, list(range(t + 1)))
    lbr = lr[..., None] * bbr[None] - li[..., None] * bbi[None]
    lbi = lr[..., None] * bbi[None] + li[..., None] * bbr[None]
    kt = (jnp.einsum("ghp,tgpk->tghk", c_re, lbr, precision=hp)
          - jnp.einsum("ghp,tgpk->tghk", c_im, lbi, precision=hp))
    lag = np.arange(t)[None, :] - np.arange(t)[:, None]
    kblk = kt.transpose(0, 1, 3, 2).reshape(t + 1, S5_NGB, S5_GB, S5_GROUP, S5_GROUP)
    kblk = jnp.einsum("tBaph,am->tBapmh", kblk, eye).reshape(t + 1, S5_NGB, LANES, LANES).astype(BF16)
    toep = jnp.where((lag >= 0)[:, :, None, None, None], kblk[np.clip(lag, 0, t)], 0.0)
    toep = toep.transpose(2, 0, 3, 1, 4).reshape(S5_NGB, S5_K, S5_K)
    sel = t - 1 - np.arange(t)
    wz = jnp.stack([lbr[sel], lbi[sel]])
    wz = wz.transpose(2, 1, 4, 0, 3).reshape(S5_NGB, S5_GB, t, S5_GROUP, 2, S5_STATE)
    wz = jnp.einsum("Baipcq,am->Biapcmq", wz.astype(BF16), eye.astype(BF16)).reshape(S5_NGB, S5_K, 2 * S5_HALF)
    lr1, li1 = lr[1:], li[1:]
    wy = jnp.stack([c_re[None] * lr1[:, :, None, :] - c_im[None] * li1[:, :, None, :],
                    -(c_re[None] * li1[:, :, None, :] + c_im[None] * lr1[:, :, None, :])])
    wy = wy.transpose(2, 0, 4, 1, 3).reshape(S5_NGB, S5_GB, 2, S5_STATE, t, S5_GROUP)
    wy = jnp.einsum("Bacpjq,am->Bcapjmq", wy.astype(BF16), eye.astype(BF16)).reshape(S5_NGB, 2 * S5_HALF, S5_K)
    n_steps = int(math.log2(S5_ROWS))
    sr, si = _s5_powers(a_re, a_im, log_dt, [t * (1 << i) for i in range(n_steps)])
    blocks = lambda v: v.reshape(n_steps, S5_NGB, S5_HALF).transpose(1, 0, 2)
    sc_r = jnp.concatenate([blocks(sr), blocks(sr)], axis=-1)
    sc_i = jnp.concatenate([-blocks(si), blocks(si)], axis=-1)
    return toep, wz, wy, sc_r, sc_i


def _s5_body(u_ref, toep_ref, wz_ref, wy_ref, lr_ref, li_ref, y_ref, sfin_ref, carry_s):
    rb, b = pl.program_id(1), pl.program_id(2)
    lanes = [pl.ds(pl.multiple_of(t * S5_WIDTH + b * LANES, LANES), LANES) for t in range(S5_T)]
    u = jnp.concatenate([u_ref[:, lanes[t]] for t in range(S5_T)], axis=1).astype(BF16)

    @pl.when(rb == 0)
    def _():
        carry_s[b] = jnp.zeros((1, 2 * S5_HALF), F32)

    swap = lambda v: pltpu.roll(v, S5_HALF, 1)
    z = _dot(u, wz_ref[...])
    rowi = lax.broadcasted_iota(jnp.int32, z.shape, 0)
    carry = carry_s[b]
    s = z + jnp.where(rowi == 0, lr_ref[0:1, :] * carry + li_ref[0:1, :] * swap(carry), 0.0)
    for i in range(lr_ref.shape[0]):
        d = 1 << i
        sh = jnp.where(rowi >= d, pltpu.roll(s, d, 0), 0.0)
        s = s + lr_ref[i:i + 1, :] * sh + li_ref[i:i + 1, :] * swap(sh)
    last = s[z.shape[0] - 1:z.shape[0], :]
    sprev = jnp.where(rowi >= 1, pltpu.roll(s, 1, 0), carry)
    carry_s[b] = last
    sfin_ref[b] = last
    y = _dot(u, toep_ref[...]) + _dot(sprev.astype(BF16), wy_ref[...])
    for t in range(S5_T):
        y_ref[:, lanes[t]] = y[:, t * LANES:(t + 1) * LANES]


def _s5_prompt(u, wts):
    n, l, _ = u.shape
    nc = l // S5_T
    assert nc % S5_ROWS == 0
    width = S5_T * S5_WIDTH
    wspec = lambda t: pl.BlockSpec((None, None) + t.shape[1:], lambda i, r, b: (t.layer, b, 0, 0))
    y, sfin = pl.pallas_call(
        _s5_body,
        grid=(n, nc // S5_ROWS, S5_NGB),
        in_specs=[pl.BlockSpec((None, S5_ROWS, width), lambda i, r, b: (i, r, 0)),
                  *map(wspec, wts)],
        out_specs=[pl.BlockSpec((None, S5_ROWS, width), lambda i, r, b: (i, r, 0)),
                   pl.BlockSpec((None, S5_NGB, 1, 2 * S5_HALF), lambda i, r, b: (i, 0, 0, 0))],
        out_shape=[jax.ShapeDtypeStruct((n, nc, width), F32),
                   jax.ShapeDtypeStruct((n, S5_NGB, 1, 2 * S5_HALF), F32)],
        scratch_shapes=[pltpu.VMEM((S5_NGB, 1, 2 * S5_HALF), F32)],
        compiler_params=_params("parallel", "arbitrary", "arbitrary"),
        name="s5_prompt",
    )(u.reshape(n, nc, width), *map(_warr, wts))
    sfin = sfin.reshape(n, S5_NGB, 2, S5_GB, S5_STATE).transpose(2, 0, 1, 3, 4).reshape(2, n, S5_GROUPS, S5_STATE)
    return y.reshape(n, l, S5_WIDTH), sfin[0], sfin[1]


def _s5_step_weights(a_re, a_im, log_dt, b_re, b_im, c_re, c_im):
    abr, abi, bbr, bbi = _s5_discretize(a_re, a_im, log_dt, b_re, b_im)
    eye = jnp.eye(S5_GROUPS, dtype=F32)
    gp = S5_GROUPS * S5_STATE
    wb = lambda b: jnp.einsum("gph,gk->ghkp", b, eye).reshape(S5_WIDTH, gp).astype(BF16)
    wc = lambda c: jnp.einsum("ghp,gk->gpkh", c, eye).reshape(gp, S5_WIDTH).astype(BF16)
    return (abr.reshape(1, gp), abi.reshape(1, gp), wb(bbr), wb(bbi), wc(c_re), wc(c_im))


def _s5_step_body(u_ref, s0r_ref, s0i_ref, abr_ref, abi_ref, wbr, wbi, wcr, wci, y_ref, sr_ref, si_ref):
    u = u_ref[...].astype(BF16)
    abr, abi = abr_ref[...], abi_ref[...]
    s0r, s0i = s0r_ref[...], s0i_ref[...]
    sr = abr * s0r - abi * s0i + _dot(u, wbr[...])
    si = abr * s0i + abi * s0r + _dot(u, wbi[...])
    sr_ref[...] = sr
    si_ref[...] = si
    y_ref[...] = _dot(sr.astype(BF16), wcr[...]) - _dot(si.astype(BF16), wci[...])


def _s5_step(u, s0_re, s0_im, wts):
    n = u.shape[0]
    gp = S5_GROUPS * S5_STATE
    args = (u, s0_re.reshape(n, gp), s0_im.reshape(n, gp)) + tuple(wts)
    y, sr, si = pl.pallas_call(
        _s5_step_body,
        grid=(1,),
        in_specs=[_wspec(a) for a in args],
        out_specs=[_const_spec((n, S5_WIDTH)), _const_spec((n, gp)), _const_spec((n, gp))],
        out_shape=[jax.ShapeDtypeStruct((n, S5_WIDTH), F32), jax.ShapeDtypeStruct((n, gp), F32),
                   jax.ShapeDtypeStruct((n, gp), F32)],
        compiler_params=_params("arbitrary"),
        name="s5_step",
    )(*map(_warr, args))
    return y, sr.reshape(n, S5_GROUPS, S5_STATE), si.reshape(n, S5_GROUPS, S5_STATE)


def _top_mask(gate, pos, count, axis=-1):
    sel = jnp.zeros(gate.shape, F32)
    for _ in range(count):
        m = jnp.max(gate, axis=axis, keepdims=True)
        idx = jnp.min(jnp.where(gate == m, pos, 2 ** 30), axis=axis, keepdims=True)
        hit = jnp.logical_and(pos == idx, m > -jnp.inf)
        sel = jnp.where(hit, 1.0, sel)
        gate = jnp.where(pos == idx, -jnp.inf, gate)
    return sel


def _alibi_slopes():
    s = np.exp2(-8.0 * np.arange(1, MB_HEADS + 1, dtype=np.float32) / MB_HEADS).astype(np.float32)
    return jnp.asarray(np.broadcast_to(s[:, None, None], (MB_HEADS, 1, LANES)))


MB_SLOTS = 16
MB_V_ROWS = MB_HD + 16
MB_GROUP = 2
MB_HEADS_PER_STEP = 4


def _moba_body(q_ref, k_ref, v_ref, slope_ref, o_ref, kaug_s, vaug_s, means_s, *, nb):
    blk = MB_BLOCK
    qi = pl.program_id(2)
    heads = range(MB_HEADS_PER_STEP)
    cols = [slice(h * MB_HD, (h + 1) * MB_HD) for h in heads]
    slope = [slope_ref[h, :, 0:1] for h in heads]

    @pl.when(qi == 0)
    def _():
        lane = lax.broadcasted_iota(jnp.int32, (blk, MB_HD), 1)
        off = lax.broadcasted_iota(jnp.int32, (blk, MB_HD), 0).astype(F32)
        means_s[...] = jnp.zeros(means_s.shape, F32)
        vaug_s[:, :, MB_HD:, :] = jnp.ones((len(heads), nb, MB_V_ROWS - MB_HD, blk), BF16)
        for b in range(nb):
            for h in heads:
                kb = k_ref[0, b * blk:(b + 1) * blk, cols[h]]
                means_s[h, b:b + 1, :] = jnp.mean(kb, axis=0, keepdims=True)
                feat = jnp.where(lane == b, 1.0, 0.0)
                feat = jnp.where(lane == MB_SLOTS, slope[h] * off, feat)
                feat = jnp.where(lane == MB_SLOTS + 1, slope[h] * float(b * blk), feat)
                feat = jnp.where(lane == MB_SLOTS + 2, 1.0, feat)
                kaug_s[h, b] = jnp.concatenate([kb.astype(BF16), feat.astype(BF16)], axis=1)
                vaug_s[h, b, 0:MB_HD, :] = v_ref[0, b * blk:(b + 1) * blk, cols[h]].T.astype(BF16)

    slot = lax.broadcasted_iota(jnp.int32, (MB_SLOTS, blk), 0)
    pad = jnp.zeros((MB_HD - 2 * MB_SLOTS, blk), BF16)
    qt = [q_ref[0, :, cols[h]].astype(F32).T for h in heads]
    gate = [jnp.dot(means_s[h], qt[h], precision=lax.Precision.HIGHEST, preferred_element_type=F32) for h in heads]
    sel = [_top_mask(jnp.where(slot < qi, gate[h], -jnp.inf), slot, MB_TOPK, axis=0) for h in heads]
    bias_rows = [jnp.where(slot < 2, 1.0, jnp.where(slot == 2, -slope[h] * (qi * blk).astype(F32), 0.0)).astype(BF16)
                 for h in heads]
    qt16 = [qt[h].astype(BF16) for h in heads]
    q_past = [jnp.concatenate([qt16[h], jnp.where(sel[h] > 0.0, 0.0, NEG).astype(BF16), bias_rows[h], pad], axis=0)
              for h in heads]
    q_own = [jnp.concatenate([qt16[h], jnp.zeros((MB_SLOTS, blk), BF16), bias_rows[h], pad], axis=0) for h in heads]

    key = lax.broadcasted_iota(jnp.int32, (blk, blk), 0)
    qry = lax.broadcasted_iota(jnp.int32, (blk, blk), 1)
    s = [jnp.where(key <= qry, _dot(kaug_s[h, qi], q_own[h]), NEG) for h in heads]
    m = [jnp.max(s[h], axis=0, keepdims=True) for h in heads]
    acc = [_dot(vaug_s[h, qi], jnp.exp(s[h] - m[h]).astype(BF16)) for h in heads]

    grp = MB_GROUP

    def body(i, carry):
        m_old, acc = carry
        s = [[_dot(kaug_s[h, grp * i + g], q_past[h]) for g in range(grp)] for h in heads]
        m_new = list(m_old)
        for g in range(grp):
            m_new = [jnp.maximum(m_new[h], jnp.max(s[h][g], axis=0, keepdims=True)) for h in heads]
        p = [[jnp.exp(s[h][g] - m_new[h]).astype(BF16) for g in range(grp)] for h in heads]
        acc = [jnp.exp(m_old[h] - m_new[h]) * acc[h] for h in heads]
        for g in range(grp):
            acc = [acc[h] + _dot(vaug_s[h, grp * i + g], p[h][g]) for h in heads]
        return tuple(m_new), tuple(acc)

    m, acc = lax.fori_loop(0, (qi + grp - 1) // grp, body, (tuple(m), tuple(acc)))
    for h in heads:
        o_ref[0, :, cols[h]] = (acc[h][0:MB_HD] / acc[h][MB_HD:MB_HD + 1]).T.astype(o_ref.dtype)


def _moba_prompt(q, k, v):
    n, l, _ = q.shape
    nb = l // MB_BLOCK
    hps = MB_HEADS_PER_STEP
    width = hps * MB_HD
    assert l % MB_BLOCK == 0 and nb <= MB_SLOTS and nb % MB_GROUP == 0 and MB_HEADS % hps == 0
    kv_spec = pl.BlockSpec((1, l, width), lambda i, h, t: (i, 0, h))
    return pl.pallas_call(
        functools.partial(_moba_body, nb=nb),
        grid=(n, MB_HEADS // hps, nb),
        in_specs=[
            pl.BlockSpec((1, MB_BLOCK, width), lambda i, h, t: (i, t, h)),
            kv_spec, kv_spec,
            pl.BlockSpec((hps, 1, LANES), lambda i, h, t: (h, 0, 0)),
        ],
        out_specs=pl.BlockSpec((1, MB_BLOCK, width), lambda i, h, t: (i, t, h)),
        out_shape=jax.ShapeDtypeStruct((n, l, MB_WIDTH), BF16),
        scratch_shapes=[pltpu.VMEM((hps, nb, MB_BLOCK, 2 * MB_HD), BF16),
                        pltpu.VMEM((hps, nb, MB_V_ROWS, MB_BLOCK), BF16),
                        pltpu.VMEM((hps, MB_SLOTS, MB_HD), F32)],
        compiler_params=_params("parallel", "parallel", "arbitrary"),
        name="moba_prompt",
    )(q, k, v, _alibi_slopes())


PAGES_PER_STEP = 32
PAGES_PER_BLOCK = MB_BLOCK // PAGE_SIZE


def _page_sum_body(pt_ref, *refs):
    del pt_ref
    o_ref = refs[-1]
    for b in range(PAGES_PER_STEP // PAGES_PER_BLOCK):
        acc = jnp.sum(refs[PAGES_PER_BLOCK * b][...], axis=0)
        for j in range(1, PAGES_PER_BLOCK):
            acc = acc + jnp.sum(refs[PAGES_PER_BLOCK * b + j][...], axis=0)
        o_ref[b] = acc


def _block_key_sums(pool, page_table, layer):
    n, n_pages = page_table.shape
    assert n_pages % PAGES_PER_STEP == 0
    tail = pool.shape[2:]

    def page_spec(j):
        return pl.BlockSpec((None, None) + tail, lambda i, s, pt: (layer, pt[i, s * PAGES_PER_STEP + j], 0, 0, 0))

    blocks_per_step = PAGES_PER_STEP // PAGES_PER_BLOCK
    return pl.pallas_call(
        _page_sum_body,
        grid_spec=pltpu.PrefetchScalarGridSpec(
            num_scalar_prefetch=1,
            grid=(n, n_pages // PAGES_PER_STEP),
            in_specs=[page_spec(j) for j in range(PAGES_PER_STEP)],
            out_specs=pl.BlockSpec((None, blocks_per_step) + tail[1:], lambda i, s, pt: (i, s, 0, 0)),
        ),
        out_shape=jax.ShapeDtypeStruct((n, n_pages // PAGES_PER_BLOCK) + tail[1:], F32),
        compiler_params=_params("parallel", "arbitrary"),
        name="moba_page_sums",
    )(page_table, *([pool] * PAGES_PER_STEP))


def _moba_pick_body(q_ref, sums_ref, sel_ref):
    nbk = sums_ref.shape[0]
    out = jnp.zeros(sel_ref.shape, jnp.int32)
    osub = lax.broadcasted_iota(jnp.int32, out.shape, 0)
    olane = lax.broadcasted_iota(jnp.int32, out.shape, 1)
    bidx = lax.broadcasted_iota(jnp.int32, (nbk, 1), 0)
    for h in range(MB_HEADS):
        means = sums_ref[:, h, :] / float(MB_BLOCK)
        q = q_ref[:, h * MB_HD:(h + 1) * MB_HD].astype(F32)
        gate = jnp.sum(means * q, axis=-1, keepdims=True)
        for rnk in range(MB_TOPK):
            m = jnp.max(gate, axis=0, keepdims=True)
            idx = jnp.min(jnp.where(gate == m, bidx, 2 ** 30), axis=0, keepdims=True)
            out = jnp.where(jnp.logical_and(osub == h, olane == rnk), idx, out)
            gate = jnp.where(bidx == idx, -jnp.inf, gate)
    sel_ref[...] = out


def _moba_pick(q, sums):
    n = q.shape[0]
    return pl.pallas_call(
        _moba_pick_body,
        grid=(n,),
        in_specs=[pl.BlockSpec((None, 1, MB_WIDTH), lambda i: (i, 0, 0)),
                  pl.BlockSpec((None,) + sums.shape[1:], lambda i: (i, 0, 0, 0))],
        out_specs=pl.BlockSpec((None, 8, LANES), lambda i: (i, 0, 0)),
        out_shape=jax.ShapeDtypeStruct((n, 8, LANES), jnp.int32),
        compiler_params=_params("parallel"),
        name="moba_pick",
    )(q, sums)


def _moba_step_body(sel_ref, pt_ref, q_ref, kn_ref, vn_ref, slope_ref, *refs, past):
    del pt_ref
    pages, o_ref, m_s, l_s, acc_s = refs[:4 * MB_HEADS], refs[4 * MB_HEADS], *refs[4 * MB_HEADS + 1:]
    i, s = pl.program_id(0), pl.program_id(1)
    sub = lax.broadcasted_iota(jnp.int32, (PAGE_SIZE, 1), 0)
    for h in range(MB_HEADS):
        q = q_ref[0, h].astype(F32)
        slope = slope_ref[h, :, 0:1]

        @pl.when(s == 0)
        def _():
            m_s[h] = jnp.sum(q * kn_ref[0, h], axis=-1, keepdims=True)
            l_s[h] = jnp.ones((1, 1), F32)
            acc_s[h] = vn_ref[0, h]

        blk = sel_ref[(i * MB_HEADS + h) * MB_TOPK + s]
        k0, k1, v0, v1 = pages[4 * h:4 * h + 4]
        m_old = m_s[h]
        sc = []
        for half, kr in enumerate((k0, k1)):
            kpos = blk * MB_BLOCK + half * PAGE_SIZE + sub
            sc.append(jnp.sum(kr[:, h, :] * q, axis=-1, keepdims=True) - slope * (past - kpos).astype(F32))
        m_new = jnp.maximum(m_old, jnp.maximum(jnp.max(sc[0], axis=0, keepdims=True),
                                               jnp.max(sc[1], axis=0, keepdims=True)))
        alpha = jnp.exp(m_old - m_new)
        p0 = jnp.exp(sc[0] - m_new)
        p1 = jnp.exp(sc[1] - m_new)
        l_new = alpha * l_s[h] + jnp.sum(p0, axis=0, keepdims=True) + jnp.sum(p1, axis=0, keepdims=True)
        acc_new = (alpha * acc_s[h] + jnp.sum(p0 * v0[:, h, :], axis=0, keepdims=True)
                   + jnp.sum(p1 * v1[:, h, :], axis=0, keepdims=True))
        m_s[h] = m_new
        l_s[h] = l_new
        acc_s[h] = acc_new

        @pl.when(s == MB_TOPK - 1)
        def _():
            o_ref[0, h] = (acc_new / l_new).astype(o_ref.dtype)


def _moba_step(q, k_new, v_new, sel, page_table, k_pool, v_pool, layer):
    n = q.shape[0]
    past = page_table.shape[1] * PAGE_SIZE
    assert PAGES_PER_BLOCK == 2 and past // MB_BLOCK >= MB_TOPK
    tail = k_pool.shape[2:]

    def page_spec(h, half):
        def imap(i, s, sel_r, pt_r):
            blk = sel_r[(i * MB_HEADS + h) * MB_TOPK + s]
            return (layer, pt_r[i, blk * PAGES_PER_BLOCK + half], 0, 0, 0)
        return pl.BlockSpec((None, None) + tail, imap)

    page_specs, page_args = [], []
    for h in range(MB_HEADS):
        for pool in (k_pool, v_pool):
            for half in range(PAGES_PER_BLOCK):
                page_specs.append(page_spec(h, half))
                page_args.append(pool)
    tok = pl.BlockSpec((1, MB_HEADS, 1, MB_HD), lambda i, s, *_: (i, 0, 0, 0))
    return pl.pallas_call(
        functools.partial(_moba_step_body, past=past),
        grid_spec=pltpu.PrefetchScalarGridSpec(
            num_scalar_prefetch=2,
            grid=(n, MB_TOPK),
            in_specs=[tok, tok, tok, _const_spec((MB_HEADS, 1, LANES))] + page_specs,
            out_specs=tok,
            scratch_shapes=[pltpu.VMEM((MB_HEADS, 1, 1), F32), pltpu.VMEM((MB_HEADS, 1, 1), F32),
                            pltpu.VMEM((MB_HEADS, 1, MB_HD), F32)],
        ),
        out_shape=jax.ShapeDtypeStruct((n, MB_HEADS, 1, MB_HD), BF16),
        compiler_params=_params("parallel", "arbitrary"),
        name="moba_step",
    )(sel, page_table, q, k_new, v_new, _alibi_slopes(), *page_args)


def _merge_body(x_ref, a_ref, y5_ref, u_ref, c_ref, g_ref, d_ref, gluw, glub, wa, wb, wc, wo, lng, lnb, h_ref,
                *, alpha):
    y = jax.nn.gelu(y5_ref[...] + d_ref[...] * u_ref[...])
    b = y * jax.nn.sigmoid(_dot(y.astype(BF16), gluw[...]) + glub[...])
    dm = x_ref.shape[1]
    gate = lambda j: jax.nn.sigmoid(g_ref[:, j * dm:(j + 1) * dm].astype(F32))
    merged = (gate(0) * _dot(a_ref[...], wa[...]) + gate(1) * _dot(b.astype(BF16), wb[...])
              + gate(2) * _dot(c_ref[...], wc[...]))
    r = alpha * x_ref[...] + _dot(merged.astype(BF16), wo[...])
    h_ref[...] = _layer_norm(r, lng[...], lnb[...])


def _merge(x, a, y5, u, c, g, w, alpha, tm):
    m, dm = x.shape
    acts = (x, a, y5, u, c, g)
    consts = (w["s5_d"], w["glu_w"], w["glu_b"], w["wa"], w["wb"], w["wc"], w["wo"], w["ln1_g"], w["ln1_b"])
    return pl.pallas_call(
        functools.partial(_merge_body, alpha=alpha),
        grid=(m // tm,),
        in_specs=[pl.BlockSpec((tm, t.shape[1]), lambda i: (i, 0)) for t in acts]
        + [_wspec(t) for t in consts],
        out_specs=pl.BlockSpec((tm, dm), lambda i: (i, 0)),
        out_shape=jax.ShapeDtypeStruct((m, dm), F32),
        compiler_params=_params("parallel"),
        name="merge",
    )(*acts, *map(_warr, consts))


def _moe_body(h_ref, rw, rb, wg, wu, wd, sgu, sd, lng, lnb, y_ref, xb_s, gate_s, acc_s, *, alpha):
    e = pl.program_id(1)
    lane = lax.broadcasted_iota(jnp.int32, gate_s.shape, 1)

    @pl.when(e == 0)
    def _():
        xb = h_ref[...].astype(BF16)
        xb_s[...] = xb
        scores = jax.nn.sigmoid(_dot_nt(rw[...], xb))
        slot = lax.broadcasted_iota(jnp.int32, scores.shape, 0)
        ranked = jnp.where(slot < N_EXPERTS, scores + rb[...], -jnp.inf)
        picked = _top_mask(ranked, slot, TOP_K, axis=0) * scores
        gate_s[...] = (picked / jnp.sum(picked, axis=0, keepdims=True) * ROUTED_SCALE).T
        hs = _dot(xb, sgu[...])
        ff = sd.shape[0]
        acc_s[...] = _dot((_silu(hs[:, :ff]) * hs[:, ff:]).astype(BF16), sd[...])

    xb = xb_s[...]
    gcol = jnp.sum(jnp.where(lane == e, gate_s[...], 0.0), axis=-1, keepdims=True)
    hm = _silu(_dot(xb, wg[...])) * _dot(xb, wu[...]) * gcol
    acc_s[...] += _dot(hm.astype(BF16), wd[...])

    @pl.when(e == pl.num_programs(1) - 1)
    def _():
        y_ref[...] = _layer_norm(alpha * h_ref[...] + acc_s[...], lng[...], lnb[...])


def _moe(h, w, alpha, tm):
    m, dm = h.shape
    consts_a = (w["router_w"], w["router_b"])
    consts_b = (w["sh_gu"], w["sh_d"], w["ln2_g"], w["ln2_b"])
    layer = w["layer"]
    return pl.pallas_call(
        functools.partial(_moe_body, alpha=alpha),
        grid=(m // tm, N_EXPERTS),
        in_specs=[pl.BlockSpec((tm, dm), lambda i, e: (i, 0), pipeline_mode=pl.Buffered(1))]
        + [_wspec(t) for t in consts_a]
        + [pl.BlockSpec((None, None, dm, EXPERT_FF), lambda i, e: (layer, e, 0, 0)),
           pl.BlockSpec((None, None, dm, EXPERT_FF), lambda i, e: (layer, e, 0, 0)),
           pl.BlockSpec((None, None, EXPERT_FF, dm), lambda i, e: (layer, e, 0, 0))]
        + [_wspec(t) for t in consts_b],
        out_specs=pl.BlockSpec((tm, dm), lambda i, e: (i, 0), pipeline_mode=pl.Buffered(1)),
        out_shape=jax.ShapeDtypeStruct((m, dm), F32),
        scratch_shapes=[pltpu.VMEM((tm, dm), BF16), pltpu.VMEM((tm, LANES), F32), pltpu.VMEM((tm, dm), F32)],
        compiler_params=_params("parallel", "arbitrary"),
        name="moe",
    )(h, *map(_warr, consts_a), w["exp_g"], w["exp_u"], w["exp_d"], *map(_warr, consts_b))


def _stacked_weights(p):
    w_in = p["w_in"]
    o = 0
    w = {}
    for name, size in (("dn", 4 * DN_WIDTH), ("ba", 2 * DN_HEADS), ("s5", S5_WIDTH), ("mq", MB_WIDTH),
                       ("k", MB_WIDTH), ("v", MB_WIDTH), ("g", 3 * w_in.shape[1])):
        w["w_" + name] = w_in[:, :, o:o + size].astype(BF16)
        o += size
    w["w_ba"] = jnp.pad(w["w_ba"], ((0, 0), (0, 0), (0, LANES - 2 * DN_HEADS)))
    row = lambda v: v[:, None, :]
    w["conv_w"] = p["dn_conv_w"]
    w["dn_ab"] = _dn_gate_rows(p["dn_a_log"], p["dn_dt_bias"])
    w["dn_norm_w"] = row(p["dn_norm_w"])
    s5 = tuple(p[k] for k in ("s5_a_re", "s5_a_im", "s5_log_dt", "s5_b_re", "s5_b_im", "s5_c_re", "s5_c_im"))
    w["s5_prompt"] = jax.vmap(_s5_prompt_weights)(*s5)
    w["s5_step"] = jax.vmap(_s5_step_weights)(*s5)
    w["s5_d"] = row(p["s5_d"])
    w["glu_w"] = p["s5_glu_w"].astype(BF16)
    w["glu_b"] = row(p["s5_glu_b"])
    w["wa"] = p["w_branch_a"].astype(BF16)
    w["wb"] = p["w_branch_b"].astype(BF16)
    w["wc"] = p["w_branch_c"].astype(BF16)
    w["wo"] = p["w_out"].astype(BF16)
    for k in ("ln1_g", "ln1_b", "ln2_g", "ln2_b"):
        w[k] = row(p[k])
    w["router_w"] = jnp.pad(p["router_w"].astype(BF16), ((0, 0), (0, 0), (0, LANES - N_EXPERTS))).transpose(0, 2, 1)
    w["router_b"] = jnp.pad(p["router_bias"], ((0, 0), (0, LANES - N_EXPERTS)))[:, :, None]
    w["sh_gu"] = jnp.concatenate([p["sh_w_gate"], p["sh_w_up"]], axis=-1).astype(BF16)
    w["sh_d"] = p["sh_w_down"].astype(BF16)
    w["exp_g"], w["exp_u"], w["exp_d"] = (p[k].astype(BF16) for k in ("exp_w_gate", "exp_w_up", "exp_w_down"))
    return w


def _layer_weights(l, stacked):
    view = lambda t: tuple(_Layered(a, l) for a in t) if isinstance(t, tuple) else _Layered(t, l)
    w = {k: (v if k.startswith("exp_") else view(v)) for k, v in stacked.items()}
    w["layer"] = l
    return w


def _prompt_layer(x, w, alpha, tm, tm_moe):
    n, l, dm = x.shape
    x2 = x.reshape(n * l, dm)
    pdn, pba, ps5, pmq, pk, pv, pg = _proj(x2, w, tm)
    a_out, s_fin = _deltanet_prompt(pdn.reshape(n, l, -1), pba.reshape(n, l, -1), w["conv_w"], w["dn_ab"],
                                    w["dn_norm_w"])
    y5, s5_re, s5_im = _s5_prompt(ps5.reshape(n, l, -1), w["s5_prompt"])
    c_out = _moba_prompt(pmq.reshape(n, l, -1), pk.reshape(n, l, -1), pv.reshape(n, l, -1))
    pk, pv = pk.reshape(n, l, MB_HEADS, MB_HD), pv.reshape(n, l, MB_HEADS, MB_HD)
    h = _merge(x2, a_out.reshape(n * l, -1), y5.reshape(n * l, -1), ps5, c_out.reshape(n * l, -1), pg, w, alpha, tm)
    y = _moe(h, w, alpha, tm_moe)
    conv_new = pdn.reshape(n, l, -1)[:, l - (DN_CONV - 1):, :3 * DN_WIDTH].astype(F32)
    return y.reshape(n, l, dm), (pk, pv, conv_new, s_fin, s5_re, s5_im)


def _sample_layer(x, w, alpha, conv_buf, s_dn, s5_re, s5_im, k_pool, v_pool, page_table, layer):
    n, l, dm = x.shape
    assert l == 1
    x2 = x.reshape(n, dm)
    pdn, pba, ps5, pmq, pk, pv, pg = _proj(x2, w, n, dn_dtype=F32)
    a_out, s_new = _deltanet_step(pdn.reshape(n, 1, -1), pba.reshape(n, 1, -1), conv_buf, s_dn, w["conv_w"],
                                  w["dn_ab"], w["dn_norm_w"])
    y5, s5_re_new, s5_im_new = _s5_step(ps5, s5_re, s5_im, w["s5_step"])
    sums = _block_key_sums(k_pool, page_table, layer)
    sel = _moba_pick(pmq.reshape(n, 1, -1), sums)
    sel = sel[:, :MB_HEADS, :MB_TOPK].reshape(-1)
    heads = lambda t: t.reshape(n, MB_HEADS, 1, MB_HD)
    c_out = _moba_step(heads(pmq), heads(pk), heads(pv), sel, page_table, k_pool, v_pool, layer)
    h = _merge(x2, a_out.reshape(n, -1), y5, ps5, c_out.reshape(n, -1), pg, w, alpha, n)
    y = _moe(h, w, alpha, n)
    conv_new = jnp.concatenate([conv_buf[:, 1:], pdn[:, None, :3 * DN_WIDTH].astype(F32)], axis=1)
    heads4 = lambda t: t.reshape(n, 1, MB_HEADS, MB_HD)
    return y.reshape(n, 1, dm), (heads4(pk), heads4(pv), conv_new, s_new, s5_re_new, s5_im_new)


def kernel(x_prompt, x_sample, cache_k, cache_v, page_table, state_dn_conv, state_dn, state_s5_re, state_s5_im, w_in, dn_conv_w, dn_a_log, dn_dt_bias, dn_norm_w, s5_a_re, s5_a_im, s5_log_dt, s5_b_re, s5_b_im, s5_c_re, s5_c_im, s5_d, s5_glu_w, s5_glu_b, w_branch_a, w_branch_b, w_branch_c, w_out, ln1_g, ln1_b, router_w, router_bias, exp_w_gate, exp_w_up, exp_w_down, sh_w_gate, sh_w_up, sh_w_down, ln2_g, ln2_b):
    p = dict(w_in=w_in, dn_conv_w=dn_conv_w, dn_a_log=dn_a_log, dn_dt_bias=dn_dt_bias, dn_norm_w=dn_norm_w,
             s5_a_re=s5_a_re, s5_a_im=s5_a_im, s5_log_dt=s5_log_dt, s5_b_re=s5_b_re, s5_b_im=s5_b_im,
             s5_c_re=s5_c_re, s5_c_im=s5_c_im, s5_d=s5_d, s5_glu_w=s5_glu_w, s5_glu_b=s5_glu_b,
             w_branch_a=w_branch_a, w_branch_b=w_branch_b, w_branch_c=w_branch_c, w_out=w_out,
             ln1_g=ln1_g, ln1_b=ln1_b, router_w=router_w, router_bias=router_bias, exp_w_gate=exp_w_gate,
             exp_w_up=exp_w_up, exp_w_down=exp_w_down, sh_w_gate=sh_w_gate, sh_w_up=sh_w_up,
             sh_w_down=sh_w_down, ln2_g=ln2_g, ln2_b=ln2_b)
    stacked = _stacked_weights(p)
    depth = w_in.shape[0]
    alpha = (2 * depth) ** 0.25
    seq = x_prompt.shape[1]
    tm = min(512, x_prompt.shape[0] * seq)
    tm_moe = min(2048, x_prompt.shape[0] * seq)
    hp, hs = x_prompt, x_sample
    p_st, s_st = [], []
    for l in range(depth):
        w = _layer_weights(l, stacked)
        hp, st = _prompt_layer(hp, w, alpha, tm, tm_moe)
        p_st.append(st)
        hs, st = _sample_layer(hs, w, alpha, state_dn_conv[l], state_dn[l], state_s5_re[l], state_s5_im[l],
                               cache_k, cache_v, page_table, l)
        s_st.append(st)
    p_out = [jnp.stack(t) for t in zip(*p_st)]
    s_out = [jnp.stack(t) for t in zip(*s_st)]
    return (hp, hs, *p_out, *s_out)
```
